```python
import math
import jax, jax.numpy as jnp
from jax import lax
import numpy as np

D_MODEL = 2048
BATCH = 2
SEQ = 4096
DEPTH = 4
DEC_BATCH = 32
DEC_SEQ = 8
PAST_LEN = 16384
PAGE_SIZE = 128

N_A_LAYERS = DEPTH // 2
N_B_LAYERS = DEPTH - N_A_LAYERS
HGRN_HEADS = 16
HGRN_DK = D_MODEL // HGRN_HEADS
HGRN_DV = D_MODEL // HGRN_HEADS
HGRN_CHUNK = 16
ATTN_HEAD_DIM = 64
ATTN_Q_HEADS = D_MODEL // ATTN_HEAD_DIM
ATTN_KV_HEADS = 8
GQA_GROUP = ATTN_Q_HEADS // ATTN_KV_HEADS
WINDOW = 128
ROPE_THETA = 10000.0
ATTN_SCALE = ATTN_HEAD_DIM ** -0.5
D_FF = -(-8 * D_MODEL // (3 * 256)) * 256
RMS_EPS = 1e-6
MASK_VALUE = -1e30
MIN_FORGET = 1e-30

kernel_name = 'hybrid_hgrn2_yoco_swa_sink_step'


def rms_norm(x, g):
    xf = x.astype(jnp.float32)
    y = xf * lax.rsqrt(jnp.mean(xf * xf, axis=-1, keepdims=True) + RMS_EPS)
    return (y * g.astype(jnp.float32)).astype(x.dtype)


def rope(x, pos):
    half = x.shape[-1] // 2
    inv = ROPE_THETA ** (-jnp.arange(half, dtype=jnp.float32) / half)
    ang = pos.astype(jnp.float32)[:, None] * inv[None, :]
    cos = jnp.cos(ang)[:, None, :]
    sin = jnp.sin(ang)[:, None, :]
    xf = x.astype(jnp.float32)
    x1, x2 = xf[..., :half], xf[..., half:]
    return jnp.concatenate([x1 * cos - x2 * sin, x2 * cos + x1 * sin], axis=-1).astype(x.dtype)


def swiglu(h, w_gate, w_up, w_down):
    return (jax.nn.silu(h @ w_gate) * (h @ w_up)) @ w_down


def gla_chunked(q, k, log_f, v, s0, chunk):
    B, T, H, DK = q.shape
    n = T // chunk

    def to_chunks(a):
        return jnp.moveaxis(a.reshape(B, n, chunk, H, a.shape[-1]), 1, 0)

    causal = jnp.tril(jnp.ones((chunk, chunk), dtype=bool))[None, :, :, None, None]

    def step(S, inp):
        qc, kc, gc, vc = inp
        G = jnp.cumsum(gc, axis=1)
        diff = G[:, :, None] - G[:, None, :]
        decay = jnp.where(causal, jnp.exp(jnp.minimum(diff, 0.0)), 0.0)
        A = jnp.einsum('bihd,bjhd,bijhd->bhij', qc, kc, decay)
        o_intra = jnp.einsum('bhij,bjhv->bihv', A, vc)
        o_inter = jnp.einsum('bihd,bhdv->bihv', qc * jnp.exp(G), S)
        G_last = G[:, -1]
        k_dec = kc * jnp.exp(G_last[:, None] - G)
        S_new = jnp.exp(G_last)[..., None] * S + jnp.einsum('bjhd,bjhv->bhdv', k_dec, vc)
        return S_new, o_intra + o_inter

    s_fin, o = lax.scan(step, s0, (to_chunks(q), to_chunks(k), to_chunks(log_f), to_chunks(v)))
    o = jnp.moveaxis(o, 0, 1).reshape(B, T, H, v.shape[-1])
    return o, s_fin


def hgrn2_mixer(h, w_q, w_f, w_i, w_g, lb, g_o, w_o, s0, chunk):
    B, T, _ = h.shape
    shp = (B, T, HGRN_HEADS, HGRN_DK)
    q = jax.nn.silu(h @ w_q).reshape(shp).astype(jnp.float32) * (HGRN_DK ** -0.5)
    z = (h @ w_f).reshape(shp).astype(jnp.float32)
    lb = lb.reshape(HGRN_HEADS, HGRN_DK)
    f = lb + (1.0 - lb) * jax.nn.sigmoid(z)
    log_f = jnp.log(jnp.maximum(f, MIN_FORGET))
    k = (1.0 - lb) * jax.nn.sigmoid(-z)
    v = (h @ w_i).reshape(B, T, HGRN_HEADS, HGRN_DV).astype(jnp.float32)
    o, s_fin = gla_chunked(q, k, log_f, v, s0.astype(jnp.float32), chunk)
    o = rms_norm(o, g_o).reshape(B, T, D_MODEL).astype(h.dtype)
    o = o * jax.nn.silu(h @ w_g)
    return o @ w_o, s_fin


def sink_probs(s, mask, sinks):
    s = jnp.where(mask, s, MASK_VALUE)
    sk = sinks.astype(jnp.float32).reshape(ATTN_KV_HEADS, GQA_GROUP, 1, 1)
    m = jnp.maximum(jnp.max(s, axis=-1, keepdims=True), sk)
    p = jnp.where(mask, jnp.exp(s - m), 0.0)
    return p / (jnp.sum(p, axis=-1, keepdims=True) + jnp.exp(sk - m))


def swa_prompt(q, k, v, sinks):
    B, T = q.shape[:2]
    n = T // WINDOW
    qb = q.reshape(B, n, WINDOW, ATTN_KV_HEADS, GQA_GROUP, ATTN_HEAD_DIM)
    pad = ((0, 0), (WINDOW, 0), (0, 0), (0, 0))
    kb = jnp.pad(k, pad).reshape(B, n + 1, WINDOW, ATTN_KV_HEADS, ATTN_HEAD_DIM)
    vb = jnp.pad(v, pad).reshape(B, n + 1, WINDOW, ATTN_KV_HEADS, ATTN_HEAD_DIM)
    kband = jnp.concatenate([kb[:, :-1], kb[:, 1:]], axis=2)
    vband = jnp.concatenate([vb[:, :-1], vb[:, 1:]], axis=2)
    s = jnp.einsum('bnqhgd,bnkhd->bnhgqk', qb, kband).astype(jnp.float32) * ATTN_SCALE
    i = jnp.arange(WINDOW)[:, None]
    j = jnp.arange(2 * WINDOW)[None, :]
    rel = WINDOW + i - j
    blk = jnp.arange(n)[:, None, None]
    mask = (rel >= 0) & (rel < WINDOW) & ((blk - 1) * WINDOW + j >= 0)
    p = sink_probs(s, mask[None, :, None, None], sinks)
    o = jnp.einsum('bnhgqk,bnkhd->bnqhgd', p.astype(v.dtype), vband)
    return o.reshape(B, T, ATTN_Q_HEADS * ATTN_HEAD_DIM)


def swa_sample(q, k_all, v_all, sinks):
    B, S = q.shape[:2]
    L = k_all.shape[1]
    qg = q.reshape(B, S, ATTN_KV_HEADS, GQA_GROUP, ATTN_HEAD_DIM)
    s = jnp.einsum('bqhgd,bkhd->bhgqk', qg, k_all).astype(jnp.float32) * ATTN_SCALE
    rel = (L - S) + jnp.arange(S)[:, None] - jnp.arange(L)[None, :]
    mask = (rel >= 0) & (rel < WINDOW)
    p = sink_probs(s, mask, sinks)
    o = jnp.einsum('bhgqk,bkhd->bqhgd', p.astype(v_all.dtype), v_all)
    return o.reshape(B, S, ATTN_Q_HEADS * ATTN_HEAD_DIM)


def trunk(x, pos, s0, chunk, past_k, past_v,
          hgrn_norm, hgrn_wq, hgrn_wf, hgrn_wi, hgrn_wg, lower_bounds, hgrn_onorm, hgrn_wo,
          kv_norm, w_k, w_v, k_norm,
          attn_norm, attn_wq, q_norm, sinks, attn_wo,
          ffn_norm, w_gate, w_up, w_down):
    B, T, _ = x.shape
    new_states = []
    k_all = None
    v_all = None
    for l in range(DEPTH):
        if l < N_A_LAYERS:
            h = rms_norm(x, hgrn_norm[l])
            o, s_fin = hgrn2_mixer(h, hgrn_wq[l], hgrn_wf[l], hgrn_wi[l], hgrn_wg[l],
                                   lower_bounds[l], hgrn_onorm[l], hgrn_wo[l], s0[l], chunk)
            new_states.append(s_fin)
        else:
            j = l - N_A_LAYERS
            if j == 0:
                hkv = rms_norm(x, kv_norm)
                k_new = rope(rms_norm((hkv @ w_k).reshape(B, T, ATTN_KV_HEADS, ATTN_HEAD_DIM), k_norm), pos)
                v_new = (hkv @ w_v).reshape(B, T, ATTN_KV_HEADS, ATTN_HEAD_DIM)
                if past_k is None:
                    k_all, v_all = k_new, v_new
                else:
                    k_all = jnp.concatenate([past_k.astype(k_new.dtype), k_new], axis=1)
                    v_all = jnp.concatenate([past_v.astype(v_new.dtype), v_new], axis=1)
            h = rms_norm(x, attn_norm[j])
            q = rope(rms_norm((h @ attn_wq[j]).reshape(B, T, ATTN_Q_HEADS, ATTN_HEAD_DIM), q_norm[j]), pos)
            if past_k is None:
                a = swa_prompt(q, k_all, v_all, sinks[j])
            else:
                a = swa_sample(q, k_all, v_all, sinks[j])
            o = a @ attn_wo[j]
        x = x + o
        x = x + swiglu(rms_norm(x, ffn_norm[l]), w_gate[l], w_up[l], w_down[l])
    return x, jnp.stack(new_states), k_all[:, -WINDOW:], v_all[:, -WINDOW:]


def setup_inputs(seed: int = 0) -> dict:
    key = jax.random.key(seed)
    ks = jax.random.split(key, 32)
    f32 = jnp.float32

    def nrm(k, shape, scale):
        return jax.random.normal(k, shape, f32) * scale

    def gain(k, shape):
        return 1.0 + 0.05 * jax.random.normal(k, shape, f32)

    D = D_MODEL
    KVW = ATTN_KV_HEADS * ATTN_HEAD_DIM
    QW = ATTN_Q_HEADS * ATTN_HEAD_DIM
    return {
        'x_prompt': nrm(ks[0], (BATCH, SEQ, D), 1.0),
        'x_sample': nrm(ks[1], (DEC_BATCH, DEC_SEQ, D), 1.0),
        'state_hgrn': nrm(ks[2], (N_A_LAYERS, DEC_BATCH, HGRN_HEADS, HGRN_DK, HGRN_DV), 0.5),
        'cache_k_win': nrm(ks[3], (DEC_BATCH, WINDOW, ATTN_KV_HEADS, ATTN_HEAD_DIM), 1.0),
        'cache_v_win': nrm(ks[4], (DEC_BATCH, WINDOW, ATTN_KV_HEADS, ATTN_HEAD_DIM), 1.0),
        'hgrn_norm': gain(ks[5], (N_A_LAYERS, D)),
        'hgrn_wq': nrm(ks[6], (N_A_LAYERS, D, HGRN_HEADS * HGRN_DK), D ** -0.5),
        'hgrn_wf': nrm(ks[7], (N_A_LAYERS, D, HGRN_HEADS * HGRN_DK), D ** -0.5),
        'hgrn_wi': nrm(ks[8], (N_A_LAYERS, D, HGRN_HEADS * HGRN_DV), D ** -0.5),
        'hgrn_wg': nrm(ks[9], (N_A_LAYERS, D, D), D ** -0.5),
        'hgrn_lb_logits': nrm(ks[10], (N_A_LAYERS, HGRN_HEADS * HGRN_DK), 0.5),
        'hgrn_onorm': gain(ks[11], (N_A_LAYERS, HGRN_DV)),
        'hgrn_wo': nrm(ks[12], (N_A_LAYERS, D, D), D ** -0.5),
        'kv_norm': gain(ks[13], (D,)),
        'w_k': nrm(ks[14], (D, KVW), D ** -0.5),
        'w_v': nrm(ks[15], (D, KVW), D ** -0.5),
        'k_norm': gain(ks[16], (ATTN_HEAD_DIM,)),
        'attn_norm': gain(ks[17], (N_B_LAYERS, D)),
        'attn_wq': nrm(ks[18], (N_B_LAYERS, D, QW), D ** -0.5),
        'q_norm': gain(ks[19], (N_B_LAYERS, ATTN_HEAD_DIM)),
        'sinks': nrm(ks[20], (N_B_LAYERS, ATTN_Q_HEADS), 0.5),
        'attn_wo': nrm(ks[21], (N_B_LAYERS, QW, D), QW ** -0.5),
        'ffn_norm': gain(ks[22], (DEPTH, D)),
        'w_gate': nrm(ks[23], (DEPTH, D, D_FF), D ** -0.5),
        'w_up': nrm(ks[24], (DEPTH, D, D_FF), D ** -0.5),
        'w_down': nrm(ks[25], (DEPTH, D_FF, D), D_FF ** -0.5),
    }


def reference(x_prompt, x_sample, state_hgrn, cache_k_win, cache_v_win,
              hgrn_norm, hgrn_wq, hgrn_wf, hgrn_wi, hgrn_wg, hgrn_lb_logits, hgrn_onorm, hgrn_wo,
              kv_norm, w_k, w_v, k_norm,
              attn_norm, attn_wq, q_norm, sinks, attn_wo,
              ffn_norm, w_gate, w_up, w_down):
    sm = jax.nn.softmax(hgrn_lb_logits.astype(jnp.float32), axis=0)
    lower_bounds = jnp.cumsum(sm, axis=0) - sm[0:1]
    weights = (hgrn_norm, hgrn_wq, hgrn_wf, hgrn_wi, hgrn_wg, lower_bounds, hgrn_onorm, hgrn_wo,
               kv_norm, w_k, w_v, k_norm,
               attn_norm, attn_wq, q_norm, sinks, attn_wo,
               ffn_norm, w_gate, w_up, w_down)
    s0_prompt = jnp.zeros((N_A_LAYERS, x_prompt.shape[0], HGRN_HEADS, HGRN_DK, HGRN_DV), jnp.float32)
    y_prompt, st_prompt, kw_prompt, vw_prompt = trunk(
        x_prompt, jnp.arange(x_prompt.shape[1]), s0_prompt, HGRN_CHUNK, None, None, *weights)
    n_new = x_sample.shape[1]
    y_sample, st_sample, kw_sample, vw_sample = trunk(
        x_sample, PAST_LEN + jnp.arange(n_new), state_hgrn, n_new, cache_k_win, cache_v_win, *weights)
    return (y_prompt, y_sample, st_prompt, st_sample, kw_prompt, vw_prompt, kw_sample, vw_sample)
```

```python
import functools
import math

import numpy as np
import jax
import jax.numpy as jnp
from jax import lax
from jax.experimental import pallas as pl
from jax.experimental.pallas import tpu as pltpu

HGRN_HEADS = 16
HEAD_DK = 128
ATTN_HEAD_DIM = 64
ATTN_KV_HEADS = 8
GQA_GROUP = 4
WINDOW = 128
ROPE_THETA = 10000.0
ATTN_SCALE = ATTN_HEAD_DIM ** -0.5
RMS_EPS = 1e-6
MASK_VALUE = -1e30
MIN_FORGET = 1e-30
PAST_LEN = 16384

LANES = 128
VMEM_LIMIT_BYTES = 56 * 1024 * 1024

GLA_CHUNK = 128
SAMPLE_PAD = 16

F32 = jnp.float32
BF16 = jnp.bfloat16


def _cparams(sem):
    return pltpu.CompilerParams(dimension_semantics=sem, vmem_limit_bytes=VMEM_LIMIT_BYTES)


def _dot(a, b):
    return jnp.dot(a, b, preferred_element_type=F32)


def _dot_nt(a, b):
    return lax.dot_general(a, b, (((1,), (1,)), ((), ())), preferred_element_type=F32)


def _dot_tn(a, b):
    return lax.dot_general(a, b, (((0,), (0,)), ((), ())), preferred_element_type=F32)


def _sigmoid(x):
    return 1.0 / (1.0 + jnp.exp(-x))


def _rms_to_scratch(x_ref, g_ref, h_scr):
    x = x_ref[...]
    ms = jnp.mean(x * x, axis=-1, keepdims=True)
    h_scr[...] = (x * lax.rsqrt(ms + RMS_EPS) * g_ref[...]).astype(BF16)


def _hgrn_proj_kernel(layer, x_ref, gn_ref, wq_ref, wf_ref, wi_ref, wg_ref, lbl_ref,
                      q_ref, k_ref, lf_ref, v_ref, gt_ref, h_scr):
    @pl.when(pl.program_id(1) == 0)
    def _():
        _rms_to_scratch(x_ref, gn_ref, h_scr)

    h = h_scr[...]
    aq = _dot(h, wq_ref[...])
    q_ref[...] = (aq * _sigmoid(aq) * (HEAD_DK ** -0.5)).astype(BF16)

    lg = lbl_ref[...]
    mx = jnp.max(lg, axis=0, keepdims=True)
    e = jnp.exp(lg - mx)
    sm = e / jnp.sum(e, axis=0, keepdims=True)
    cs = sm[0:1]
    for r in range(1, layer + 1):
        cs = cs + sm[r:r + 1]
    lb = cs - sm[0:1]

    z = _dot(h, wf_ref[...])
    ez = jnp.exp(-jnp.abs(z))
    r = 1.0 / (1.0 + ez)
    pos = z >= 0.0
    sig_p = jnp.where(pos, r, ez * r)
    sig_n = jnp.where(pos, ez * r, r)
    f = lb + (1.0 - lb) * sig_p
    lf_ref[...] = jnp.log(jnp.maximum(f, MIN_FORGET))
    k_ref[...] = ((1.0 - lb) * sig_n).astype(BF16)

    v_ref[...] = _dot(h, wi_ref[...]).astype(BF16)
    ag = _dot(h, wg_ref[...])
    gt_ref[...] = (ag * _sigmoid(ag)).astype(BF16)


def _hgrn_proj(x, gain, wq, wf, wi, wg, lb_logits, layer, tm, tn):
    M, D = x.shape
    N = wq.shape[1]
    L = lb_logits.shape[0]
    grid = (M // tm, N // tn)
    wspec = pl.BlockSpec((D, tn), lambda i, n: (0, n))
    ospec = pl.BlockSpec((tm, tn), lambda i, n: (i, n))
    return pl.pallas_call(
        functools.partial(_hgrn_proj_kernel, layer),
        grid=grid,
        in_specs=[pl.BlockSpec((tm, D), lambda i, n: (i, 0)),
                  pl.BlockSpec((1, D), lambda i, n: (0, 0)),
                  wspec, wspec, wspec, wspec,
                  pl.BlockSpec((L, tn), lambda i, n: (0, n))],
        out_specs=[ospec] * 5,
        out_shape=[jax.ShapeDtypeStruct((M, N), BF16),
                   jax.ShapeDtypeStruct((M, N), BF16),
                   jax.ShapeDtypeStruct((M, N), F32),
                   jax.ShapeDtypeStruct((M, N), BF16),
                   jax.ShapeDtypeStruct((M, N), BF16)],
        scratch_shapes=[pltpu.VMEM((tm, D), BF16)],
        compiler_params=_cparams(("parallel", "arbitrary")),
        name=f"hgrn_proj_{layer}",
    )(x, gain, wq, wf, wi, wg, lb_logits)


def _gla_consts(C):
    nlev = int(math.log2(C))
    blocks = [np.tril(np.ones((C, C), np.float32))]
    for l in range(nlev):
        h = 1 << l
        m = np.zeros((C, C), np.float32)
        for i in range(C):
            r = (i // (2 * h)) * 2 * h + h - 1
            if i > r:
                m[i, r + 1:i + 1] = 1.0
            else:
                m[i, i + 1:r + 1] = 1.0
        blocks.append(m)
    msel = np.concatenate(blocks, axis=0)
    ii, jj = np.meshgrid(np.arange(C), np.arange(C), indexing="ij")
    x = ii ^ jj
    lev = np.floor(np.log2(np.maximum(x, 1))).astype(np.int32)
    lmap = np.where(ii == jj, -1, np.where(ii > jj, lev, -2)).astype(np.int32)
    return jnp.asarray(msel, BF16), jnp.asarray(lmap, jnp.int32), nlev


def _gla_core(q, k, g, v, st, msel, lmap, nlev):
    C = q.shape[0]
    ghi = g.astype(BF16)
    glo = (g - ghi.astype(F32)).astype(BF16)
    el2 = _dot(msel, jnp.concatenate([ghi, glo], axis=1))
    el = el2[:, :LANES] + el2[:, LANES:]
    gcum = el[0:C]
    a = jnp.where(lmap == -1, _dot_nt(q.astype(BF16), k.astype(BF16)), 0.0)
    for l in range(nlev):
        e = jnp.exp(el[C * (l + 1):C * (l + 2)])
        al = _dot_nt((q * e).astype(BF16), (k * e).astype(BF16))
        a = jnp.where(lmap == l, al, a)
    o = _dot(a.astype(BF16), v.astype(BF16))
    o = o + _dot_nt((q * jnp.exp(gcum)).astype(BF16), st.astype(BF16))
    glast = gcum[C - 1:C, :]
    kdec = k * jnp.exp(glast - gcum)
    st_new = st * jnp.exp(glast) + _dot_tn(v.astype(BF16), kdec.astype(BF16))
    return o, st_new


def _gla_out(o, gon, gate):
    ms = jnp.mean(o * o, axis=-1, keepdims=True)
    return o * lax.rsqrt(ms + RMS_EPS) * gon * gate


def _gla_prompt_kernel(nlev, n_heads, q_ref, k_ref, g_ref, v_ref, gt_ref, msel_ref, lmap_ref,
                       gon_ref, o_ref, sfin_ref, st_scr):
    c = pl.program_id(1)

    @pl.when(c == 0)
    def _():
        st_scr[...] = jnp.zeros_like(st_scr)

    def body(h, carry):
        hs = pl.ds(pl.multiple_of(h * LANES, LANES), LANES)
        o, st_new = _gla_core(q_ref[:, hs].astype(F32), k_ref[:, hs].astype(F32), g_ref[:, hs],
                              v_ref[:, hs].astype(F32), st_scr[h], msel_ref[...], lmap_ref[...], nlev)
        st_scr[h] = st_new
        o_ref[:, hs] = _gla_out(o, gon_ref[...], gt_ref[:, hs].astype(F32)).astype(BF16)
        return carry

    lax.fori_loop(0, n_heads, body, 0)

    @pl.when(c == pl.num_programs(1) - 1)
    def _():
        def wb(h, carry):
            sfin_ref[0, h] = st_scr[h].T
            return carry
        lax.fori_loop(0, n_heads, wb, 0)


def _gla_prompt(q, k, g, v, gate, gon, batch, seq, name):
    C = GLA_CHUNK
    W = q.shape[1]
    H = W // LANES
    nc = seq // C
    msel, lmap, nlev = _gla_consts(C)
    rspec = pl.BlockSpec((C, W), lambda b, c: (b * nc + c, 0))
    return pl.pallas_call(
        functools.partial(_gla_prompt_kernel, nlev, H),
        grid=(batch, nc),
        in_specs=[rspec, rspec, rspec, rspec, rspec,
                  pl.BlockSpec(msel.shape, lambda b, c: (0, 0)),
                  pl.BlockSpec(lmap.shape, lambda b, c: (0, 0)),
                  pl.BlockSpec((1, LANES), lambda b, c: (0, 0))],
        out_specs=[rspec, pl.BlockSpec((1, H, LANES, LANES), lambda b, c: (b, 0, 0, 0))],
        out_shape=[jax.ShapeDtypeStruct((batch * seq, W), BF16),
                   jax.ShapeDtypeStruct((batch, H, LANES, LANES), F32)],
        scratch_shapes=[pltpu.VMEM((H, LANES, LANES), F32)],
        compiler_params=_cparams(("parallel", "arbitrary")),
        name=name,
    )(q, k, g, v, gate, msel, lmap, gon)


def _gla_sample_kernel(nlev, n_heads, n_new, q_ref, k_ref, g_ref, v_ref, gt_ref, msel_ref, lmap_ref,
                       gon_ref, s0_ref, o_ref, sfin_ref):
    nb = SAMPLE_PAD // n_new
    zpad = jnp.zeros((SAMPLE_PAD - n_new, LANES), F32)

    def body(h, carry):
        hs = pl.ds(pl.multiple_of(h * LANES, LANES), LANES)
        q = q_ref[:, hs].astype(F32)
        k = k_ref[:, hs].astype(F32)
        g = g_ref[:, hs]
        v = v_ref[:, hs].astype(F32)
        gate = gt_ref[:, hs].astype(F32)
        outs = []
        for bb in range(nb):
            rs = slice(bb * n_new, (bb + 1) * n_new)
            pad = lambda t: jnp.concatenate([t[rs], zpad], axis=0)
            o, st_new = _gla_core(pad(q), pad(k), pad(g), pad(v), s0_ref[bb, h].T,
                                  msel_ref[...], lmap_ref[...], nlev)
            sfin_ref[bb, h] = st_new.T
            outs.append(_gla_out(o[0:n_new], gon_ref[...], gate[rs]))
        o_ref[:, hs] = jnp.concatenate(outs, axis=0).astype(BF16)
        return carry

    lax.fori_loop(0, n_heads, body, 0)


def _gla_sample(q, k, g, v, gate, gon, s0, row0, n_seq, n_new, name):
    W = q.shape[1]
    H = W // LANES
    nb = SAMPLE_PAD // n_new
    msel, lmap, nlev = _gla_consts(SAMPLE_PAD)
    blk0 = row0 // SAMPLE_PAD
    rspec = pl.BlockSpec((SAMPLE_PAD, W), lambda i: (blk0 + i, 0))
    sspec = pl.BlockSpec((nb, H, LANES, LANES), lambda i: (i, 0, 0, 0))
    return pl.pallas_call(
        functools.partial(_gla_sample_kernel, nlev, H, n_new),
        grid=(n_seq // nb,),
        in_specs=[rspec, rspec, rspec, rspec, rspec,
                  pl.BlockSpec(msel.shape, lambda i: (0, 0)),
                  pl.BlockSpec(lmap.shape, lambda i: (0, 0)),
                  pl.BlockSpec((1, LANES), lambda i: (0, 0)),
                  sspec],
        out_specs=[pl.BlockSpec((SAMPLE_PAD, W), lambda i: (i, 0)), sspec],
        out_shape=[jax.ShapeDtypeStruct((n_seq * n_new, W), BF16),
                   jax.ShapeDtypeStruct((n_seq, H, LANES, LANES), F32)],
        compiler_params=_cparams(("parallel",)),
        name=name,
    )(q, k, g, v, gate, msel, lmap, gon, s0)


def _out_proj_kernel(a_ref, w_ref, x_ref, o_ref):
    o_ref[...] = x_ref[...] + _dot(a_ref[...], w_ref[...])


def _out_proj(a, w, x, tm, tn, name):
    M, K = a.shape
    N = w.shape[1]
    return pl.pallas_call(
        _out_proj_kernel,
        grid=(M // tm, N // tn),
        in_specs=[pl.BlockSpec((tm, K), lambda i, n: (i, 0)),
                  pl.BlockSpec((K, tn), lambda i, n: (0, n)),
                  pl.BlockSpec((tm, tn), lambda i, n: (i, n))],
        out_specs=pl.BlockSpec((tm, tn), lambda i, n: (i, n)),
        out_shape=jax.ShapeDtypeStruct((M, N), F32),
        compiler_params=_cparams(("parallel", "arbitrary")),
        name=name,
    )(a, w, x)


def _ffn_kernel(x_ref, gn_ref, wg_ref, wu_ref, wd_ref, o_ref, h_scr):
    @pl.when(pl.program_id(1) == 0)
    def _():
        _rms_to_scratch(x_ref, gn_ref, h_scr)
        o_ref[...] = x_ref[...]

    h = h_scr[...]
    g = _dot(h, wg_ref[...])
    u = _dot(h, wu_ref[...])
    a = (g * _sigmoid(g) * u).astype(BF16)
    o_ref[...] += _dot(a, wd_ref[...])


def _ffn(x, gain, wg, wu, wd, tm, tf, name):
    M, D = x.shape
    FF = wg.shape[1]
    return pl.pallas_call(
        _ffn_kernel,
        grid=(M // tm, FF // tf),
        in_specs=[pl.BlockSpec((tm, D), lambda i, f: (i, 0)),
                  pl.BlockSpec((1, D), lambda i, f: (0, 0)),
                  pl.BlockSpec((D, tf), lambda i, f: (0, f)),
                  pl.BlockSpec((D, tf), lambda i, f: (0, f)),
                  pl.BlockSpec((tf, D), lambda i, f: (f, 0))],
        out_specs=pl.BlockSpec((tm, D), lambda i, f: (i, 0)),
        out_shape=jax.ShapeDtypeStruct((M, D), F32),
        scratch_shapes=[pltpu.VMEM((tm, D), BF16)],
        compiler_params=_cparams(("parallel", "arbitrary")),
        name=name,
    )(x, gain, wg, wu, wd)


def _head_norm_rope(acc, gain, cos, sin, scale):
    r = lax.broadcasted_iota(jnp.int32, (LANES, LANES), 0)
    c = lax.broadcasted_iota(jnp.int32, (LANES, LANES), 1)
    seg = jnp.where((r >> 6) == (c >> 6), 1.0, 0.0).astype(BF16)
    ss = _dot((acc * acc).astype(BF16), seg)
    y = acc * lax.rsqrt(ss * (1.0 / ATTN_HEAD_DIM) + RMS_EPS) * gain
    lane = lax.broadcasted_iota(jnp.int32, y.shape, 1)
    half = ATTN_HEAD_DIM // 2
    rot = jnp.where((lane & half) == 0, pltpu.roll(y, LANES - half, 1), pltpu.roll(y, half, 1))
    out = y * cos + rot * sin
    return out * scale if scale != 1.0 else out


def _q_proj_kernel(x_ref, gn_ref, w_ref, hg_ref, cos_ref, sin_ref, q_ref, h_scr):
    @pl.when(pl.program_id(1) == 0)
    def _():
        _rms_to_scratch(x_ref, gn_ref, h_scr)

    acc = _dot(h_scr[...], w_ref[...])
    for c in range(acc.shape[1] // LANES):
        cs = slice(c * LANES, (c + 1) * LANES)
        q_ref[:, cs] = _head_norm_rope(acc[:, cs], hg_ref[...], cos_ref[...], sin_ref[...],
                                       ATTN_SCALE).astype(BF16)


def _q_proj(x, gain, w, head_gain, cos, sin, tm, tn, name):
    M, D = x.shape
    N = w.shape[1]
    return pl.pallas_call(
        _q_proj_kernel,
        grid=(M // tm, N // tn),
        in_specs=[pl.BlockSpec((tm, D), lambda i, n: (i, 0)),
                  pl.BlockSpec((1, D), lambda i, n: (0, 0)),
                  pl.BlockSpec((D, tn), lambda i, n: (0, n)),
                  pl.BlockSpec((1, LANES), lambda i, n: (0, 0)),
                  pl.BlockSpec((tm, LANES), lambda i, n: (i, 0)),
                  pl.BlockSpec((tm, LANES), lambda i, n: (i, 0))],
        out_specs=pl.BlockSpec((tm, tn), lambda i, n: (i, n)),
        out_shape=jax.ShapeDtypeStruct((M, N), BF16),
        scratch_shapes=[pltpu.VMEM((tm, D), BF16)],
        compiler_params=_cparams(("parallel", "arbitrary")),
        name=name,
    )(x, gain, w, head_gain, cos, sin)


def _kv_proj_kernel(x_ref, gn_ref, wk_ref, wv_ref, hg_ref, cos_ref, sin_ref, k_ref, v_ref, h_scr):
    @pl.when(pl.program_id(1) == 0)
    def _():
        _rms_to_scratch(x_ref, gn_ref, h_scr)

    h = h_scr[...]
    acc = _dot(h, wk_ref[...])
    for c in range(acc.shape[1] // LANES):
        cs = slice(c * LANES, (c + 1) * LANES)
        k_ref[:, cs] = _head_norm_rope(acc[:, cs], hg_ref[...], cos_ref[...], sin_ref[...],
                                       1.0).astype(BF16)
    v_ref[...] = _dot(h, wv_ref[...]).astype(BF16)


def _kv_proj(x, gain, wk, wv, head_gain, cos, sin, tm, tn):
    M, D = x.shape
    N = wk.shape[1]
    wspec = pl.BlockSpec((D, tn), lambda i, n: (0, n))
    ospec = pl.BlockSpec((tm, tn), lambda i, n: (i, n))
    return pl.pallas_call(
        _kv_proj_kernel,
        grid=(M // tm, N // tn),
        in_specs=[pl.BlockSpec((tm, D), lambda i, n: (i, 0)),
                  pl.BlockSpec((1, D), lambda i, n: (0, 0)),
                  wspec, wspec,
                  pl.BlockSpec((1, LANES), lambda i, n: (0, 0)),
                  pl.BlockSpec((tm, LANES), lambda i, n: (i, 0)),
                  pl.BlockSpec((tm, LANES), lambda i, n: (i, 0))],
        out_specs=[ospec, ospec],
        out_shape=[jax.ShapeDtypeStruct((M, N), BF16), jax.ShapeDtypeStruct((M, N), BF16)],
        scratch_shapes=[pltpu.VMEM((tm, D), BF16)],
        compiler_params=_cparams(("parallel", "arbitrary")),
        name="kv_proj",
    )(x, gain, wk, wv, head_gain, cos, sin)


def _softmax_sink_pv(s_parts, v_parts, sink):
    m = sink
    for s in s_parts:
        m = jnp.maximum(m, jnp.max(s, axis=-1, keepdims=True))
    den = jnp.exp(sink - m)
    o = None
    for s, v in zip(s_parts, v_parts):
        p = jnp.exp(s - m)
        den = den + jnp.sum(p, axis=-1, keepdims=True)
        pv = _dot(p.astype(BF16), v)
        o = pv if o is None else o + pv
    return o / den


def _swa_prompt_kernel(sk_ref, q_ref, kp_ref, kc_ref, vp_ref, vc_ref, o_ref):
    i = pl.program_id(1)
    W = WINDOW
    row = lax.broadcasted_iota(jnp.int32, (W, 2 * W), 0)
    col = lax.broadcasted_iota(jnp.int32, (W, 2 * W), 1)
    first_key = jnp.where(i > 0, 0, W)
    mask = (col > row) & (col <= row + W) & (col >= first_key)
    lo = lax.broadcasted_iota(jnp.int32, (W, LANES), 1) < ATTN_HEAD_DIM
    for g in range(ATTN_KV_HEADS):
        gs = slice(g * LANES, (g + 1) * LANES)
        kd = jnp.concatenate([kp_ref[:, gs], kc_ref[:, gs]], axis=0)
        vd = jnp.concatenate([vp_ref[:, gs], vc_ref[:, gs]], axis=0)
        for c in range(GQA_GROUP // 2):
            cs = slice((2 * g + c) * LANES, (2 * g + c + 1) * LANES)
            qc = q_ref[:, cs]
            halves = []
            for par in range(2):
                qm = jnp.where(lo, qc, jnp.zeros_like(qc)) if par == 0 else jnp.where(lo, jnp.zeros_like(qc), qc)
                s = jnp.where(mask, _dot_nt(qm, kd), MASK_VALUE)
                halves.append(_softmax_sink_pv([s], [vd], sk_ref[GQA_GROUP * g + 2 * c + par]))
            o_ref[:, cs] = jnp.where(lo, halves[0], halves[1]).astype(BF16)


def _swa_prompt(q, kdup, vdup, sinks, batch, seq, name):
    W = WINDOW
    nb = seq // W
    QW = q.shape[1]
    KW = kdup.shape[1]
    prev = lambda b, i: (b * nb + jnp.maximum(i - 1, 0), 0)
    cur = lambda b, i: (b * nb + i, 0)
    return pl.pallas_call(
        _swa_prompt_kernel,
        grid=(batch, nb),
        in_specs=[pl.BlockSpec(memory_space=pltpu.SMEM),
                  pl.BlockSpec((W, QW), cur),
                  pl.BlockSpec((W, KW), prev), pl.BlockSpec((W, KW), cur),
                  pl.BlockSpec((W, KW), prev), pl.BlockSpec((W, KW), cur)],
        out_specs=pl.BlockSpec((W, QW), cur),
        out_shape=jax.ShapeDtypeStruct((batch * seq, QW), BF16),
        compiler_params=_cparams(("parallel", "arbitrary")),
        name=name,
    )(sinks, q, kdup, kdup, vdup, vdup)


def _swa_sample_kernel(n_new, sk_ref, q_ref, kn_ref, vn_ref, kc_ref, vc_ref, o_ref):
    W = WINDOW
    nb = SAMPLE_PAD // n_new
    rows = GQA_GROUP * n_new
    qi = lax.broadcasted_iota(jnp.int32, (rows, W), 0) & (n_new - 1)
    mask_c = lax.broadcasted_iota(jnp.int32, (rows, W), 1) > qi
    qi_n = lax.broadcasted_iota(jnp.int32, (rows, SAMPLE_PAD), 0) & (n_new - 1)
    mask_n = lax.broadcasted_iota(jnp.int32, (rows, SAMPLE_PAD), 1) <= qi_n
    hrow = lax.broadcasted_iota(jnp.int32, (rows, 1), 0) // n_new
    lo = lax.broadcasted_iota(jnp.int32, (n_new, LANES), 1) < ATTN_HEAD_DIM
    zq = jnp.zeros((n_new, LANES), F32)
    zpad = jnp.zeros((SAMPLE_PAD - n_new, LANES), F32)
    q = q_ref[...].astype(F32)
    kn = kn_ref[...].astype(F32)
    vn = vn_ref[...].astype(F32)
    out_rows = []
    for bb in range(nb):
        rs = slice(bb * n_new, (bb + 1) * n_new)
        chunks = []
        for g in range(ATTN_KV_HEADS):
            gs = slice(g * LANES, (g + 1) * LANES)
            kc = kc_ref[bb, :, gs]
            vc = vc_ref[bb, :, gs]
            kn_g = jnp.concatenate([kn[rs, gs], zpad], axis=0).astype(BF16)
            vn_g = jnp.concatenate([vn[rs, gs], zpad], axis=0).astype(BF16)
            lhs = []
            sink = jnp.zeros((rows, 1), F32)
            for a in range(GQA_GROUP):
                c, par = divmod(a, 2)
                cs = slice((2 * g + c) * LANES, (2 * g + c + 1) * LANES)
                qc = q[rs, cs]
                lhs.append(jnp.where(lo, qc, zq) if par == 0 else jnp.where(lo, zq, qc))
                sink = jnp.where(hrow == a, sk_ref[GQA_GROUP * g + a], sink)
            lhs = jnp.concatenate(lhs, axis=0).astype(BF16)
            s_c = jnp.where(mask_c, _dot_nt(lhs, kc), MASK_VALUE)
            s_n = jnp.where(mask_n, _dot_nt(lhs, kn_g), MASK_VALUE)
            o = _softmax_sink_pv([s_c, s_n], [vc, vn_g], sink)
            for c in range(GQA_GROUP // 2):
                chunks.append(jnp.where(lo, o[(2 * c) * n_new:(2 * c + 1) * n_new],
                                        o[(2 * c + 1) * n_new:(2 * c + 2) * n_new]))
        out_rows.append(jnp.concatenate(chunks, axis=1))
    o_ref[...] = jnp.concatenate(out_rows, axis=0).astype(BF16)


def _swa_sample(q, kdup, vdup, cache_kdup, cache_vdup, sinks, row0, n_seq, n_new, name):
    W = WINDOW
    QW = q.shape[1]
    KW = kdup.shape[1]
    nb = SAMPLE_PAD // n_new
    blk0 = row0 // SAMPLE_PAD
    rmap = lambda i: (blk0 + i, 0)
    cspec = pl.BlockSpec((nb, W, KW), lambda i: (i, 0, 0))
    return pl.pallas_call(
        functools.partial(_swa_sample_kernel, n_new),
        grid=(n_seq // nb,),
        in_specs=[pl.BlockSpec(memory_space=pltpu.SMEM),
                  pl.BlockSpec((SAMPLE_PAD, QW), rmap),
                  pl.BlockSpec((SAMPLE_PAD, KW), rmap), pl.BlockSpec((SAMPLE_PAD, KW), rmap),
                  cspec, cspec],
        out_specs=pl.BlockSpec((SAMPLE_PAD, QW), lambda i: (i, 0)),
        out_shape=jax.ShapeDtypeStruct((n_seq * n_new, QW), BF16),
        compiler_params=_cparams(("parallel",)),
        name=name,
    )(sinks, q, kdup, vdup, cache_kdup, cache_vdup)


def _dup_heads(w):
    lead = w.shape[:-1]
    w3 = w.reshape(lead + (ATTN_KV_HEADS, ATTN_HEAD_DIM))
    return jnp.concatenate([w3, w3], axis=-1).reshape(lead + (ATTN_KV_HEADS * 2 * ATTN_HEAD_DIM,))


def _undup_heads(a, lead):
    return a.reshape(lead + (ATTN_KV_HEADS, 2 * ATTN_HEAD_DIM))[..., :ATTN_HEAD_DIM].astype(F32)


def _rope_tables(pos):
    half = ATTN_HEAD_DIM // 2
    inv = ROPE_THETA ** (-jnp.arange(half, dtype=F32) / half)
    ang = pos.astype(F32)[:, None] * inv[None, :]
    cos = jnp.cos(ang)
    sin = jnp.sin(ang)
    reps = LANES // ATTN_HEAD_DIM
    cos_t = jnp.tile(jnp.concatenate([cos, cos], axis=-1), (1, reps))
    sin_t = jnp.tile(jnp.concatenate([-sin, sin], axis=-1), (1, reps))
    return cos_t, sin_t


def kernel(x_prompt, x_sample, state_hgrn, cache_k_win, cache_v_win, hgrn_norm, hgrn_wq, hgrn_wf, hgrn_wi, hgrn_wg, hgrn_lb_logits, hgrn_onorm, hgrn_wo, kv_norm, w_k, w_v, k_norm, attn_norm, attn_wq, q_norm, sinks, attn_wo, ffn_norm, w_gate, w_up, w_down):
    B, T, D = x_prompt.shape
    SB, S, _ = x_sample.shape
    MP = B * T
    MS = SB * S
    n_a = hgrn_wq.shape[0]
    n_b = attn_wq.shape[0]
    assert SAMPLE_PAD % S == 0 and SB % (SAMPLE_PAD // S) == 0 and MP % SAMPLE_PAD == 0
    assert T % GLA_CHUNK == 0 and T % WINDOW == 0

    x = jnp.concatenate([x_prompt.reshape(MP, D), x_sample.reshape(MS, D)], axis=0)
    M = MP + MS
    tm = 768 if M % 768 == 0 else 256
    tn = 512
    tf = 512

    bf = lambda w: w.astype(BF16)
    row = lambda g: g.reshape(1, -1).astype(F32)
    pos = jnp.concatenate([jnp.tile(jnp.arange(T), B), jnp.tile(PAST_LEN + jnp.arange(S), SB)])
    cos_t, sin_t = _rope_tables(pos)
    head_gain = lambda g: jnp.tile(g.astype(F32), LANES // ATTN_HEAD_DIM).reshape(1, LANES)

    st_prompt, st_sample = [], []
    for l in range(n_a):
        q, k, lf, v, gate = _hgrn_proj(x, row(hgrn_norm[l]), bf(hgrn_wq[l]), bf(hgrn_wf[l]), bf(hgrn_wi[l]),
                                       bf(hgrn_wg[l]), hgrn_lb_logits.astype(F32), l, tm, tn)
        gon = row(hgrn_onorm[l])
        o_p, s_p = _gla_prompt(q, k, lf, v, gate, gon, B, T, f"gla_prompt_{l}")
        o_s, s_s = _gla_sample(q, k, lf, v, gate, gon, state_hgrn[l].astype(F32), MP, SB, S, f"gla_sample_{l}")
        st_prompt.append(s_p)
        st_sample.append(s_s)
        o = jnp.concatenate([o_p, o_s], axis=0)
        x = _out_proj(o, bf(hgrn_wo[l]), x, tm, tn, f"hgrn_out_{l}")
        x = _ffn(x, row(ffn_norm[l]), bf(w_gate[l]), bf(w_up[l]), bf(w_down[l]), tm, tf, f"ffn_{l}")

    kdup, vdup = _kv_proj(x, row(kv_norm), bf(_dup_heads(w_k)), bf(_dup_heads(w_v)), head_gain(k_norm),
                          cos_t, sin_t, tm, tn)
    KW = ATTN_KV_HEADS * ATTN_HEAD_DIM
    cache_kdup = bf(_dup_heads(cache_k_win.reshape(SB, WINDOW, KW)))
    cache_vdup = bf(_dup_heads(cache_v_win.reshape(SB, WINDOW, KW)))
    for j in range(n_b):
        l = n_a + j
        q = _q_proj(x, row(attn_norm[j]), bf(attn_wq[j]), head_gain(q_norm[j]), cos_t, sin_t, tm, tn,
                    f"q_proj_{j}")
        sk = sinks[j].astype(F32)
        a_p = _swa_prompt(q, kdup, vdup, sk, B, T, f"swa_prompt_{j}")
        a_s = _swa_sample(q, kdup, vdup, cache_kdup, cache_vdup, sk, MP, SB, S, f"swa_sample_{j}")
        a = jnp.concatenate([a_p, a_s], axis=0)
        x = _out_proj(a, bf(attn_wo[j]), x, tm, tn, f"attn_out_{j}")
        x = _ffn(x, row(ffn_norm[l]), bf(w_gate[l]), bf(w_up[l]), bf(w_down[l]), tm, tf, f"ffn_{l}")

    y_prompt = x[:MP].reshape(B, T, D)
    y_sample = x[MP:].reshape(SB, S, D)
    k_win_p = _undup_heads(kdup[:MP].reshape(B, T, -1)[:, T - WINDOW:], (B, WINDOW))
    v_win_p = _undup_heads(vdup[:MP].reshape(B, T, -1)[:, T - WINDOW:], (B, WINDOW))
    k_new_s = _undup_heads(kdup[MP:].reshape(SB, S, -1), (SB, S))
    v_new_s = _undup_heads(vdup[MP:].reshape(SB, S, -1), (SB, S))
    k_win_s = jnp.concatenate([cache_k_win[:, S:].astype(F32), k_new_s], axis=1)
    v_win_s = jnp.concatenate([cache_v_win[:, S:].astype(F32), v_new_s], axis=1)
    return (y_prompt, y_sample, jnp.stack(st_prompt), jnp.stack(st_sample),
            k_win_p, v_win_p, k_win_s, v_win_s)
```

```python
import functools
import math

import numpy as np
import jax
import jax.numpy as jnp
from jax import lax
from jax.experimental import pallas as pl
from jax.experimental.pallas import tpu as pltpu

HGRN_HEADS = 16
HEAD_DK = 128
ATTN_HEAD_DIM = 64
ATTN_KV_HEADS = 8
GQA_GROUP = 4
WINDOW = 128
ROPE_THETA = 10000.0
ATTN_SCALE = ATTN_HEAD_DIM ** -0.5
RMS_EPS = 1e-6
MASK_VALUE = -1e30
MIN_FORGET = 1e-30
PAST_LEN = 16384

LANES = 128
VMEM_LIMIT_BYTES = 56 * 1024 * 1024

GLA_CHUNK = 128
SAMPLE_PAD = 16
GLA_HEADS_PER_ITER = 8


def _heads_per_iter(n_heads):
    return math.gcd(n_heads, GLA_HEADS_PER_ITER)

F32 = jnp.float32
BF16 = jnp.bfloat16


def _cparams(sem):
    return pltpu.CompilerParams(dimension_semantics=sem, vmem_limit_bytes=VMEM_LIMIT_BYTES)


def _dot(a, b):
    return jnp.dot(a, b, preferred_element_type=F32)


def _dot_nt(a, b):
    return lax.dot_general(a, b, (((1,), (1,)), ((), ())), preferred_element_type=F32)


def _dot_tn(a, b):
    return lax.dot_general(a, b, (((0,), (0,)), ((), ())), preferred_element_type=F32)


def _wdot(a, w_ref):
    return _dot(a, w_ref[...].astype(BF16))


def _wspec_cols(layer, k, tn):
    return pl.BlockSpec((None, k, tn), lambda i, n: (layer, 0, n))


def _sigmoid(x):
    return 1.0 / (1.0 + jnp.exp(-x))


def _rms_to_scratch(x_ref, g_ref, h_scr):
    x = x_ref[...]
    ms = jnp.mean(x * x, axis=-1, keepdims=True)
    h_scr[...] = (x * lax.rsqrt(ms + RMS_EPS) * g_ref[...]).astype(BF16)


def _hgrn_proj_kernel(layer, x_ref, gn_ref, wq_ref, wf_ref, wi_ref, wg_ref, lbl_ref,
                      q_ref, k_ref, lf_ref, v_ref, gt_ref, h_scr):
    @pl.when(pl.program_id(1) == 0)
    def _():
        _rms_to_scratch(x_ref, gn_ref, h_scr)

    h = h_scr[...]
    aq = _wdot(h, wq_ref)
    q_ref[...] = (aq * _sigmoid(aq) * (HEAD_DK ** -0.5)).astype(BF16)

    lg = lbl_ref[...]
    mx = jnp.max(lg, axis=0, keepdims=True)
    e = jnp.exp(lg - mx)
    sm = e / jnp.sum(e, axis=0, keepdims=True)
    cs = sm[0:1]
    for r in range(1, layer + 1):
        cs = cs + sm[r:r + 1]
    lb = cs - sm[0:1]

    z = _wdot(h, wf_ref)
    ez = jnp.exp(-jnp.abs(z))
    r = 1.0 / (1.0 + ez)
    pos = z >= 0.0
    sig_p = jnp.where(pos, r, ez * r)
    sig_n = jnp.where(pos, ez * r, r)
    f = lb + (1.0 - lb) * sig_p
    lf_ref[...] = jnp.log(jnp.maximum(f, MIN_FORGET))
    k_ref[...] = ((1.0 - lb) * sig_n).astype(BF16)

    v_ref[...] = _wdot(h, wi_ref).astype(BF16)
    ag = _wdot(h, wg_ref)
    gt_ref[...] = (ag * _sigmoid(ag)).astype(BF16)


def _hgrn_proj(x, gain, wq, wf, wi, wg, lb_logits, layer, tm, tn):
    M, D = x.shape
    N = wq.shape[2]
    L = lb_logits.shape[0]
    grid = (M // tm, N // tn)
    wspec = _wspec_cols(layer, D, tn)
    ospec = pl.BlockSpec((tm, tn), lambda i, n: (i, n))
    return pl.pallas_call(
        functools.partial(_hgrn_proj_kernel, layer),
        grid=grid,
        in_specs=[pl.BlockSpec((tm, D), lambda i, n: (i, 0)),
                  pl.BlockSpec((1, D), lambda i, n: (0, 0)),
                  wspec, wspec, wspec, wspec,
                  pl.BlockSpec((L, tn), lambda i, n: (0, n))],
        out_specs=[ospec] * 5,
        out_shape=[jax.ShapeDtypeStruct((M, N), BF16),
                   jax.ShapeDtypeStruct((M, N), BF16),
                   jax.ShapeDtypeStruct((M, N), F32),
                   jax.ShapeDtypeStruct((M, N), BF16),
                   jax.ShapeDtypeStruct((M, N), BF16)],
        scratch_shapes=[pltpu.VMEM((tm, D), BF16)],
        compiler_params=_cparams(("parallel", "arbitrary")),
        name=f"hgrn_proj_{layer}",
    )(x, gain, wq, wf, wi, wg, lb_logits)


def _gla_consts(C):
    nlev = int(math.log2(C))
    tri = np.tril(np.ones((C, C), np.float32))
    ii, jj = np.meshgrid(np.arange(C), np.arange(C), indexing="ij")
    x = ii ^ jj
    lev = np.floor(np.log2(np.maximum(x, 1))).astype(np.int32)
    lmap = np.where(ii == jj, -1, np.where(ii > jj, lev, -2)).astype(np.int32)
    rows = np.arange(C)
    sgn = np.stack([np.where((rows >> l) & 1 == 1, 1.0, 0.0 if l == 0 else -1.0) for l in range(nlev)])
    sgn = np.broadcast_to((sgn * math.log2(math.e))[:, :, None], (nlev, C, LANES)).astype(np.float32)
    return jnp.asarray(tri, BF16), jnp.asarray(lmap, jnp.int32), jnp.asarray(sgn), nlev


SUBLANES = 8


def _level_exponent(l, g, gcum, gs_ref, sgn):
    C = g.shape[0]
    h = 1 << l
    if l == 0:
        return g * sgn
    sub = lax.broadcasted_iota(jnp.int32, (SUBLANES, LANES), 0)
    bcast = lambda r: jnp.broadcast_to(gs_ref[pl.ds(r, 1), :], (SUBLANES, LANES))
    pieces = []
    for base in range(0, C, SUBLANES):
        if 2 * h >= SUBLANES:
            pieces.append(bcast((base // (2 * h)) * 2 * h + h - 1))
        else:
            p = bcast(base + h - 1)
            for blk in range(1, SUBLANES // (2 * h)):
                p = jnp.where(sub >= blk * 2 * h, bcast(base + blk * 2 * h + h - 1), p)
            pieces.append(p)
    gr = jnp.concatenate(pieces, axis=0)
    return (gcum - gr) * sgn


def _gla_core(chains, tri, lmap, sgn_ref, nlev, gs_scr):
    C = chains[0][0].shape[0]
    gcums = []
    for u, (q, k, g, v, st) in enumerate(chains):
        ghi = g.astype(BF16)
        glo = (g - ghi.astype(F32)).astype(BF16)
        c2 = _dot(tri, jnp.concatenate([ghi, glo], axis=1))
        gcum = c2[:, :LANES] + c2[:, LANES:]
        gs_scr[u] = gcum
        gcums.append(gcum)
    accs = [jnp.where(lmap == -1, _dot_nt(q, k), 0.0) for (q, k, g, v, st) in chains]
    qk32 = [(q.astype(F32), k.astype(F32)) for (q, k, g, v, st) in chains]
    for l in range(nlev):
        for u, (q, k, g, v, st) in enumerate(chains):
            e = jnp.exp2(_level_exponent(l, g, gcums[u], gs_scr.at[u], sgn_ref[l]))
            qf, kf = qk32[u]
            accs[u] = jnp.where(lmap == l, _dot_nt((qf * e).astype(BF16), (kf * e).astype(BF16)), accs[u])
    outs = []
    for u, (q, k, g, v, st) in enumerate(chains):
        gcum = gcums[u]
        qf, kf = qk32[u]
        o = _dot(accs[u].astype(BF16), v)
        o = o + _dot_nt((qf * jnp.exp(gcum)).astype(BF16), st.astype(BF16))
        glast = gcum[C - 1:C, :]
        kdec = (kf * jnp.exp(glast - gcum)).astype(BF16)
        st_new = st * jnp.exp(glast) + _dot_tn(v, kdec)
        outs.append((o, st_new))
    return outs


def _gla_out(o, gon, gate):
    ms = jnp.mean(o * o, axis=-1, keepdims=True)
    return o * lax.rsqrt(ms + RMS_EPS) * gon * gate


def _gla_prompt_kernel(nlev, n_heads, q_ref, k_ref, g_ref, v_ref, gt_ref, tri_ref, lmap_ref, sgn_ref,
                       gon_ref, o_ref, sfin_ref, st_scr, gs_scr):
    c = pl.program_id(1)
    hpi = _heads_per_iter(n_heads)

    @pl.when(c == 0)
    def _():
        st_scr[...] = jnp.zeros_like(st_scr)

    def body(j, carry):
        heads = [j * hpi + u for u in range(hpi)]
        lanes = [pl.ds(pl.multiple_of(h * LANES, LANES), LANES) for h in heads]
        chains = [(q_ref[:, hs], k_ref[:, hs], g_ref[:, hs], v_ref[:, hs], st_scr[h])
                  for h, hs in zip(heads, lanes)]
        res = _gla_core(chains, tri_ref[...], lmap_ref[...], sgn_ref, nlev, gs_scr)
        for h, hs, (o, st_new) in zip(heads, lanes, res):
            st_scr[h] = st_new
            o_ref[:, hs] = _gla_out(o, gon_ref[...], gt_ref[:, hs].astype(F32)).astype(BF16)
        return carry

    lax.fori_loop(0, n_heads // hpi, body, 0)

    @pl.when(c == pl.num_programs(1) - 1)
    def _():
        def wb(h, carry):
            sfin_ref[0, h] = st_scr[h].T
            return carry
        lax.fori_loop(0, n_heads, wb, 0)


def _gla_prompt(q, k, g, v, gate, gon, batch, seq, name):
    C = GLA_CHUNK
    W = q.shape[1]
    H = W // LANES
    nc = seq // C
    tri, lmap, sgn, nlev = _gla_consts(C)
    rspec = pl.BlockSpec((C, W), lambda b, c: (b * nc + c, 0))
    return pl.pallas_call(
        functools.partial(_gla_prompt_kernel, nlev, H),
        grid=(batch, nc),
        in_specs=[rspec, rspec, rspec, rspec, rspec,
                  pl.BlockSpec(tri.shape, lambda b, c: (0, 0)),
                  pl.BlockSpec(lmap.shape, lambda b, c: (0, 0)),
                  pl.BlockSpec(sgn.shape, lambda b, c: (0, 0, 0)),
                  pl.BlockSpec((1, LANES), lambda b, c: (0, 0))],
        out_specs=[rspec, pl.BlockSpec((1, H, LANES, LANES), lambda b, c: (b, 0, 0, 0))],
        out_shape=[jax.ShapeDtypeStruct((batch * seq, W), BF16),
                   jax.ShapeDtypeStruct((batch, H, LANES, LANES), F32)],
        scratch_shapes=[pltpu.VMEM((H, LANES, LANES), F32),
                        pltpu.VMEM((_heads_per_iter(H), C, LANES), F32)],
        compiler_params=_cparams(("parallel", "arbitrary")),
        name=name,
    )(q, k, g, v, gate, tri, lmap, sgn, gon)


def _gla_sample_kernel(nlev, n_heads, n_new, q_ref, k_ref, g_ref, v_ref, gt_ref, tri_ref, lmap_ref, sgn_ref,
                       gon_ref, s0_ref, o_ref, sfin_ref, gs_scr):
    nb = SAMPLE_PAD // n_new
    zpad = jnp.zeros((SAMPLE_PAD - n_new, LANES), F32)
    hpi = _heads_per_iter(n_heads)

    def body(j, carry):
        heads = [j * hpi + u for u in range(hpi)]
        lanes = [pl.ds(pl.multiple_of(h * LANES, LANES), LANES) for h in heads]
        chains = []
        for h, hs in zip(heads, lanes):
            q = q_ref[:, hs].astype(F32)
            k = k_ref[:, hs].astype(F32)
            g = g_ref[:, hs]
            v = v_ref[:, hs].astype(F32)
            for bb in range(nb):
                rs = slice(bb * n_new, (bb + 1) * n_new)
                pad = lambda t: jnp.concatenate([t[rs], zpad], axis=0)
                chains.append((pad(q).astype(BF16), pad(k).astype(BF16), pad(g), pad(v).astype(BF16),
                               s0_ref[bb, h].T))
        res = _gla_core(chains, tri_ref[...], lmap_ref[...], sgn_ref, nlev, gs_scr)
        for u, (h, hs) in enumerate(zip(heads, lanes)):
            gate = gt_ref[:, hs].astype(F32)
            outs = []
            for bb in range(nb):
                o, st_new = res[u * nb + bb]
                sfin_ref[bb, h] = st_new.T
                outs.append(_gla_out(o[0:n_new], gon_ref[...], gate[bb * n_new:(bb + 1) * n_new]))
            o_ref[:, hs] = jnp.concatenate(outs, axis=0).astype(BF16)
        return carry

    lax.fori_loop(0, n_heads // hpi, body, 0)


def _gla_sample(q, k, g, v, gate, gon, s0, row0, n_seq, n_new, name):
    W = q.shape[1]
    H = W // LANES
    nb = SAMPLE_PAD // n_new
    tri, lmap, sgn, nlev = _gla_consts(SAMPLE_PAD)
    blk0 = row0 // SAMPLE_PAD
    rspec = pl.BlockSpec((SAMPLE_PAD, W), lambda i: (blk0 + i, 0))
    sspec = pl.BlockSpec((nb, H, LANES, LANES), lambda i: (i, 0, 0, 0))
    return pl.pallas_call(
        functools.partial(_gla_sample_kernel, nlev, H, n_new),
        grid=(n_seq // nb,),
        in_specs=[rspec, rspec, rspec, rspec, rspec,
                  pl.BlockSpec(tri.shape, lambda i: (0, 0)),
                  pl.BlockSpec(lmap.shape, lambda i: (0, 0)),
                  pl.BlockSpec(sgn.shape, lambda i: (0, 0, 0)),
                  pl.BlockSpec((1, LANES), lambda i: (0, 0)),
                  sspec],
        out_specs=[pl.BlockSpec((SAMPLE_PAD, W), lambda i: (i, 0)), sspec],
        out_shape=[jax.ShapeDtypeStruct((n_seq * n_new, W), BF16),
                   jax.ShapeDtypeStruct((n_seq, H, LANES, LANES), F32)],
        scratch_shapes=[pltpu.VMEM((_heads_per_iter(H) * nb, SAMPLE_PAD, LANES), F32)],
        compiler_params=_cparams(("parallel",)),
        name=name,
    )(q, k, g, v, gate, tri, lmap, sgn, gon, s0)


def _out_proj_kernel(split, ap_ref, as_ref, w_ref, x_ref, o_ref):
    i = pl.program_id(0)
    last = pl.num_programs(0) - 1

    @pl.when(i < last)
    def _():
        o_ref[...] = x_ref[...] + _wdot(ap_ref[...], w_ref)

    @pl.when(i == last)
    def _():
        a = jnp.concatenate([ap_ref[0:split, :], as_ref[...]], axis=0)
        o_ref[...] = x_ref[...] + _wdot(a, w_ref)


def _out_proj(a_prompt, a_sample, w, layer, x, tm, tn, name):
    MP, K = a_prompt.shape
    MS = a_sample.shape[0]
    M, N = x.shape
    split = MP % tm
    assert M == MP + MS and M % tm == 0 and tm - split == MS
    return pl.pallas_call(
        functools.partial(_out_proj_kernel, split),
        grid=(M // tm, N // tn),
        in_specs=[pl.BlockSpec((tm, K), lambda i, n: (i, 0)),
                  pl.BlockSpec((MS, K), lambda i, n: (0, 0)),
                  _wspec_cols(layer, K, tn),
                  pl.BlockSpec((tm, tn), lambda i, n: (i, n))],
        out_specs=pl.BlockSpec((tm, tn), lambda i, n: (i, n)),
        out_shape=jax.ShapeDtypeStruct((M, N), F32),
        compiler_params=_cparams(("parallel", "arbitrary")),
        name=name,
    )(a_prompt, a_sample, w, x)


def _ffn_kernel(split, x_ref, gn_ref, wg_ref, wu_ref, wd_ref, o_ref, *rest):
    ys_ref = rest[0] if split is not None else None
    h_scr = rest[-1]
    f = pl.program_id(1)

    @pl.when(f == 0)
    def _():
        _rms_to_scratch(x_ref, gn_ref, h_scr)
        o_ref[...] = x_ref[...]

    h = h_scr[...]
    g = _wdot(h, wg_ref)
    u = _wdot(h, wu_ref)
    a = (g * _sigmoid(g) * u).astype(BF16)
    o_ref[...] += _wdot(a, wd_ref)

    if split is not None:
        @pl.when((f == pl.num_programs(1) - 1) & (pl.program_id(0) == pl.num_programs(0) - 1))
        def _():
            ys_ref[...] = o_ref[split:, :]


def _ffn(x, gain, wg, wu, wd, layer, tm, tf, name, prompt_rows=None):
    M, D = x.shape
    FF = wg.shape[2]
    o_spec = pl.BlockSpec((tm, D), lambda i, f: (i, 0))
    if prompt_rows is None:
        split, out_specs, out_shape = None, o_spec, jax.ShapeDtypeStruct((M, D), F32)
    else:
        split = prompt_rows % tm
        ms = M - prompt_rows
        assert tm - split == ms
        out_specs = [o_spec, pl.BlockSpec((ms, D), lambda i, f: (0, 0))]
        out_shape = [jax.ShapeDtypeStruct((prompt_rows, D), F32), jax.ShapeDtypeStruct((ms, D), F32)]
    return pl.pallas_call(
        functools.partial(_ffn_kernel, split),
        grid=(M // tm, FF // tf),
        in_specs=[pl.BlockSpec((tm, D), lambda i, f: (i, 0)),
                  pl.BlockSpec((1, D), lambda i, f: (0, 0)),
                  pl.BlockSpec((None, D, tf), lambda i, f: (layer, 0, f)),
                  pl.BlockSpec((None, D, tf), lambda i, f: (layer, 0, f)),
                  pl.BlockSpec((None, tf, D), lambda i, f: (layer, f, 0))],
        out_specs=out_specs,
        out_shape=out_shape,
        scratch_shapes=[pltpu.VMEM((tm, D), BF16)],
        compiler_params=_cparams(("arbitrary", "arbitrary")),
        name=name,
    )(x, gain, wg, wu, wd)


def _head_norm_rope(acc, gain, cos, sin, scale):
    r = lax.broadcasted_iota(jnp.int32, (LANES, LANES), 0)
    c = lax.broadcasted_iota(jnp.int32, (LANES, LANES), 1)
    seg = jnp.where((r >> 6) == (c >> 6), 1.0, 0.0).astype(BF16)
    ss = _dot((acc * acc).astype(BF16), seg)
    y = acc * lax.rsqrt(ss * (1.0 / ATTN_HEAD_DIM) + RMS_EPS) * gain
    lane = lax.broadcasted_iota(jnp.int32, y.shape, 1)
    half = ATTN_HEAD_DIM // 2
    rot = jnp.where((lane & half) == 0, pltpu.roll(y, LANES - half, 1), pltpu.roll(y, half, 1))
    out = y * cos + rot * sin
    return out * scale if scale != 1.0 else out


def _q_proj_kernel(x_ref, gn_ref, w_ref, hg_ref, cos_ref, sin_ref, q_ref, h_scr):
    @pl.when(pl.program_id(1) == 0)
    def _():
        _rms_to_scratch(x_ref, gn_ref, h_scr)

    acc = _wdot(h_scr[...], w_ref)
    for c in range(acc.shape[1] // LANES):
        cs = slice(c * LANES, (c + 1) * LANES)
        q_ref[:, cs] = _head_norm_rope(acc[:, cs], hg_ref[...], cos_ref[...], sin_ref[...],
                                       ATTN_SCALE).astype(BF16)


def _q_proj(x, gain, w, layer, head_gain, cos, sin, tm, tn, name):
    M, D = x.shape
    N = w.shape[2]
    return pl.pallas_call(
        _q_proj_kernel,
        grid=(M // tm, N // tn),
        in_specs=[pl.BlockSpec((tm, D), lambda i, n: (i, 0)),
                  pl.BlockSpec((1, D), lambda i, n: (0, 0)),
                  _wspec_cols(layer, D, tn),
                  pl.BlockSpec((1, LANES), lambda i, n: (0, 0)),
                  pl.BlockSpec((tm, LANES), lambda i, n: (i, 0)),
                  pl.BlockSpec((tm, LANES), lambda i, n: (i, 0))],
        out_specs=pl.BlockSpec((tm, tn), lambda i, n: (i, n)),
        out_shape=jax.ShapeDtypeStruct((M, N), BF16),
        scratch_shapes=[pltpu.VMEM((tm, D), BF16)],
        compiler_params=_cparams(("parallel", "arbitrary")),
        name=name,
    )(x, gain, w, head_gain, cos, sin)


def _kv_proj_kernel(x_ref, gn_ref, wk_ref, wv_ref, hg_ref, cos_ref, sin_ref, k_ref, v_ref, h_scr):
    @pl.when(pl.program_id(1) == 0)
    def _():
        _rms_to_scratch(x_ref, gn_ref, h_scr)

    h = h_scr[...]
    acc = _wdot(h, wk_ref)
    for c in range(acc.shape[1] // LANES):
        cs = slice(c * LANES, (c + 1) * LANES)
        k_ref[:, cs] = _head_norm_rope(acc[:, cs], hg_ref[...], cos_ref[...], sin_ref[...],
                                       1.0).astype(BF16)
    v_ref[...] = _wdot(h, wv_ref).astype(BF16)


def _kv_proj(x, gain, wk, wv, head_gain, cos, sin, tm, tn):
    M, D = x.shape
    N = wk.shape[1]
    wspec = pl.BlockSpec((D, tn), lambda i, n: (0, n))
    ospec = pl.BlockSpec((tm, tn), lambda i, n: (i, n))
    return pl.pallas_call(
        _kv_proj_kernel,
        grid=(M // tm, N // tn),
        in_specs=[pl.BlockSpec((tm, D), lambda i, n: (i, 0)),
                  pl.BlockSpec((1, D), lambda i, n: (0, 0)),
                  wspec, wspec,
                  pl.BlockSpec((1, LANES), lambda i, n: (0, 0)),
                  pl.BlockSpec((tm, LANES), lambda i, n: (i, 0)),
                  pl.BlockSpec((tm, LANES), lambda i, n: (i, 0))],
        out_specs=[ospec, ospec],
        out_shape=[jax.ShapeDtypeStruct((M, N), BF16), jax.ShapeDtypeStruct((M, N), BF16)],
        scratch_shapes=[pltpu.VMEM((tm, D), BF16)],
        compiler_params=_cparams(("parallel", "arbitrary")),
        name="kv_proj",
    )(x, gain, wk, wv, head_gain, cos, sin)


def _softmax_sink_pv(s_parts, v_parts, sink):
    m = sink
    for s in s_parts:
        m = jnp.maximum(m, jnp.max(s, axis=-1, keepdims=True))
    den = jnp.exp(sink - m)
    o = None
    for s, v in zip(s_parts, v_parts):
        p = jnp.exp(s - m)
        den = den + jnp.sum(p, axis=-1, keepdims=True)
        pv = _dot(p.astype(BF16), v)
        o = pv if o is None else o + pv
    return o / den


def _swa_prompt_kernel(sk_ref, q_ref, kp_ref, kc_ref, vp_ref, vc_ref, o_ref):
    i = pl.program_id(1)
    W = WINDOW
    row = lax.broadcasted_iota(jnp.int32, (W, 2 * W), 0)
    col = lax.broadcasted_iota(jnp.int32, (W, 2 * W), 1)
    first_key = jnp.where(i > 0, 0, W)
    mask = (col > row) & (col <= row + W) & (col >= first_key)
    lo = lax.broadcasted_iota(jnp.int32, (W, LANES), 1) < ATTN_HEAD_DIM
    for g in range(ATTN_KV_HEADS):
        gs = slice(g * LANES, (g + 1) * LANES)
        kd = jnp.concatenate([kp_ref[:, gs], kc_ref[:, gs]], axis=0)
        vd = jnp.concatenate([vp_ref[:, gs], vc_ref[:, gs]], axis=0)
        for c in range(GQA_GROUP // 2):
            cs = slice((2 * g + c) * LANES, (2 * g + c + 1) * LANES)
            qc = q_ref[:, cs]
            halves = []
            for par in range(2):
                qm = jnp.where(lo, qc, jnp.zeros_like(qc)) if par == 0 else jnp.where(lo, jnp.zeros_like(qc), qc)
                s = jnp.where(mask, _dot_nt(qm, kd), MASK_VALUE)
                halves.append(_softmax_sink_pv([s], [vd], sk_ref[GQA_GROUP * g + 2 * c + par]))
            o_ref[:, cs] = jnp.where(lo, halves[0], halves[1]).astype(BF16)


def _swa_prompt(q, kdup, vdup, sinks, batch, seq, name):
    W = WINDOW
    nb = seq // W
    QW = q.shape[1]
    KW = kdup.shape[1]
    prev = lambda b, i: (b * nb + jnp.maximum(i - 1, 0), 0)
    cur = lambda b, i: (b * nb + i, 0)
    return pl.pallas_call(
        _swa_prompt_kernel,
        grid=(batch, nb),
        in_specs=[pl.BlockSpec(memory_space=pltpu.SMEM),
                  pl.BlockSpec((W, QW), cur),
                  pl.BlockSpec((W, KW), prev), pl.BlockSpec((W, KW), cur),
                  pl.BlockSpec((W, KW), prev), pl.BlockSpec((W, KW), cur)],
        out_specs=pl.BlockSpec((W, QW), cur),
        out_shape=jax.ShapeDtypeStruct((batch * seq, QW), BF16),
        compiler_params=_cparams(("parallel", "arbitrary")),
        name=name,
    )(sinks, q, kdup, kdup, vdup, vdup)


def _swa_sample_kernel(n_new, sk_ref, q_ref, kn_ref, vn_ref, kc_ref, vc_ref, o_ref):
    W = WINDOW
    nb = SAMPLE_PAD // n_new
    rows = GQA_GROUP * n_new
    qi = lax.broadcasted_iota(jnp.int32, (rows, W), 0) & (n_new - 1)
    mask_c = lax.broadcasted_iota(jnp.int32, (rows, W), 1) > qi
    qi_n = lax.broadcasted_iota(jnp.int32, (rows, SAMPLE_PAD), 0) & (n_new - 1)
    mask_n = lax.broadcasted_iota(jnp.int32, (rows, SAMPLE_PAD), 1) <= qi_n
    hrow = lax.broadcasted_iota(jnp.int32, (rows, 1), 0) // n_new
    lo = lax.broadcasted_iota(jnp.int32, (n_new, LANES), 1) < ATTN_HEAD_DIM
    zq = jnp.zeros((n_new, LANES), F32)
    zpad = jnp.zeros((SAMPLE_PAD - n_new, LANES), F32)
    q = q_ref[...].astype(F32)
    kn = kn_ref[...].astype(F32)
    vn = vn_ref[...].astype(F32)
    out_rows = []
    for bb in range(nb):
        rs = slice(bb * n_new, (bb + 1) * n_new)
        chunks = []
        for g in range(ATTN_KV_HEADS):
            gs = slice(g * LANES, (g + 1) * LANES)
            kc = kc_ref[bb, :, gs]
            vc = vc_ref[bb, :, gs]
            kn_g = jnp.concatenate([kn[rs, gs], zpad], axis=0).astype(BF16)
            vn_g = jnp.concatenate([vn[rs, gs], zpad], axis=0).astype(BF16)
            lhs = []
            sink = jnp.zeros((rows, 1), F32)
            for a in range(GQA_GROUP):
                c, par = divmod(a, 2)
                cs = slice((2 * g + c) * LANES, (2 * g + c + 1) * LANES)
                qc = q[rs, cs]
                lhs.append(jnp.where(lo, qc, zq) if par == 0 else jnp.where(lo, zq, qc))
                sink = jnp.where(hrow == a, sk_ref[GQA_GROUP * g + a], sink)
            lhs = jnp.concatenate(lhs, axis=0).astype(BF16)
            s_c = jnp.where(mask_c, _dot_nt(lhs, kc), MASK_VALUE)
            s_n = jnp.where(mask_n, _dot_nt(lhs, kn_g), MASK_VALUE)
            o = _softmax_sink_pv([s_c, s_n], [vc, vn_g], sink)
            for c in range(GQA_GROUP // 2):
                chunks.append(jnp.where(lo, o[(2 * c) * n_new:(2 * c + 1) * n_new],
                                        o[(2 * c + 1) * n_new:(2 * c + 2) * n_new]))
        out_rows.append(jnp.concatenate(chunks, axis=1))
    o_ref[...] = jnp.concatenate(out_rows, axis=0).astype(BF16)


def _swa_sample(q, kdup, vdup, cache_kdup, cache_vdup, sinks, row0, n_seq, n_new, name):
    W = WINDOW
    QW = q.shape[1]
    KW = kdup.shape[1]
    nb = SAMPLE_PAD // n_new
    blk0 = row0 // SAMPLE_PAD
    rmap = lambda i: (blk0 + i, 0)
    cspec = pl.BlockSpec((nb, W, KW), lambda i: (i, 0, 0))
    return pl.pallas_call(
        functools.partial(_swa_sample_kernel, n_new),
        grid=(n_seq // nb,),
        in_specs=[pl.BlockSpec(memory_space=pltpu.SMEM),
                  pl.BlockSpec((SAMPLE_PAD, QW), rmap),
                  pl.BlockSpec((SAMPLE_PAD, KW), rmap), pl.BlockSpec((SAMPLE_PAD, KW), rmap),
                  cspec, cspec],
        out_specs=pl.BlockSpec((SAMPLE_PAD, QW), lambda i: (i, 0)),
        out_shape=jax.ShapeDtypeStruct((n_seq * n_new, QW), BF16),
        compiler_params=_cparams(("parallel",)),
        name=name,
    )(sinks, q, kdup, vdup, cache_kdup, cache_vdup)


def _dup_heads(w):
    lead = w.shape[:-1]
    w3 = w.reshape(lead + (ATTN_KV_HEADS, ATTN_HEAD_DIM))
    return jnp.concatenate([w3, w3], axis=-1).reshape(lead + (ATTN_KV_HEADS * 2 * ATTN_HEAD_DIM,))


def _undup_heads(a, lead):
    return a.reshape(lead + (ATTN_KV_HEADS, 2 * ATTN_HEAD_DIM))[..., :ATTN_HEAD_DIM].astype(F32)


def _rope_tables(pos):
    half = ATTN_HEAD_DIM // 2
    inv = ROPE_THETA ** (-jnp.arange(half, dtype=F32) / half)
    ang = pos.astype(F32)[:, None] * inv[None, :]
    cos = jnp.cos(ang)
    sin = jnp.sin(ang)
    reps = LANES // ATTN_HEAD_DIM
    cos_t = jnp.tile(jnp.concatenate([cos, cos], axis=-1), (1, reps))
    sin_t = jnp.tile(jnp.concatenate([-sin, sin], axis=-1), (1, reps))
    return cos_t, sin_t


def kernel(x_prompt, x_sample, state_hgrn, cache_k_win, cache_v_win, hgrn_norm, hgrn_wq, hgrn_wf, hgrn_wi, hgrn_wg, hgrn_lb_logits, hgrn_onorm, hgrn_wo, kv_norm, w_k, w_v, k_norm, attn_norm, attn_wq, q_norm, sinks, attn_wo, ffn_norm, w_gate, w_up, w_down):
    B, T, D = x_prompt.shape
    SB, S, _ = x_sample.shape
    MP = B * T
    MS = SB * S
    n_a = hgrn_wq.shape[0]
    n_b = attn_wq.shape[0]
    assert SAMPLE_PAD % S == 0 and SB % (SAMPLE_PAD // S) == 0 and MP % SAMPLE_PAD == 0
    assert T % GLA_CHUNK == 0 and T % WINDOW == 0

    x = jnp.concatenate([x_prompt.reshape(MP, D), x_sample.reshape(MS, D)], axis=0)
    M = MP + MS
    tm = 768 if M % 768 == 0 else 256
    tn = 512
    tn_hgrn = 256
    tf = 256

    bf = lambda w: w.astype(BF16)
    f32 = lambda w: w.astype(F32)
    row = lambda g: g.reshape(1, -1).astype(F32)
    pos = jnp.concatenate([jnp.tile(jnp.arange(T), B), jnp.tile(PAST_LEN + jnp.arange(S), SB)])
    cos_t, sin_t = _rope_tables(pos)
    head_gain = lambda g: jnp.tile(g.astype(F32), LANES // ATTN_HEAD_DIM).reshape(1, LANES)

    st_prompt, st_sample = [], []
    for l in range(n_a):
        q, k, lf, v, gate = _hgrn_proj(x, row(hgrn_norm[l]), f32(hgrn_wq), f32(hgrn_wf), f32(hgrn_wi),
                                       f32(hgrn_wg), f32(hgrn_lb_logits), l, tm, tn_hgrn)
        gon = row(hgrn_onorm[l])
        o_p, s_p = _gla_prompt(q, k, lf, v, gate, gon, B, T, f"gla_prompt_{l}")
        o_s, s_s = _gla_sample(q, k, lf, v, gate, gon, state_hgrn[l].astype(F32), MP, SB, S, f"gla_sample_{l}")
        st_prompt.append(s_p)
        st_sample.append(s_s)
        x = _out_proj(o_p, o_s, f32(hgrn_wo), l, x, tm, tn, f"hgrn_out_{l}")
        x = _ffn(x, row(ffn_norm[l]), f32(w_gate), f32(w_up), f32(w_down), l, tm, tf, f"ffn_{l}")

    kdup, vdup = _kv_proj(x, row(kv_norm), f32(_dup_heads(w_k)), f32(_dup_heads(w_v)), head_gain(k_norm),
                          cos_t, sin_t, tm, tn)
    KW = ATTN_KV_HEADS * ATTN_HEAD_DIM
    cache_kdup = bf(_dup_heads(cache_k_win.reshape(SB, WINDOW, KW)))
    cache_vdup = bf(_dup_heads(cache_v_win.reshape(SB, WINDOW, KW)))
    for j in range(n_b):
        l = n_a + j
        q = _q_proj(x, row(attn_norm[j]), f32(attn_wq), j, head_gain(q_norm[j]), cos_t, sin_t, tm, tn,
                    f"q_proj_{j}")
        sk = sinks[j].astype(F32)
        a_p = _swa_prompt(q, kdup, vdup, sk, B, T, f"swa_prompt_{j}")
        a_s = _swa_sample(q, kdup, vdup, cache_kdup, cache_vdup, sk, MP, SB, S, f"swa_sample_{j}")
        x = _out_proj(a_p, a_s, f32(attn_wo), j, x, tm, tn, f"attn_out_{j}")
        last = j == n_b - 1
        x = _ffn(x, row(ffn_norm[l]), f32(w_gate), f32(w_up), f32(w_down), l, tm, tf, f"ffn_{l}",
                 prompt_rows=MP if last else None)

    y_prompt = x[0].reshape(B, T, D)
    y_sample = x[1].reshape(SB, S, D)
    k_win_p = _undup_heads(kdup[:MP].reshape(B, T, -1)[:, T - WINDOW:], (B, WINDOW))
    v_win_p = _undup_heads(vdup[:MP].reshape(B, T, -1)[:, T - WINDOW:], (B, WINDOW))
    k_new_s = _undup_heads(kdup[MP:].reshape(SB, S, -1), (SB, S))
    v_new_s = _undup_heads(vdup[MP:].reshape(SB, S, -1), (SB, S))
    k_win_s = jnp.concatenate([cache_k_win[:, S:].astype(F32), k_new_s], axis=1)
    v_win_s = jnp.concatenate([cache_v_win[:, S:].astype(F32), v_new_s], axis=1)
    return (y_prompt, y_sample, jnp.stack(st_prompt), jnp.stack(st_sample),
            k_win_p, v_win_p, k_win_s, v_win_s)
```

```python
import functools
import math

import numpy as np
import jax
import jax.numpy as jnp
from jax import lax
from jax.experimental import pallas as pl
from jax.experimental.pallas import tpu as pltpu

HGRN_HEADS = 16
HEAD_DK = 128
ATTN_HEAD_DIM = 64
ATTN_KV_HEADS = 8
GQA_GROUP = 4
WINDOW = 128
ROPE_THETA = 10000.0
ATTN_SCALE = ATTN_HEAD_DIM ** -0.5
RMS_EPS = 1e-6
MASK_VALUE = -1e30
MIN_FORGET = 1e-30
PAST_LEN = 16384

LANES = 128
VMEM_LIMIT_BYTES = 56 * 1024 * 1024

GLA_CHUNK = 128
SAMPLE_PAD = 16
GLA_HEADS_PER_ITER = 8


def _heads_per_iter(n_heads):
    return math.gcd(n_heads, GLA_HEADS_PER_ITER)

F32 = jnp.float32
BF16 = jnp.bfloat16


def _cparams(sem):
    return pltpu.CompilerParams(dimension_semantics=sem, vmem_limit_bytes=VMEM_LIMIT_BYTES)


def _dot(a, b):
    return jnp.dot(a, b, preferred_element_type=F32)


def _dot_nt(a, b):
    return lax.dot_general(a, b, (((1,), (1,)), ((), ())), preferred_element_type=F32)


def _dot_tn(a, b):
    return lax.dot_general(a, b, (((0,), (0,)), ((), ())), preferred_element_type=F32)


def _wdot(a, w_ref):
    return _dot(a, w_ref[...].astype(BF16))


def _wspec_cols(w, layer, tn):
    k = w.shape[-2]
    if w.ndim == 3:
        return pl.BlockSpec((None, k, tn), lambda i, n: (layer, 0, n))
    return pl.BlockSpec((k, tn), lambda i, n: (0, n))


def _sigmoid(x):
    return 1.0 / (1.0 + jnp.exp(-x))


def _rider_specs(riders, steps_per_batch, n_steps):
    in_specs, out_specs, out_shapes = [], [], []
    for w, layer in riders:
        _, R, N = w.shape
        rb = next(r for r in range(16, R + 1, 16) if R % r == 0 and R // r <= n_steps)
        last = R // rb - 1
        blk = lambda b, c, last=last: jnp.minimum(b * steps_per_batch + c, last)
        in_specs.append(pl.BlockSpec((None, rb, N), lambda b, c, layer=layer, blk=blk: (layer, blk(b, c), 0)))
        out_specs.append(pl.BlockSpec((rb, N), lambda b, c, blk=blk: (blk(b, c), 0)))
        out_shapes.append(jax.ShapeDtypeStruct((R, N), BF16))
    return in_specs, out_specs, out_shapes


def _run_riders(in_refs, out_refs):
    for i_ref, o_ref in zip(in_refs, out_refs):
        o_ref[...] = i_ref[...].astype(BF16)


def _rms_to_scratch(x_ref, g_ref, h_scr):
    x = x_ref[...]
    ms = jnp.mean(x * x, axis=-1, keepdims=True)
    h_scr[...] = (x * lax.rsqrt(ms + RMS_EPS) * g_ref[...]).astype(BF16)


def _hgrn_proj_kernel(layer, x_ref, gn_ref, wq_ref, wf_ref, wi_ref, wg_ref, lbl_ref,
                      q_ref, k_ref, lf_ref, v_ref, gt_ref, h_scr):
    @pl.when(pl.program_id(1) == 0)
    def _():
        _rms_to_scratch(x_ref, gn_ref, h_scr)

    h = h_scr[...]
    aq = _wdot(h, wq_ref)
    q_ref[...] = (aq * _sigmoid(aq) * (HEAD_DK ** -0.5)).astype(BF16)

    lg = lbl_ref[...]
    mx = jnp.max(lg, axis=0, keepdims=True)
    e = jnp.exp(lg - mx)
    sm = e / jnp.sum(e, axis=0, keepdims=True)
    cs = sm[0:1]
    for r in range(1, layer + 1):
        cs = cs + sm[r:r + 1]
    lb = cs - sm[0:1]

    z = _wdot(h, wf_ref)
    ez = jnp.exp(-jnp.abs(z))
    r = 1.0 / (1.0 + ez)
    pos = z >= 0.0
    sig_p = jnp.where(pos, r, ez * r)
    sig_n = jnp.where(pos, ez * r, r)
    f = lb + (1.0 - lb) * sig_p
    lf_ref[...] = jnp.log(jnp.maximum(f, MIN_FORGET))
    k_ref[...] = ((1.0 - lb) * sig_n).astype(BF16)

    v_ref[...] = _wdot(h, wi_ref).astype(BF16)
    ag = _wdot(h, wg_ref)
    gt_ref[...] = (ag * _sigmoid(ag)).astype(BF16)


def _hgrn_proj(x, gain, wq, wf, wi, wg, lb_logits, layer, tm, tn):
    M, D = x.shape
    N = wq.shape[-1]
    L = lb_logits.shape[0]
    grid = (M // tm, N // tn)
    wspec = _wspec_cols(wq, layer, tn)
    ospec = pl.BlockSpec((tm, tn), lambda i, n: (i, n))
    return pl.pallas_call(
        functools.partial(_hgrn_proj_kernel, layer),
        grid=grid,
        in_specs=[pl.BlockSpec((tm, D), lambda i, n: (i, 0)),
                  pl.BlockSpec((1, D), lambda i, n: (0, 0)),
                  wspec, wspec, wspec, wspec,
                  pl.BlockSpec((L, tn), lambda i, n: (0, n))],
        out_specs=[ospec] * 5,
        out_shape=[jax.ShapeDtypeStruct((M, N), BF16),
                   jax.ShapeDtypeStruct((M, N), BF16),
                   jax.ShapeDtypeStruct((M, N), F32),
                   jax.ShapeDtypeStruct((M, N), BF16),
                   jax.ShapeDtypeStruct((M, N), BF16)],
        scratch_shapes=[pltpu.VMEM((tm, D), BF16)],
        compiler_params=_cparams(("parallel", "arbitrary")),
        name=f"hgrn_proj_{layer}",
    )(x, gain, wq, wf, wi, wg, lb_logits)


def _gla_consts(C):
    nlev = int(math.log2(C))
    tri = np.tril(np.ones((C, C), np.float32))
    ii, jj = np.meshgrid(np.arange(C), np.arange(C), indexing="ij")
    x = ii ^ jj
    lev = np.floor(np.log2(np.maximum(x, 1))).astype(np.int32)
    lmap = np.where(ii == jj, -1, np.where(ii > jj, lev, -2)).astype(np.int32)
    rows = np.arange(C)
    sgn = np.stack([np.where((rows >> l) & 1 == 1, 1.0, 0.0 if l == 0 else -1.0) for l in range(nlev)])
    sgn = np.broadcast_to((sgn * math.log2(math.e))[:, :, None], (nlev, C, LANES)).astype(np.float32)
    return jnp.asarray(tri, BF16), jnp.asarray(lmap, jnp.int32), jnp.asarray(sgn), nlev


SUBLANES = 8


def _level_exponent(l, g, gcum, gs_ref, sgn):
    C = g.shape[0]
    h = 1 << l
    if l == 0:
        return g * sgn
    sub = lax.broadcasted_iota(jnp.int32, (SUBLANES, LANES), 0)
    bcast = lambda r: jnp.broadcast_to(gs_ref[pl.ds(r, 1), :], (SUBLANES, LANES))
    pieces = []
    for base in range(0, C, SUBLANES):
        if 2 * h >= SUBLANES:
            pieces.append(bcast((base // (2 * h)) * 2 * h + h - 1))
        else:
            p = bcast(base + h - 1)
            for blk in range(1, SUBLANES // (2 * h)):
                p = jnp.where(sub >= blk * 2 * h, bcast(base + blk * 2 * h + h - 1), p)
            pieces.append(p)
    gr = jnp.concatenate(pieces, axis=0)
    return (gcum - gr) * sgn


def _gla_core(chains, tri, lmap, sgn_ref, nlev, gs_scr):
    C = chains[0][0].shape[0]
    gcums = []
    for u, (q, k, g, v, st) in enumerate(chains):
        ghi = g.astype(BF16)
        glo = (g - ghi.astype(F32)).astype(BF16)
        c2 = _dot(tri, jnp.concatenate([ghi, glo], axis=1))
        gcum = c2[:, :LANES] + c2[:, LANES:]
        gs_scr[u] = gcum
        gcums.append(gcum)
    accs = [jnp.where(lmap == -1, _dot_nt(q, k), 0.0) for (q, k, g, v, st) in chains]
    qk32 = [(q.astype(F32), k.astype(F32)) for (q, k, g, v, st) in chains]
    for l in range(nlev):
        for u, (q, k, g, v, st) in enumerate(chains):
            e = jnp.exp2(_level_exponent(l, g, gcums[u], gs_scr.at[u], sgn_ref[l]))
            qf, kf = qk32[u]
            accs[u] = jnp.where(lmap == l, _dot_nt((qf * e).astype(BF16), (kf * e).astype(BF16)), accs[u])
    outs = []
    for u, (q, k, g, v, st) in enumerate(chains):
        gcum = gcums[u]
        qf, kf = qk32[u]
        o = _dot(accs[u].astype(BF16), v)
        o = o + _dot_nt((qf * jnp.exp(gcum)).astype(BF16), st.astype(BF16))
        glast = gcum[C - 1:C, :]
        kdec = (kf * jnp.exp(glast - gcum)).astype(BF16)
        st_new = st * jnp.exp(glast) + _dot_tn(v, kdec)
        outs.append((o, st_new))
    return outs


def _gla_out(o, gon, gate):
    ms = jnp.mean(o * o, axis=-1, keepdims=True)
    return o * lax.rsqrt(ms + RMS_EPS) * gon * gate


def _gla_prompt_kernel(nlev, n_heads, n_riders, q_ref, k_ref, g_ref, v_ref, gt_ref, tri_ref, lmap_ref,
                       sgn_ref, gon_ref, *rest):
    rider_in, rest = rest[:n_riders], rest[n_riders:]
    o_ref, sfin_ref = rest[0], rest[1]
    rider_out = rest[2:2 + n_riders]
    st_scr, gs_scr = rest[2 + n_riders:]
    _run_riders(rider_in, rider_out)
    c = pl.program_id(1)
    hpi = _heads_per_iter(n_heads)

    @pl.when(c == 0)
    def _():
        st_scr[...] = jnp.zeros_like(st_scr)

    def body(j, carry):
        heads = [j * hpi + u for u in range(hpi)]
        lanes = [pl.ds(pl.multiple_of(h * LANES, LANES), LANES) for h in heads]
        chains = [(q_ref[:, hs], k_ref[:, hs], g_ref[:, hs], v_ref[:, hs], st_scr[h])
                  for h, hs in zip(heads, lanes)]
        res = _gla_core(chains, tri_ref[...], lmap_ref[...], sgn_ref, nlev, gs_scr)
        for h, hs, (o, st_new) in zip(heads, lanes, res):
            st_scr[h] = st_new
            o_ref[:, hs] = _gla_out(o, gon_ref[...], gt_ref[:, hs].astype(F32)).astype(BF16)
        return carry

    lax.fori_loop(0, n_heads // hpi, body, 0)

    @pl.when(c == pl.num_programs(1) - 1)
    def _():
        def wb(h, carry):
            sfin_ref[0, h] = st_scr[h].T
            return carry
        lax.fori_loop(0, n_heads, wb, 0)


def _gla_prompt(q, k, g, v, gate, gon, batch, seq, riders, name):
    C = GLA_CHUNK
    W = q.shape[1]
    H = W // LANES
    nc = seq // C
    tri, lmap, sgn, nlev = _gla_consts(C)
    rspec = pl.BlockSpec((C, W), lambda b, c: (b * nc + c, 0))
    r_in, r_out, r_shape = _rider_specs(riders, nc, batch * nc)
    return pl.pallas_call(
        functools.partial(_gla_prompt_kernel, nlev, H, len(riders)),
        grid=(batch, nc),
        in_specs=[rspec, rspec, rspec, rspec, rspec,
                  pl.BlockSpec(tri.shape, lambda b, c: (0, 0)),
                  pl.BlockSpec(lmap.shape, lambda b, c: (0, 0)),
                  pl.BlockSpec(sgn.shape, lambda b, c: (0, 0, 0)),
                  pl.BlockSpec((1, LANES), lambda b, c: (0, 0))] + r_in,
        out_specs=[rspec, pl.BlockSpec((1, H, LANES, LANES), lambda b, c: (b, 0, 0, 0))] + r_out,
        out_shape=[jax.ShapeDtypeStruct((batch * seq, W), BF16),
                   jax.ShapeDtypeStruct((batch, H, LANES, LANES), F32)] + r_shape,
        scratch_shapes=[pltpu.VMEM((H, LANES, LANES), F32),
                        pltpu.VMEM((_heads_per_iter(H), C, LANES), F32)],
        compiler_params=_cparams(("arbitrary", "arbitrary")),
        name=name,
    )(q, k, g, v, gate, tri, lmap, sgn, gon, *[w for w, _ in riders])


def _gla_sample_kernel(nlev, n_heads, n_new, q_ref, k_ref, g_ref, v_ref, gt_ref, tri_ref, lmap_ref, sgn_ref,
                       gon_ref, s0_ref, o_ref, sfin_ref, gs_scr):
    nb = SAMPLE_PAD // n_new
    zpad = jnp.zeros((SAMPLE_PAD - n_new, LANES), F32)
    hpi = _heads_per_iter(n_heads)

    def body(j, carry):
        heads = [j * hpi + u for u in range(hpi)]
        lanes = [pl.ds(pl.multiple_of(h * LANES, LANES), LANES) for h in heads]
        chains = []
        for h, hs in zip(heads, lanes):
            q = q_ref[:, hs].astype(F32)
            k = k_ref[:, hs].astype(F32)
            g = g_ref[:, hs]
            v = v_ref[:, hs].astype(F32)
            for bb in range(nb):
                rs = slice(bb * n_new, (bb + 1) * n_new)
                pad = lambda t: jnp.concatenate([t[rs], zpad], axis=0)
                chains.append((pad(q).astype(BF16), pad(k).astype(BF16), pad(g), pad(v).astype(BF16),
                               s0_ref[bb, h].T))
        res = _gla_core(chains, tri_ref[...], lmap_ref[...], sgn_ref, nlev, gs_scr)
        for u, (h, hs) in enumerate(zip(heads, lanes)):
            gate = gt_ref[:, hs].astype(F32)
            outs = []
            for bb in range(nb):
                o, st_new = res[u * nb + bb]
                sfin_ref[bb, h] = st_new.T
                outs.append(_gla_out(o[0:n_new], gon_ref[...], gate[bb * n_new:(bb + 1) * n_new]))
            o_ref[:, hs] = jnp.concatenate(outs, axis=0).astype(BF16)
        return carry

    lax.fori_loop(0, n_heads // hpi, body, 0)


def _gla_sample(q, k, g, v, gate, gon, s0, layer, row0, n_seq, n_new, name):
    W = q.shape[1]
    H = W // LANES
    nb = SAMPLE_PAD // n_new
    tri, lmap, sgn, nlev = _gla_consts(SAMPLE_PAD)
    blk0 = row0 // SAMPLE_PAD
    rspec = pl.BlockSpec((SAMPLE_PAD, W), lambda i: (blk0 + i, 0))
    sspec = pl.BlockSpec((nb, H, LANES, LANES), lambda i: (i, 0, 0, 0))
    s0spec = pl.BlockSpec((None, nb, H, LANES, LANES), lambda i: (layer, i, 0, 0, 0))
    return pl.pallas_call(
        functools.partial(_gla_sample_kernel, nlev, H, n_new),
        grid=(n_seq // nb,),
        in_specs=[rspec, rspec, rspec, rspec, rspec,
                  pl.BlockSpec(tri.shape, lambda i: (0, 0)),
                  pl.BlockSpec(lmap.shape, lambda i: (0, 0)),
                  pl.BlockSpec(sgn.shape, lambda i: (0, 0, 0)),
                  pl.BlockSpec((1, LANES), lambda i: (0, 0)),
                  s0spec],
        out_specs=[pl.BlockSpec((SAMPLE_PAD, W), lambda i: (i, 0)), sspec],
        out_shape=[jax.ShapeDtypeStruct((n_seq * n_new, W), BF16),
                   jax.ShapeDtypeStruct((n_seq, H, LANES, LANES), F32)],
        scratch_shapes=[pltpu.VMEM((_heads_per_iter(H) * nb, SAMPLE_PAD, LANES), F32)],
        compiler_params=_cparams(("parallel",)),
        name=name,
    )(q, k, g, v, gate, tri, lmap, sgn, gon, s0)


def _out_proj_kernel(split, ap_ref, as_ref, w_ref, x_ref, o_ref):
    i = pl.program_id(0)
    last = pl.num_programs(0) - 1

    @pl.when(i < last)
    def _():
        o_ref[...] = x_ref[...] + _wdot(ap_ref[...], w_ref)

    @pl.when(i == last)
    def _():
        a = jnp.concatenate([ap_ref[0:split, :], as_ref[...]], axis=0)
        o_ref[...] = x_ref[...] + _wdot(a, w_ref)


def _out_proj(a_prompt, a_sample, w, x, tm, name):
    MP, K = a_prompt.shape
    MS = a_sample.shape[0]
    M, N = x.shape
    split = MP % tm
    assert M == MP + MS and M % tm == 0 and tm - split == MS
    return pl.pallas_call(
        functools.partial(_out_proj_kernel, split),
        grid=(M // tm,),
        in_specs=[pl.BlockSpec((tm, K), lambda i: (i, 0)),
                  pl.BlockSpec((MS, K), lambda i: (0, 0)),
                  pl.BlockSpec((K, N), lambda i: (0, 0)),
                  pl.BlockSpec((tm, N), lambda i: (i, 0))],
        out_specs=pl.BlockSpec((tm, N), lambda i: (i, 0)),
        out_shape=jax.ShapeDtypeStruct((M, N), F32),
        compiler_params=_cparams(("parallel",)),
        name=name,
    )(a_prompt, a_sample, w, x)


def _ffn_kernel(split, x_ref, gn_ref, wg_ref, wu_ref, wd_ref, o_ref, *rest):
    ys_ref = rest[0] if split is not None else None
    h_scr = rest[-1]
    f = pl.program_id(1)

    @pl.when(f == 0)
    def _():
        _rms_to_scratch(x_ref, gn_ref, h_scr)
        o_ref[...] = x_ref[...]

    h = h_scr[...]
    g = _wdot(h, wg_ref)
    u = _wdot(h, wu_ref)
    a = (g * _sigmoid(g) * u).astype(BF16)
    o_ref[...] += _wdot(a, wd_ref)

    if split is not None:
        @pl.when((f == pl.num_programs(1) - 1) & (pl.program_id(0) == pl.num_programs(0) - 1))
        def _():
            ys_ref[...] = o_ref[split:, :]


def _ffn(x, gain, wg, wu, wd, tm, tf, name, prompt_rows=None):
    M, D = x.shape
    FF = wg.shape[1]
    o_spec = pl.BlockSpec((tm, D), lambda i, f: (i, 0))
    if prompt_rows is None:
        split, out_specs, out_shape = None, o_spec, jax.ShapeDtypeStruct((M, D), F32)
    else:
        split = prompt_rows % tm
        ms = M - prompt_rows
        assert tm - split == ms
        out_specs = [o_spec, pl.BlockSpec((ms, D), lambda i, f: (0, 0))]
        out_shape = [jax.ShapeDtypeStruct((prompt_rows, D), F32), jax.ShapeDtypeStruct((ms, D), F32)]
    return pl.pallas_call(
        functools.partial(_ffn_kernel, split),
        grid=(M // tm, FF // tf),
        in_specs=[pl.BlockSpec((tm, D), lambda i, f: (i, 0)),
                  pl.BlockSpec((1, D), lambda i, f: (0, 0)),
                  pl.BlockSpec((D, tf), lambda i, f: (0, f)),
                  pl.BlockSpec((D, tf), lambda i, f: (0, f)),
                  pl.BlockSpec((tf, D), lambda i, f: (f, 0))],
        out_specs=out_specs,
        out_shape=out_shape,
        scratch_shapes=[pltpu.VMEM((tm, D), BF16)],
        compiler_params=_cparams(("arbitrary", "arbitrary")),
        name=name,
    )(x, gain, wg, wu, wd)


def _head_norm_rope(acc, gain, cos, sin, scale):
    r = lax.broadcasted_iota(jnp.int32, (LANES, LANES), 0)
    c = lax.broadcasted_iota(jnp.int32, (LANES, LANES), 1)
    seg = jnp.where((r >> 6) == (c >> 6), 1.0, 0.0).astype(BF16)
    ss = _dot((acc * acc).astype(BF16), seg)
    y = acc * lax.rsqrt(ss * (1.0 / ATTN_HEAD_DIM) + RMS_EPS) * gain
    lane = lax.broadcasted_iota(jnp.int32, y.shape, 1)
    half = ATTN_HEAD_DIM // 2
    rot = jnp.where((lane & half) == 0, pltpu.roll(y, LANES - half, 1), pltpu.roll(y, half, 1))
    out = y * cos + rot * sin
    return out * scale if scale != 1.0 else out


def _rope_cols(h, w_ref, hg_ref, cos_ref, sin_ref, o_ref, tn, scale):
    for n in range(w_ref.shape[1] // tn):
        acc = _dot(h, w_ref[:, n * tn:(n + 1) * tn])
        for c in range(tn // LANES):
            cs = slice(c * LANES, (c + 1) * LANES)
            os_ = slice(n * tn + c * LANES, n * tn + (c + 1) * LANES)
            o_ref[:, os_] = _head_norm_rope(acc[:, cs], hg_ref[...], cos_ref[...], sin_ref[...],
                                            scale).astype(BF16)


def _q_proj_kernel(tn, x_ref, gn_ref, w_ref, hg_ref, cos_ref, sin_ref, q_ref, h_scr):
    _rms_to_scratch(x_ref, gn_ref, h_scr)
    _rope_cols(h_scr[...], w_ref, hg_ref, cos_ref, sin_ref, q_ref, tn, ATTN_SCALE)


def _q_proj(x, gain, w, head_gain, cos, sin, tm, tn, name):
    M, D = x.shape
    N = w.shape[1]
    return pl.pallas_call(
        functools.partial(_q_proj_kernel, tn),
        grid=(M // tm,),
        in_specs=[pl.BlockSpec((tm, D), lambda i: (i, 0)),
                  pl.BlockSpec((1, D), lambda i: (0, 0)),
                  pl.BlockSpec((D, N), lambda i: (0, 0)),
                  pl.BlockSpec((1, LANES), lambda i: (0, 0)),
                  pl.BlockSpec((tm, LANES), lambda i: (i, 0)),
                  pl.BlockSpec((tm, LANES), lambda i: (i, 0))],
        out_specs=pl.BlockSpec((tm, N), lambda i: (i, 0)),
        out_shape=jax.ShapeDtypeStruct((M, N), BF16),
        scratch_shapes=[pltpu.VMEM((tm, D), BF16)],
        compiler_params=_cparams(("parallel",)),
        name=name,
    )(x, gain, w, head_gain, cos, sin)


def _kv_proj_kernel(tn, x_ref, gn_ref, wk_ref, wv_ref, hg_ref, cos_ref, sin_ref, k_ref, v_ref, h_scr):
    _rms_to_scratch(x_ref, gn_ref, h_scr)
    h = h_scr[...]
    _rope_cols(h, wk_ref, hg_ref, cos_ref, sin_ref, k_ref, tn, 1.0)
    for n in range(wv_ref.shape[1] // tn):
        ns = slice(n * tn, (n + 1) * tn)
        v_ref[:, ns] = _dot(h, wv_ref[:, ns]).astype(BF16)


def _kv_proj(x, gain, wk, wv, head_gain, cos, sin, tm, tn):
    M, D = x.shape
    N = wk.shape[1]
    wspec = pl.BlockSpec((D, N), lambda i: (0, 0))
    ospec = pl.BlockSpec((tm, N), lambda i: (i, 0))
    return pl.pallas_call(
        functools.partial(_kv_proj_kernel, tn),
        grid=(M // tm,),
        in_specs=[pl.BlockSpec((tm, D), lambda i: (i, 0)),
                  pl.BlockSpec((1, D), lambda i: (0, 0)),
                  wspec, wspec,
                  pl.BlockSpec((1, LANES), lambda i: (0, 0)),
                  pl.BlockSpec((tm, LANES), lambda i: (i, 0)),
                  pl.BlockSpec((tm, LANES), lambda i: (i, 0))],
        out_specs=[ospec, ospec],
        out_shape=[jax.ShapeDtypeStruct((M, N), BF16), jax.ShapeDtypeStruct((M, N), BF16)],
        scratch_shapes=[pltpu.VMEM((tm, D), BF16)],
        compiler_params=_cparams(("parallel",)),
        name="kv_proj",
    )(x, gain, wk, wv, head_gain, cos, sin)


def _softmax_sink_pv(s_parts, v_parts, sink):
    m = sink
    for s in s_parts:
        m = jnp.maximum(m, jnp.max(s, axis=-1, keepdims=True))
    den = jnp.exp(sink - m)
    o = None
    for s, v in zip(s_parts, v_parts):
        p = jnp.exp(s - m)
        den = den + jnp.sum(p, axis=-1, keepdims=True)
        pv = _dot(p.astype(BF16), v)
        o = pv if o is None else o + pv
    return o / den


def _swa_prompt_kernel(n_riders, sk_ref, q_ref, kp_ref, kc_ref, vp_ref, vc_ref, *rest):
    rider_in, o_ref, rider_out = rest[:n_riders], rest[n_riders], rest[n_riders + 1:]
    _run_riders(rider_in, rider_out)
    i = pl.program_id(1)
    W = WINDOW
    row = lax.broadcasted_iota(jnp.int32, (W, 2 * W), 0)
    col = lax.broadcasted_iota(jnp.int32, (W, 2 * W), 1)
    first_key = jnp.where(i > 0, 0, W)
    mask = (col > row) & (col <= row + W) & (col >= first_key)
    lo = lax.broadcasted_iota(jnp.int32, (W, LANES), 1) < ATTN_HEAD_DIM
    for g in range(ATTN_KV_HEADS):
        gs = slice(g * LANES, (g + 1) * LANES)
        kd = jnp.concatenate([kp_ref[:, gs], kc_ref[:, gs]], axis=0)
        vd = jnp.concatenate([vp_ref[:, gs], vc_ref[:, gs]], axis=0)
        for c in range(GQA_GROUP // 2):
            cs = slice((2 * g + c) * LANES, (2 * g + c + 1) * LANES)
            qc = q_ref[:, cs]
            halves = []
            for par in range(2):
                qm = jnp.where(lo, qc, jnp.zeros_like(qc)) if par == 0 else jnp.where(lo, jnp.zeros_like(qc), qc)
                s = jnp.where(mask, _dot_nt(qm, kd), MASK_VALUE)
                halves.append(_softmax_sink_pv([s], [vd], sk_ref[GQA_GROUP * g + 2 * c + par]))
            o_ref[:, cs] = jnp.where(lo, halves[0], halves[1]).astype(BF16)


def _swa_prompt(q, kdup, vdup, sinks, batch, seq, riders, name):
    W = WINDOW
    nb = seq // W
    QW = q.shape[1]
    KW = kdup.shape[1]
    prev = lambda b, i: (b * nb + jnp.maximum(i - 1, 0), 0)
    cur = lambda b, i: (b * nb + i, 0)
    r_in, r_out, r_shape = _rider_specs(riders, nb, batch * nb)
    return pl.pallas_call(
        functools.partial(_swa_prompt_kernel, len(riders)),
        grid=(batch, nb),
        in_specs=[pl.BlockSpec(memory_space=pltpu.SMEM),
                  pl.BlockSpec((W, QW), cur),
                  pl.BlockSpec((W, KW), prev), pl.BlockSpec((W, KW), cur),
                  pl.BlockSpec((W, KW), prev), pl.BlockSpec((W, KW), cur)] + r_in,
        out_specs=[pl.BlockSpec((W, QW), cur)] + r_out,
        out_shape=[jax.ShapeDtypeStruct((batch * seq, QW), BF16)] + r_shape,
        compiler_params=_cparams(("arbitrary", "arbitrary")),
        name=name,
    )(sinks, q, kdup, kdup, vdup, vdup, *[w for w, _ in riders])


def _swa_sample_kernel(n_new, sk_ref, q_ref, kn_ref, vn_ref, kc_ref, vc_ref, o_ref):
    W = WINDOW
    nb = SAMPLE_PAD // n_new
    rows = GQA_GROUP * n_new
    qi = lax.broadcasted_iota(jnp.int32, (rows, W), 0) & (n_new - 1)
    mask_c = lax.broadcasted_iota(jnp.int32, (rows, W), 1) > qi
    qi_n = lax.broadcasted_iota(jnp.int32, (rows, SAMPLE_PAD), 0) & (n_new - 1)
    mask_n = lax.broadcasted_iota(jnp.int32, (rows, SAMPLE_PAD), 1) <= qi_n
    hrow = lax.broadcasted_iota(jnp.int32, (rows, 1), 0) // n_new
    lo = lax.broadcasted_iota(jnp.int32, (n_new, LANES), 1) < ATTN_HEAD_DIM
    zq = jnp.zeros((n_new, LANES), F32)
    zpad = jnp.zeros((SAMPLE_PAD - n_new, LANES), F32)
    q = q_ref[...].astype(F32)
    kn = kn_ref[...].astype(F32)
    vn = vn_ref[...].astype(F32)
    out_rows = []
    for bb in range(nb):
        rs = slice(bb * n_new, (bb + 1) * n_new)
        chunks = []
        for g in range(ATTN_KV_HEADS):
            gs = slice(g * LANES, (g + 1) * LANES)
            kc = kc_ref[bb, :, gs]
            vc = vc_ref[bb, :, gs]
            kn_g = jnp.concatenate([kn[rs, gs], zpad], axis=0).astype(BF16)
            vn_g = jnp.concatenate([vn[rs, gs], zpad], axis=0).astype(BF16)
            lhs = []
            sink = jnp.zeros((rows, 1), F32)
            for a in range(GQA_GROUP):
                c, par = divmod(a, 2)
                cs = slice((2 * g + c) * LANES, (2 * g + c + 1) * LANES)
                qc = q[rs, cs]
                lhs.append(jnp.where(lo, qc, zq) if par == 0 else jnp.where(lo, zq, qc))
                sink = jnp.where(hrow == a, sk_ref[GQA_GROUP * g + a], sink)
            lhs = jnp.concatenate(lhs, axis=0).astype(BF16)
            s_c = jnp.where(mask_c, _dot_nt(lhs, kc), MASK_VALUE)
            s_n = jnp.where(mask_n, _dot_nt(lhs, kn_g), MASK_VALUE)
            o = _softmax_sink_pv([s_c, s_n], [vc, vn_g], sink)
            for c in range(GQA_GROUP // 2):
                chunks.append(jnp.where(lo, o[(2 * c) * n_new:(2 * c + 1) * n_new],
                                        o[(2 * c + 1) * n_new:(2 * c + 2) * n_new]))
        out_rows.append(jnp.concatenate(chunks, axis=1))
    o_ref[...] = jnp.concatenate(out_rows, axis=0).astype(BF16)


def _swa_sample(q, kdup, vdup, cache_kdup, cache_vdup, sinks, row0, n_seq, n_new, name):
    W = WINDOW
    QW = q.shape[1]
    KW = kdup.shape[1]
    nb = SAMPLE_PAD // n_new
    blk0 = row0 // SAMPLE_PAD
    rmap = lambda i: (blk0 + i, 0)
    cspec = pl.BlockSpec((nb, W, KW), lambda i: (i, 0, 0))
    return pl.pallas_call(
        functools.partial(_swa_sample_kernel, n_new),
        grid=(n_seq // nb,),
        in_specs=[pl.BlockSpec(memory_space=pltpu.SMEM),
                  pl.BlockSpec((SAMPLE_PAD, QW), rmap),
                  pl.BlockSpec((SAMPLE_PAD, KW), rmap), pl.BlockSpec((SAMPLE_PAD, KW), rmap),
                  cspec, cspec],
        out_specs=pl.BlockSpec((SAMPLE_PAD, QW), lambda i: (i, 0)),
        out_shape=jax.ShapeDtypeStruct((n_seq * n_new, QW), BF16),
        compiler_params=_cparams(("parallel",)),
        name=name,
    )(sinks, q, kdup, vdup, cache_kdup, cache_vdup)


def _dup_heads(w):
    lead = w.shape[:-1]
    w3 = w.reshape(lead + (ATTN_KV_HEADS, ATTN_HEAD_DIM))
    return jnp.concatenate([w3, w3], axis=-1).reshape(lead + (ATTN_KV_HEADS * 2 * ATTN_HEAD_DIM,))


def _undup_heads(a, lead):
    return a.reshape(lead + (ATTN_KV_HEADS, 2 * ATTN_HEAD_DIM))[..., :ATTN_HEAD_DIM].astype(F32)


def _rope_tables(pos):
    half = ATTN_HEAD_DIM // 2
    inv = ROPE_THETA ** (-jnp.arange(half, dtype=F32) / half)
    ang = pos.astype(F32)[:, None] * inv[None, :]
    cos = jnp.cos(ang)
    sin = jnp.sin(ang)
    reps = LANES // ATTN_HEAD_DIM
    cos_t = jnp.tile(jnp.concatenate([cos, cos], axis=-1), (1, reps))
    sin_t = jnp.tile(jnp.concatenate([-sin, sin], axis=-1), (1, reps))
    return cos_t, sin_t


def kernel(x_prompt, x_sample, state_hgrn, cache_k_win, cache_v_win, hgrn_norm, hgrn_wq, hgrn_wf, hgrn_wi, hgrn_wg, hgrn_lb_logits, hgrn_onorm, hgrn_wo, kv_norm, w_k, w_v, k_norm, attn_norm, attn_wq, q_norm, sinks, attn_wo, ffn_norm, w_gate, w_up, w_down):
    B, T, D = x_prompt.shape
    SB, S, _ = x_sample.shape
    MP = B * T
    MS = SB * S
    n_a = hgrn_wq.shape[0]
    n_b = attn_wq.shape[0]
    assert SAMPLE_PAD % S == 0 and SB % (SAMPLE_PAD // S) == 0 and MP % SAMPLE_PAD == 0
    assert T % GLA_CHUNK == 0 and T % WINDOW == 0

    x = jnp.concatenate([x_prompt.reshape(MP, D), x_sample.reshape(MS, D)], axis=0)
    M = MP + MS
    tm = 768 if M % 768 == 0 else 256
    tn = 512
    tn_f32 = 256
    tf = 512

    bf = lambda w: w.astype(BF16)
    f32 = lambda w: w.astype(F32)
    row = lambda g: g.reshape(1, -1).astype(F32)
    pos = jnp.concatenate([jnp.tile(jnp.arange(T), B), jnp.tile(PAST_LEN + jnp.arange(S), SB)])
    cos_t, sin_t = _rope_tables(pos)
    head_gain = lambda g: jnp.tile(g.astype(F32), LANES // ATTN_HEAD_DIM).reshape(1, LANES)

    w_gate32, w_up32, w_down32 = f32(w_gate), f32(w_up), f32(w_down)
    hgrn_w32 = [f32(hgrn_wq), f32(hgrn_wf), f32(hgrn_wi), f32(hgrn_wg)]
    hgrn_wo32, attn_wq32, attn_wo32 = f32(hgrn_wo), f32(attn_wq), f32(attn_wo)
    ffn_riders = lambda l: [(w_gate32, l), (w_up32, l), (w_down32, l)]

    st_prompt, st_sample = [], []
    s0 = f32(state_hgrn)
    proj_w = hgrn_w32
    attn_wq_bf = None
    for l in range(n_a):
        q, k, lf, v, gate = _hgrn_proj(x, row(hgrn_norm[l]), *proj_w, f32(hgrn_lb_logits), l, tm,
                                       tn_f32 if proj_w[0].ndim == 3 else tn)
        gon = row(hgrn_onorm[l])
        riders = ffn_riders(l) + [(hgrn_wo32, l)]
        if l + 1 < n_a:
            riders += [(w, l + 1) for w in hgrn_w32]
        elif n_b > 0:
            riders += [(attn_wq32, 0)]
        o_p, s_p, wg_bf, wu_bf, wd_bf, wo_bf, *nxt = _gla_prompt(q, k, lf, v, gate, gon, B, T, riders,
                                                                 f"gla_prompt_{l}")
        if l + 1 < n_a:
            proj_w = nxt
        elif n_b > 0:
            attn_wq_bf = nxt[0]
        o_s, s_s = _gla_sample(q, k, lf, v, gate, gon, s0, l, MP, SB, S, f"gla_sample_{l}")
        st_prompt.append(s_p)
        st_sample.append(s_s)
        x = _out_proj(o_p, o_s, wo_bf, x, tm, f"hgrn_out_{l}")
        x = _ffn(x, row(ffn_norm[l]), wg_bf, wu_bf, wd_bf, tm, tf, f"ffn_{l}")

    kdup, vdup = _kv_proj(x, row(kv_norm), bf(_dup_heads(w_k)), bf(_dup_heads(w_v)), head_gain(k_norm),
                          cos_t, sin_t, tm, tn)
    KW = ATTN_KV_HEADS * ATTN_HEAD_DIM
    cache_kdup = bf(_dup_heads(cache_k_win.reshape(SB, WINDOW, KW)))
    cache_vdup = bf(_dup_heads(cache_v_win.reshape(SB, WINDOW, KW)))
    for j in range(n_b):
        l = n_a + j
        if attn_wq_bf is None:
            attn_wq_bf = bf(attn_wq[j])
        q = _q_proj(x, row(attn_norm[j]), attn_wq_bf, head_gain(q_norm[j]), cos_t, sin_t, tm, tn,
                    f"q_proj_{j}")
        sk = sinks[j].astype(F32)
        riders = ffn_riders(l) + [(attn_wo32, j)] + ([(attn_wq32, j + 1)] if j + 1 < n_b else [])
        a_p, wg_bf, wu_bf, wd_bf, wo_bf, *nxt = _swa_prompt(q, kdup, vdup, sk, B, T, riders, f"swa_prompt_{j}")
        attn_wq_bf = nxt[0] if nxt else None
        a_s = _swa_sample(q, kdup, vdup, cache_kdup, cache_vdup, sk, MP, SB, S, f"swa_sample_{j}")
        x = _out_proj(a_p, a_s, wo_bf, x, tm, f"attn_out_{j}")
        last = j == n_b - 1
        x = _ffn(x, row(ffn_norm[l]), wg_bf, wu_bf, wd_bf, tm, tf, f"ffn_{l}",
                 prompt_rows=MP if last else None)

    y_prompt = x[0].reshape(B, T, D)
    y_sample = x[1].reshape(SB, S, D)
    k_win_p = _undup_heads(kdup[:MP].reshape(B, T, -1)[:, T - WINDOW:], (B, WINDOW))
    v_win_p = _undup_heads(vdup[:MP].reshape(B, T, -1)[:, T - WINDOW:], (B, WINDOW))
    k_new_s = _undup_heads(kdup[MP:].reshape(SB, S, -1), (SB, S))
    v_new_s = _undup_heads(vdup[MP:].reshape(SB, S, -1), (SB, S))
    k_win_s = jnp.concatenate([cache_k_win[:, S:].astype(F32), k_new_s], axis=1)
    v_win_s = jnp.concatenate([cache_v_win[:, S:].astype(F32), v_new_s], axis=1)
    return (y_prompt, y_sample, jnp.stack(st_prompt), jnp.stack(st_sample),
            k_win_p, v_win_p, k_win_s, v_win_s)
```

```python
import functools
import math

import numpy as np
import jax
import jax.numpy as jnp
from jax import lax
from jax.experimental import pallas as pl
from jax.experimental.pallas import tpu as pltpu

HGRN_HEADS = 16
HEAD_DK = 128
ATTN_HEAD_DIM = 64
ATTN_KV_HEADS = 8
GQA_GROUP = 4
WINDOW = 128
ROPE_THETA = 10000.0
ATTN_SCALE = ATTN_HEAD_DIM ** -0.5
LOG2E = math.log2(math.e)
RMS_EPS = 1e-6
MASK_VALUE = -1e30
MIN_FORGET = 1e-30
PAST_LEN = 16384

LANES = 128
VMEM_LIMIT_BYTES = 56 * 1024 * 1024

GLA_CHUNK = 128
SAMPLE_PAD = 16
GLA_HEADS_PER_ITER = 8


def _heads_per_iter(n_heads):
    return math.gcd(n_heads, GLA_HEADS_PER_ITER)

F32 = jnp.float32
BF16 = jnp.bfloat16


def _cparams(sem):
    return pltpu.CompilerParams(dimension_semantics=sem, vmem_limit_bytes=VMEM_LIMIT_BYTES)


def _dot(a, b):
    return jnp.dot(a, b, preferred_element_type=F32)


def _dot_nt(a, b):
    return lax.dot_general(a, b, (((1,), (1,)), ((), ())), preferred_element_type=F32)


def _dot_tn(a, b):
    return lax.dot_general(a, b, (((0,), (0,)), ((), ())), preferred_element_type=F32)


def _wdot(a, w_ref):
    return _dot(a, w_ref[...].astype(BF16))


def _wspec_cols(w, layer, tn):
    k = w.shape[-2]
    if w.ndim == 3:
        return pl.BlockSpec((None, k, tn), lambda i, n: (layer, 0, n))
    return pl.BlockSpec((k, tn), lambda i, n: (0, n))


def _sigmoid(x):
    return 1.0 / (1.0 + jnp.exp(-x))


def _rider_specs(riders, steps_per_batch, n_steps):
    in_specs, out_specs, out_shapes = [], [], []
    for w, layer in riders:
        _, R, N = w.shape
        rb = next(r for r in range(16, R + 1, 16) if R % r == 0 and R // r <= n_steps)
        last = R // rb - 1
        blk = lambda b, c, last=last: jnp.minimum(b * steps_per_batch + c, last)
        in_specs.append(pl.BlockSpec((None, rb, N), lambda b, c, layer=layer, blk=blk: (layer, blk(b, c), 0)))
        out_specs.append(pl.BlockSpec((rb, N), lambda b, c, blk=blk: (blk(b, c), 0)))
        out_shapes.append(jax.ShapeDtypeStruct((R, N), BF16))
    return in_specs, out_specs, out_shapes


def _run_riders(in_refs, out_refs):
    for i_ref, o_ref in zip(in_refs, out_refs):
        o_ref[...] = i_ref[...].astype(BF16)


def _rms_store(x, g_ref, h_scr):
    ms = jnp.mean(x * x, axis=-1, keepdims=True)
    h_scr[...] = (x * lax.rsqrt(ms + RMS_EPS) * g_ref[...]).astype(BF16)


def _rms_to_scratch(x_ref, g_ref, h_scr):
    _rms_store(x_ref[...], g_ref, h_scr)


def _row_split(prompt_rows, sample_rows, tm):
    split = prompt_rows % tm
    assert (prompt_rows + sample_rows) % tm == 0 and tm - split == sample_rows
    return split


def _on_row_tile(split, p_ref, s_ref, fn):
    i = pl.program_id(0)
    last = pl.num_programs(0) - 1

    @pl.when(i < last)
    def _():
        fn(p_ref[...])

    @pl.when(i == last)
    def _():
        fn(jnp.concatenate([p_ref[0:split, :], s_ref[...]], axis=0))


def _hgrn_proj_kernel(layer, split, x_ref, *rest):
    xs_ref, rest = (rest[0], rest[1:]) if split is not None else (None, rest)
    gn_ref, wq_ref, wf_ref, wi_ref, wg_ref, lbl_ref, q_ref, k_ref, lf_ref, v_ref, gt_ref, h_scr = rest

    @pl.when(pl.program_id(1) == 0)
    def _():
        if split is None:
            _rms_to_scratch(x_ref, gn_ref, h_scr)
        else:
            _on_row_tile(split, x_ref, xs_ref, lambda x: _rms_store(x, gn_ref, h_scr))

    h = h_scr[...]
    aq = _wdot(h, wq_ref)
    q_ref[...] = (aq * _sigmoid(aq) * (HEAD_DK ** -0.5)).astype(BF16)

    lg = lbl_ref[...]
    mx = jnp.max(lg, axis=0, keepdims=True)
    e = jnp.exp(lg - mx)
    sm = e / jnp.sum(e, axis=0, keepdims=True)
    cs = sm[0:1]
    for r in range(1, layer + 1):
        cs = cs + sm[r:r + 1]
    lb = cs - sm[0:1]

    z = _wdot(h, wf_ref)
    ez = jnp.exp(-jnp.abs(z))
    r = 1.0 / (1.0 + ez)
    pos = z >= 0.0
    sig_p = jnp.where(pos, r, ez * r)
    sig_n = jnp.where(pos, ez * r, r)
    f = lb + (1.0 - lb) * sig_p
    lf_ref[...] = jnp.log(jnp.maximum(f, MIN_FORGET))
    k_ref[...] = ((1.0 - lb) * sig_n).astype(BF16)

    ag = _wdot(h, wg_ref)
    gt_ref[...] = (ag * _sigmoid(ag)).astype(BF16)
    v_ref[...] = _wdot(h, wi_ref).astype(BF16)


def _hgrn_proj(x, gain, wq, wf, wi, wg, lb_logits, layer, tm, tn):
    N = wq.shape[-1]
    L = lb_logits.shape[0]
    if isinstance(x, tuple):
        xp, xs = x
        D = xp.shape[1]
        M = xp.shape[0] + xs.shape[0]
        split = _row_split(xp.shape[0], xs.shape[0], tm)
        x_args = [xp, xs]
        x_specs = [pl.BlockSpec((tm, D), lambda i, n: (i, 0)), pl.BlockSpec(xs.shape, lambda i, n: (0, 0))]
    else:
        M, D = x.shape
        split, x_args, x_specs = None, [x], [pl.BlockSpec((tm, D), lambda i, n: (i, 0))]
    grid = (M // tm, N // tn)
    wspec = _wspec_cols(wq, layer, tn)
    ospec = pl.BlockSpec((tm, tn), lambda i, n: (i, n))
    return pl.pallas_call(
        functools.partial(_hgrn_proj_kernel, layer, split),
        grid=grid,
        in_specs=x_specs + [
                  pl.BlockSpec((1, D), lambda i, n: (0, 0)),
                  wspec, wspec, wspec, wspec,
                  pl.BlockSpec((L, tn), lambda i, n: (0, n))],
        out_specs=[ospec] * 5,
        out_shape=[jax.ShapeDtypeStruct((M, N), BF16),
                   jax.ShapeDtypeStruct((M, N), BF16),
                   jax.ShapeDtypeStruct((M, N), F32),
                   jax.ShapeDtypeStruct((M, N), BF16),
                   jax.ShapeDtypeStruct((M, N), BF16)],
        scratch_shapes=[pltpu.VMEM((tm, D), BF16)],
        compiler_params=_cparams(("parallel", "arbitrary")),
        name=f"hgrn_proj_{layer}",
    )(*x_args, gain, wq, wf, wi, wg, lb_logits)


def _gla_consts(C):
    nlev = int(math.log2(C))
    tri = np.tril(np.ones((C, C), np.float32))
    ii, jj = np.meshgrid(np.arange(C), np.arange(C), indexing="ij")
    x = ii ^ jj
    lev = np.floor(np.log2(np.maximum(x, 1))).astype(np.int32)
    lmap = np.where(ii == jj, -1, np.where(ii > jj, lev, -2)).astype(np.int32)
    rows = np.arange(C)
    sgn = np.stack([np.where((rows >> l) & 1 == 1, 1.0, 0.0 if l == 0 else -1.0) for l in range(nlev)])
    sgn = np.broadcast_to((sgn * math.log2(math.e))[:, :, None], (nlev, C, LANES)).astype(np.float32)
    return jnp.asarray(tri, BF16), jnp.asarray(lmap, jnp.int32), jnp.asarray(sgn), nlev


SUBLANES = 8


def _level_exponent(l, g, gcum, gs_ref, sgn):
    C = g.shape[0]
    h = 1 << l
    if l == 0:
        return g * sgn
    sub = lax.broadcasted_iota(jnp.int32, (SUBLANES, LANES), 0)
    bcast = lambda r: jnp.broadcast_to(gs_ref[pl.ds(r, 1), :], (SUBLANES, LANES))
    pieces = []
    for base in range(0, C, SUBLANES):
        if 2 * h >= SUBLANES:
            pieces.append(bcast((base // (2 * h)) * 2 * h + h - 1))
        else:
            p = bcast(base + h - 1)
            for blk in range(1, SUBLANES // (2 * h)):
                p = jnp.where(sub >= blk * 2 * h, bcast(base + blk * 2 * h + h - 1), p)
            pieces.append(p)
    gr = jnp.concatenate(pieces, axis=0)
    return (gcum - gr) * sgn


def _gla_core(chains, tri, lmap, sgn_ref, nlev, gs_scr):
    C = chains[0][0].shape[0]
    gcums = []
    for u, (q, k, g, v, st) in enumerate(chains):
        ghi = g.astype(BF16)
        glo = (g - ghi.astype(F32)).astype(BF16)
        c2 = _dot(tri, jnp.concatenate([ghi, glo], axis=1))
        gcum = c2[:, :LANES] + c2[:, LANES:]
        gs_scr[u] = gcum
        gcums.append(gcum)
    accs = [jnp.where(lmap == -1, _dot_nt(q, k), 0.0) for (q, k, g, v, st) in chains]
    qk32 = [(q.astype(F32), k.astype(F32)) for (q, k, g, v, st) in chains]
    for l in range(nlev):
        for u, (q, k, g, v, st) in enumerate(chains):
            e = jnp.exp2(_level_exponent(l, g, gcums[u], gs_scr.at[u], sgn_ref[l]))
            qf, kf = qk32[u]
            accs[u] = jnp.where(lmap == l, _dot_nt((qf * e).astype(BF16), (kf * e).astype(BF16)), accs[u])
    outs = []
    for u, (q, k, g, v, st) in enumerate(chains):
        gcum = gcums[u]
        qf, kf = qk32[u]
        o = _dot(accs[u].astype(BF16), v)
        o = o + _dot_nt((qf * jnp.exp(gcum)).astype(BF16), st.astype(BF16))
        glast = gcum[C - 1:C, :]
        kdec = (kf * jnp.exp(glast - gcum)).astype(BF16)
        st_new = st * jnp.exp(glast) + _dot_tn(v, kdec)
        outs.append((o, st_new))
    return outs


def _gla_out(o, gon, gate):
    ms = jnp.mean(o * o, axis=-1, keepdims=True)
    return o * lax.rsqrt(ms + RMS_EPS) * gon * gate


def _gla_prompt_kernel(nlev, n_heads, n_riders, q_ref, k_ref, g_ref, v_ref, gt_ref, tri_ref, lmap_ref,
                       sgn_ref, gon_ref, *rest):
    rider_in, rest = rest[:n_riders], rest[n_riders:]
    o_ref, sfin_ref = rest[0], rest[1]
    rider_out = rest[2:2 + n_riders]
    st_scr, gs_scr = rest[2 + n_riders:]
    _run_riders(rider_in, rider_out)
    c = pl.program_id(1)
    hpi = _heads_per_iter(n_heads)

    @pl.when(c == 0)
    def _():
        st_scr[...] = jnp.zeros_like(st_scr)

    def body(j, carry):
        heads = [j * hpi + u for u in range(hpi)]
        lanes = [pl.ds(pl.multiple_of(h * LANES, LANES), LANES) for h in heads]
        chains = [(q_ref[:, hs], k_ref[:, hs], g_ref[:, hs], v_ref[:, hs], st_scr[h])
                  for h, hs in zip(heads, lanes)]
        res = _gla_core(chains, tri_ref[...], lmap_ref[...], sgn_ref, nlev, gs_scr)
        for h, hs, (o, st_new) in zip(heads, lanes, res):
            st_scr[h] = st_new
            o_ref[:, hs] = _gla_out(o, gon_ref[...], gt_ref[:, hs].astype(F32)).astype(BF16)
        return carry

    lax.fori_loop(0, n_heads // hpi, body, 0)

    @pl.when(c == pl.num_programs(1) - 1)
    def _():
        def wb(h, carry):
            sfin_ref[0, h] = st_scr[h].T
            return carry
        lax.fori_loop(0, n_heads, wb, 0)


def _gla_prompt(q, k, g, v, gate, gon, batch, seq, riders, name):
    C = GLA_CHUNK
    W = q.shape[1]
    H = W // LANES
    nc = seq // C
    tri, lmap, sgn, nlev = _gla_consts(C)
    rspec = pl.BlockSpec((C, W), lambda b, c: (b * nc + c, 0))
    r_in, r_out, r_shape = _rider_specs(riders, nc, batch * nc)
    return pl.pallas_call(
        functools.partial(_gla_prompt_kernel, nlev, H, len(riders)),
        grid=(batch, nc),
        in_specs=[rspec, rspec, rspec, rspec, rspec,
                  pl.BlockSpec(tri.shape, lambda b, c: (0, 0)),
                  pl.BlockSpec(lmap.shape, lambda b, c: (0, 0)),
                  pl.BlockSpec(sgn.shape, lambda b, c: (0, 0, 0)),
                  pl.BlockSpec((1, LANES), lambda b, c: (0, 0))] + r_in,
        out_specs=[rspec, pl.BlockSpec((1, H, LANES, LANES), lambda b, c: (b, 0, 0, 0))] + r_out,
        out_shape=[jax.ShapeDtypeStruct((batch * seq, W), BF16),
                   jax.ShapeDtypeStruct((batch, H, LANES, LANES), F32)] + r_shape,
        scratch_shapes=[pltpu.VMEM((H, LANES, LANES), F32),
                        pltpu.VMEM((_heads_per_iter(H), C, LANES), F32)],
        compiler_params=_cparams(("arbitrary", "arbitrary")),
        name=name,
    )(q, k, g, v, gate, tri, lmap, sgn, gon, *[w for w, _ in riders])


def _gla_sample_kernel(nlev, n_heads, n_new, q_ref, k_ref, g_ref, v_ref, gt_ref, tri_ref, lmap_ref, sgn_ref,
                       gon_ref, s0_ref, o_ref, sfin_ref, gs_scr):
    nb = SAMPLE_PAD // n_new
    zpad = jnp.zeros((SAMPLE_PAD - n_new, LANES), F32)
    hpi = _heads_per_iter(n_heads)

    def body(j, carry):
        heads = [j * hpi + u for u in range(hpi)]
        lanes = [pl.ds(pl.multiple_of(h * LANES, LANES), LANES) for h in heads]
        chains = []
        for h, hs in zip(heads, lanes):
            q = q_ref[:, hs].astype(F32)
            k = k_ref[:, hs].astype(F32)
            g = g_ref[:, hs]
            v = v_ref[:, hs].astype(F32)
            for bb in range(nb):
                rs = slice(bb * n_new, (bb + 1) * n_new)
                pad = lambda t: jnp.concatenate([t[rs], zpad], axis=0)
                chains.append((pad(q).astype(BF16), pad(k).astype(BF16), pad(g), pad(v).astype(BF16),
                               s0_ref[bb, h].T))
        res = _gla_core(chains, tri_ref[...], lmap_ref[...], sgn_ref, nlev, gs_scr)
        for u, (h, hs) in enumerate(zip(heads, lanes)):
            gate = gt_ref[:, hs].astype(F32)
            outs = []
            for bb in range(nb):
                o, st_new = res[u * nb + bb]
                sfin_ref[bb, h] = st_new.T
                outs.append(_gla_out(o[0:n_new], gon_ref[...], gate[bb * n_new:(bb + 1) * n_new]))
            o_ref[:, hs] = jnp.concatenate(outs, axis=0).astype(BF16)
        return carry

    lax.fori_loop(0, n_heads // hpi, body, 0)


def _gla_sample(q, k, g, v, gate, gon, s0, layer, row0, n_seq, n_new, name):
    W = q.shape[1]
    H = W // LANES
    nb = SAMPLE_PAD // n_new
    tri, lmap, sgn, nlev = _gla_consts(SAMPLE_PAD)
    blk0 = row0 // SAMPLE_PAD
    rspec = pl.BlockSpec((SAMPLE_PAD, W), lambda i: (blk0 + i, 0))
    sspec = pl.BlockSpec((nb, H, LANES, LANES), lambda i: (i, 0, 0, 0))
    s0spec = pl.BlockSpec((None, nb, H, LANES, LANES), lambda i: (layer, i, 0, 0, 0))
    return pl.pallas_call(
        functools.partial(_gla_sample_kernel, nlev, H, n_new),
        grid=(n_seq // nb,),
        in_specs=[rspec, rspec, rspec, rspec, rspec,
                  pl.BlockSpec(tri.shape, lambda i: (0, 0)),
                  pl.BlockSpec(lmap.shape, lambda i: (0, 0)),
                  pl.BlockSpec(sgn.shape, lambda i: (0, 0, 0)),
                  pl.BlockSpec((1, LANES), lambda i: (0, 0)),
                  s0spec],
        out_specs=[pl.BlockSpec((SAMPLE_PAD, W), lambda i: (i, 0)), sspec],
        out_shape=[jax.ShapeDtypeStruct((n_seq * n_new, W), BF16),
                   jax.ShapeDtypeStruct((n_seq, H, LANES, LANES), F32)],
        scratch_shapes=[pltpu.VMEM((_heads_per_iter(H) * nb, SAMPLE_PAD, LANES), F32)],
        compiler_params=_cparams(("parallel",)),
        name=name,
    )(q, k, g, v, gate, tri, lmap, sgn, gon, s0)


def _out_proj_kernel(split, x_is_pair, ap_ref, as_ref, w_ref, x_ref, *rest):
    xs_ref, o_ref = (rest[0], rest[1]) if x_is_pair else (None, rest[0])
    i = pl.program_id(0)
    last = pl.num_programs(0) - 1
    tail = lambda p_ref, s_ref: jnp.concatenate([p_ref[0:split, :], s_ref[...]], axis=0)

    @pl.when(i < last)
    def _():
        o_ref[...] = x_ref[...] + _wdot(ap_ref[...], w_ref)

    @pl.when(i == last)
    def _():
        x = tail(x_ref, xs_ref) if x_is_pair else x_ref[...]
        o_ref[...] = x + _wdot(tail(ap_ref, as_ref), w_ref)


def _out_proj(a_prompt, a_sample, w, x, tm, name):
    MP, K = a_prompt.shape
    MS = a_sample.shape[0]
    N = w.shape[1]
    M = MP + MS
    split = _row_split(MP, MS, tm)
    x_is_pair = isinstance(x, tuple)
    if x_is_pair:
        x_args = list(x)
        x_specs = [pl.BlockSpec((tm, N), lambda i: (i, 0)), pl.BlockSpec((MS, N), lambda i: (0, 0))]
    else:
        x_args, x_specs = [x], [pl.BlockSpec((tm, N), lambda i: (i, 0))]
    return pl.pallas_call(
        functools.partial(_out_proj_kernel, split, x_is_pair),
        grid=(M // tm,),
        in_specs=[pl.BlockSpec((tm, K), lambda i: (i, 0)),
                  pl.BlockSpec((MS, K), lambda i: (0, 0)),
                  pl.BlockSpec((K, N), lambda i: (0, 0))] + x_specs,
        out_specs=pl.BlockSpec((tm, N), lambda i: (i, 0)),
        out_shape=jax.ShapeDtypeStruct((M, N), F32),
        compiler_params=_cparams(("parallel",)),
        name=name,
    )(a_prompt, a_sample, w, *x_args)


def _ffn_kernel(split, x_ref, gn_ref, wg_ref, wu_ref, wd_ref, o_ref, *rest):
    ys_ref = rest[0] if split is not None else None
    h_scr = rest[-1]
    f = pl.program_id(1)

    @pl.when(f == 0)
    def _():
        _rms_to_scratch(x_ref, gn_ref, h_scr)
        o_ref[...] = x_ref[...]

    h = h_scr[...]
    g = _wdot(h, wg_ref)
    u = _wdot(h, wu_ref)
    a = (g * _sigmoid(g) * u).astype(BF16)
    o_ref[...] += _wdot(a, wd_ref)

    if split is not None:
        @pl.when((f == pl.num_programs(1) - 1) & (pl.program_id(0) == pl.num_programs(0) - 1))
        def _():
            ys_ref[...] = o_ref[split:, :]


def _ffn(x, gain, wg, wu, wd, tm, tf, name, prompt_rows=None):
    M, D = x.shape
    FF = wg.shape[1]
    o_spec = pl.BlockSpec((tm, D), lambda i, f: (i, 0))
    if prompt_rows is None:
        split, out_specs, out_shape = None, o_spec, jax.ShapeDtypeStruct((M, D), F32)
    else:
        split = prompt_rows % tm
        ms = M - prompt_rows
        assert tm - split == ms
        out_specs = [o_spec, pl.BlockSpec((ms, D), lambda i, f: (0, 0))]
        out_shape = [jax.ShapeDtypeStruct((prompt_rows, D), F32), jax.ShapeDtypeStruct((ms, D), F32)]
    return pl.pallas_call(
        functools.partial(_ffn_kernel, split),
        grid=(M // tm, FF // tf),
        in_specs=[pl.BlockSpec((tm, D), lambda i, f: (i, 0)),
                  pl.BlockSpec((1, D), lambda i, f: (0, 0)),
                  pl.BlockSpec((D, tf), lambda i, f: (0, f)),
                  pl.BlockSpec((D, tf), lambda i, f: (0, f)),
                  pl.BlockSpec((tf, D), lambda i, f: (f, 0))],
        out_specs=out_specs,
        out_shape=out_shape,
        scratch_shapes=[pltpu.VMEM((tm, D), BF16)],
        compiler_params=_cparams(("arbitrary", "arbitrary")),
        name=name,
    )(x, gain, wg, wu, wd)


def _head_norm_rope(acc, gain, cos, sin, scale):
    r = lax.broadcasted_iota(jnp.int32, (LANES, LANES), 0)
    c = lax.broadcasted_iota(jnp.int32, (LANES, LANES), 1)
    seg = jnp.where((r >> 6) == (c >> 6), 1.0, 0.0).astype(BF16)
    ss = _dot((acc * acc).astype(BF16), seg)
    y = acc * lax.rsqrt(ss * (1.0 / ATTN_HEAD_DIM) + RMS_EPS) * gain
    lane = lax.broadcasted_iota(jnp.int32, y.shape, 1)
    half = ATTN_HEAD_DIM // 2
    rot = jnp.where((lane & half) == 0, pltpu.roll(y, LANES - half, 1), pltpu.roll(y, half, 1))
    out = y * cos + rot * sin
    return out * scale if scale != 1.0 else out


def _rope_cols(h, w_ref, hg_ref, cos_ref, sin_ref, o_ref, tn, scale):
    for n in range(w_ref.shape[1] // tn):
        acc = _dot(h, w_ref[:, n * tn:(n + 1) * tn])
        for c in range(tn // LANES):
            cs = slice(c * LANES, (c + 1) * LANES)
            os_ = slice(n * tn + c * LANES, n * tn + (c + 1) * LANES)
            o_ref[:, os_] = _head_norm_rope(acc[:, cs], hg_ref[...], cos_ref[...], sin_ref[...],
                                            scale).astype(BF16)


def _q_proj_kernel(tn, x_ref, gn_ref, w_ref, hg_ref, cos_ref, sin_ref, q_ref, h_scr):
    _rms_to_scratch(x_ref, gn_ref, h_scr)
    _rope_cols(h_scr[...], w_ref, hg_ref, cos_ref, sin_ref, q_ref, tn, ATTN_SCALE * LOG2E)


def _q_proj(x, gain, w, head_gain, cos, sin, tm, tn, name):
    M, D = x.shape
    N = w.shape[1]
    return pl.pallas_call(
        functools.partial(_q_proj_kernel, tn),
        grid=(M // tm,),
        in_specs=[pl.BlockSpec((tm, D), lambda i: (i, 0)),
                  pl.BlockSpec((1, D), lambda i: (0, 0)),
                  pl.BlockSpec((D, N), lambda i: (0, 0)),
                  pl.BlockSpec((1, LANES), lambda i: (0, 0)),
                  pl.BlockSpec((tm, LANES), lambda i: (i, 0)),
                  pl.BlockSpec((tm, LANES), lambda i: (i, 0))],
        out_specs=pl.BlockSpec((tm, N), lambda i: (i, 0)),
        out_shape=jax.ShapeDtypeStruct((M, N), BF16),
        scratch_shapes=[pltpu.VMEM((tm, D), BF16)],
        compiler_params=_cparams(("parallel",)),
        name=name,
    )(x, gain, w, head_gain, cos, sin)


def _kv_proj_kernel(tn, x_ref, gn_ref, wk_ref, wv_ref, hg_ref, cos_ref, sin_ref, k_ref, v_ref, h_scr):
    _rms_to_scratch(x_ref, gn_ref, h_scr)
    h = h_scr[...]
    _rope_cols(h, wk_ref, hg_ref, cos_ref, sin_ref, k_ref, tn, 1.0)
    for n in range(wv_ref.shape[1] // tn):
        ns = slice(n * tn, (n + 1) * tn)
        v_ref[:, ns] = _dot(h, wv_ref[:, ns]).astype(BF16)


def _kv_proj(x, gain, wk, wv, head_gain, cos, sin, tm, tn):
    M, D = x.shape
    N = wk.shape[1]
    wspec = pl.BlockSpec((D, N), lambda i: (0, 0))
    ospec = pl.BlockSpec((tm, N), lambda i: (i, 0))
    return pl.pallas_call(
        functools.partial(_kv_proj_kernel, tn),
        grid=(M // tm,),
        in_specs=[pl.BlockSpec((tm, D), lambda i: (i, 0)),
                  pl.BlockSpec((1, D), lambda i: (0, 0)),
                  wspec, wspec,
                  pl.BlockSpec((1, LANES), lambda i: (0, 0)),
                  pl.BlockSpec((tm, LANES), lambda i: (i, 0)),
                  pl.BlockSpec((tm, LANES), lambda i: (i, 0))],
        out_specs=[ospec, ospec],
        out_shape=[jax.ShapeDtypeStruct((M, N), BF16), jax.ShapeDtypeStruct((M, N), BF16)],
        scratch_shapes=[pltpu.VMEM((tm, D), BF16)],
        compiler_params=_cparams(("parallel",)),
        name="kv_proj",
    )(x, gain, wk, wv, head_gain, cos, sin)


def _softmax_sink(s_parts, sink2):
    m = sink2
    for s in s_parts:
        m = jnp.maximum(m, jnp.max(s, axis=-1, keepdims=True))
    den = jnp.exp2(sink2 - m)
    ps = []
    for s in s_parts:
        p = jnp.exp2(s - m)
        den = den + jnp.sum(p, axis=-1, keepdims=True)
        ps.append(p.astype(BF16))
    return ps, den


def _pv(ps, v_parts, den):
    o = None
    for p, v in zip(ps, v_parts):
        pv = _dot(p, v)
        o = pv if o is None else o + pv
    return o / den


def _swa_prompt_kernel(n_riders, sk_ref, q_ref, kp_ref, kc_ref, vp_ref, vc_ref, *rest):
    rider_in, o_ref, rider_out = rest[:n_riders], rest[n_riders], rest[n_riders + 1:]
    _run_riders(rider_in, rider_out)
    i = pl.program_id(1)
    W = WINDOW
    row = lax.broadcasted_iota(jnp.int32, (W, 2 * W), 0)
    col = lax.broadcasted_iota(jnp.int32, (W, 2 * W), 1)
    first_key = jnp.where(i > 0, 0, W)
    mask = (col > row) & (col <= row + W) & (col >= first_key)
    lo = lax.broadcasted_iota(jnp.int32, (W, LANES), 1) < ATTN_HEAD_DIM
    for g in range(ATTN_KV_HEADS):
        gs = slice(g * LANES, (g + 1) * LANES)
        kd = jnp.concatenate([kp_ref[:, gs], kc_ref[:, gs]], axis=0)
        vd = jnp.concatenate([vp_ref[:, gs], vc_ref[:, gs]], axis=0)
        for c in range(GQA_GROUP // 2):
            cs = slice((2 * g + c) * LANES, (2 * g + c + 1) * LANES)
            qc = q_ref[:, cs]
            halves = []
            for par in range(2):
                qm = jnp.where(lo, qc, jnp.zeros_like(qc)) if par == 0 else jnp.where(lo, jnp.zeros_like(qc), qc)
                s = jnp.where(mask, _dot_nt(qm, kd), MASK_VALUE)
                ps, den = _softmax_sink([s], sk_ref[GQA_GROUP * g + 2 * c + par] * LOG2E)
                halves.append(_pv(ps, [vd], den))
            o_ref[:, cs] = jnp.where(lo, halves[0], halves[1]).astype(BF16)


def _swa_prompt(q, kdup, vdup, sinks, batch, seq, riders, name):
    W = WINDOW
    nb = seq // W
    QW = q.shape[1]
    KW = kdup.shape[1]
    prev = lambda b, i: (b * nb + jnp.maximum(i - 1, 0), 0)
    cur = lambda b, i: (b * nb + i, 0)
    r_in, r_out, r_shape = _rider_specs(riders, nb, batch * nb)
    return pl.pallas_call(
        functools.partial(_swa_prompt_kernel, len(riders)),
        grid=(batch, nb),
        in_specs=[pl.BlockSpec(memory_space=pltpu.SMEM),
                  pl.BlockSpec((W, QW), cur),
                  pl.BlockSpec((W, KW), prev), pl.BlockSpec((W, KW), cur),
                  pl.BlockSpec((W, KW), prev), pl.BlockSpec((W, KW), cur)] + r_in,
        out_specs=[pl.BlockSpec((W, QW), cur)] + r_out,
        out_shape=[jax.ShapeDtypeStruct((batch * seq, QW), BF16)] + r_shape,
        compiler_params=_cparams(("arbitrary", "arbitrary")),
        name=name,
    )(sinks, q, kdup, kdup, vdup, vdup, *[w for w, _ in riders])


def _swa_sample_kernel(n_new, sk_ref, q_ref, kn_ref, vn_ref, kc_ref, vc_ref, o_ref):
    W = WINDOW
    nb = SAMPLE_PAD // n_new
    rows = GQA_GROUP * n_new
    qi = lax.broadcasted_iota(jnp.int32, (rows, W), 0) & (n_new - 1)
    mask_c = lax.broadcasted_iota(jnp.int32, (rows, W), 1) > qi
    qi_n = lax.broadcasted_iota(jnp.int32, (rows, SAMPLE_PAD), 0) & (n_new - 1)
    mask_n = lax.broadcasted_iota(jnp.int32, (rows, SAMPLE_PAD), 1) <= qi_n
    hrow = lax.broadcasted_iota(jnp.int32, (rows, 1), 0) // n_new
    lo = lax.broadcasted_iota(jnp.int32, (n_new, LANES), 1) < ATTN_HEAD_DIM
    zq = jnp.zeros((n_new, LANES), F32)
    zpad = jnp.zeros((SAMPLE_PAD - n_new, LANES), F32)
    q = q_ref[...].astype(F32)
    kn = kn_ref[...].astype(F32)
    vn = vn_ref[...].astype(F32)
    chains = [(bb, g) for bb in range(nb) for g in range(ATTN_KV_HEADS)]
    sinks2 = []
    for g in range(ATTN_KV_HEADS):
        sink = jnp.zeros((rows, 1), F32)
        for a in range(GQA_GROUP):
            sink = jnp.where(hrow == a, sk_ref[GQA_GROUP * g + a] * LOG2E, sink)
        sinks2.append(sink)
    scores, values = [], []
    for bb, g in chains:
        rs = slice(bb * n_new, (bb + 1) * n_new)
        gs = slice(g * LANES, (g + 1) * LANES)
        kn_g = jnp.concatenate([kn[rs, gs], zpad], axis=0).astype(BF16)
        vn_g = jnp.concatenate([vn[rs, gs], zpad], axis=0).astype(BF16)
        lhs = []
        for a in range(GQA_GROUP):
            c, par = divmod(a, 2)
            qc = q[rs, (2 * g + c) * LANES:(2 * g + c + 1) * LANES]
            lhs.append(jnp.where(lo, qc, zq) if par == 0 else jnp.where(lo, zq, qc))
        lhs = jnp.concatenate(lhs, axis=0).astype(BF16)
        s_c = jnp.where(mask_c, _dot_nt(lhs, kc_ref[bb, :, gs]), MASK_VALUE)
        s_n = jnp.where(mask_n, _dot_nt(lhs, kn_g), MASK_VALUE)
        scores.append([s_c, s_n])
        values.append([vc_ref[bb, :, gs], vn_g])
    probs = [_softmax_sink(s, sinks2[g]) for s, (bb, g) in zip(scores, chains)]
    outs = [_pv(ps, v, den) for (ps, den), v in zip(probs, values)]
    out_rows = []
    for bb in range(nb):
        chunks = []
        for g in range(ATTN_KV_HEADS):
            o = outs[bb * ATTN_KV_HEADS + g]
            for c in range(GQA_GROUP // 2):
                chunks.append(jnp.where(lo, o[(2 * c) * n_new:(2 * c + 1) * n_new],
                                        o[(2 * c + 1) * n_new:(2 * c + 2) * n_new]))
        out_rows.append(jnp.concatenate(chunks, axis=1))
    o_ref[...] = jnp.concatenate(out_rows, axis=0).astype(BF16)


def _swa_sample(q, kdup, vdup, cache_kdup, cache_vdup, sinks, row0, n_seq, n_new, name):
    W = WINDOW
    QW = q.shape[1]
    KW = kdup.shape[1]
    nb = SAMPLE_PAD // n_new
    blk0 = row0 // SAMPLE_PAD
    rmap = lambda i: (blk0 + i, 0)
    cspec = pl.BlockSpec((nb, W, KW), lambda i: (i, 0, 0))
    return pl.pallas_call(
        functools.partial(_swa_sample_kernel, n_new),
        grid=(n_seq // nb,),
        in_specs=[pl.BlockSpec(memory_space=pltpu.SMEM),
                  pl.BlockSpec((SAMPLE_PAD, QW), rmap),
                  pl.BlockSpec((SAMPLE_PAD, KW), rmap), pl.BlockSpec((SAMPLE_PAD, KW), rmap),
                  cspec, cspec],
        out_specs=pl.BlockSpec((SAMPLE_PAD, QW), lambda i: (i, 0)),
        out_shape=jax.ShapeDtypeStruct((n_seq * n_new, QW), BF16),
        compiler_params=_cparams(("parallel",)),
        name=name,
    )(sinks, q, kdup, vdup, cache_kdup, cache_vdup)


def _dup_heads(w):
    lead = w.shape[:-1]
    w3 = w.reshape(lead + (ATTN_KV_HEADS, ATTN_HEAD_DIM))
    return jnp.concatenate([w3, w3], axis=-1).reshape(lead + (ATTN_KV_HEADS * 2 * ATTN_HEAD_DIM,))


def _undup_heads(a, lead):
    return a.reshape(lead + (ATTN_KV_HEADS, 2 * ATTN_HEAD_DIM))[..., :ATTN_HEAD_DIM].astype(F32)


def _rope_tables(pos):
    half = ATTN_HEAD_DIM // 2
    inv = ROPE_THETA ** (-jnp.arange(half, dtype=F32) / half)
    ang = pos.astype(F32)[:, None] * inv[None, :]
    cos = jnp.cos(ang)
    sin = jnp.sin(ang)
    reps = LANES // ATTN_HEAD_DIM
    cos_t = jnp.tile(jnp.concatenate([cos, cos], axis=-1), (1, reps))
    sin_t = jnp.tile(jnp.concatenate([-sin, sin], axis=-1), (1, reps))
    return cos_t, sin_t


def kernel(x_prompt, x_sample, state_hgrn, cache_k_win, cache_v_win, hgrn_norm, hgrn_wq, hgrn_wf, hgrn_wi, hgrn_wg, hgrn_lb_logits, hgrn_onorm, hgrn_wo, kv_norm, w_k, w_v, k_norm, attn_norm, attn_wq, q_norm, sinks, attn_wo, ffn_norm, w_gate, w_up, w_down):
    B, T, D = x_prompt.shape
    SB, S, _ = x_sample.shape
    MP = B * T
    MS = SB * S
    n_a = hgrn_wq.shape[0]
    n_b = attn_wq.shape[0]
    assert SAMPLE_PAD % S == 0 and SB % (SAMPLE_PAD // S) == 0 and MP % SAMPLE_PAD == 0
    assert T % GLA_CHUNK == 0 and T % WINDOW == 0

    x = (x_prompt.reshape(MP, D).astype(F32), x_sample.reshape(MS, D).astype(F32))
    M = MP + MS
    tm = 768 if M % 768 == 0 else 256
    tn = 512
    tn_f32 = 256
    tf = 512

    bf = lambda w: w.astype(BF16)
    f32 = lambda w: w.astype(F32)
    row = lambda g: g.reshape(1, -1).astype(F32)
    pos = jnp.concatenate([jnp.tile(jnp.arange(T), B), jnp.tile(PAST_LEN + jnp.arange(S), SB)])
    cos_t, sin_t = _rope_tables(pos)
    head_gain = lambda g: jnp.tile(g.astype(F32), LANES // ATTN_HEAD_DIM).reshape(1, LANES)

    w_gate32, w_up32, w_down32 = f32(w_gate), f32(w_up), f32(w_down)
    hgrn_w32 = [f32(hgrn_wq), f32(hgrn_wf), f32(hgrn_wi), f32(hgrn_wg)]
    hgrn_wo32, attn_wq32, attn_wo32 = f32(hgrn_wo), f32(attn_wq), f32(attn_wo)
    ffn_riders = lambda l: [(w_gate32, l), (w_up32, l), (w_down32, l)]

    st_prompt, st_sample = [], []
    s0 = f32(state_hgrn)
    proj_w = hgrn_w32
    attn_wq_bf = None
    for l in range(n_a):
        q, k, lf, v, gate = _hgrn_proj(x, row(hgrn_norm[l]), *proj_w, f32(hgrn_lb_logits), l, tm,
                                       tn_f32 if proj_w[0].ndim == 3 else tn)
        gon = row(hgrn_onorm[l])
        riders = ffn_riders(l) + [(hgrn_wo32, l)]
        if l + 1 < n_a:
            riders += [(w, l + 1) for w in hgrn_w32]
        elif n_b > 0:
            riders += [(attn_wq32, 0)]
        o_p, s_p, wg_bf, wu_bf, wd_bf, wo_bf, *nxt = _gla_prompt(q, k, lf, v, gate, gon, B, T, riders,
                                                                 f"gla_prompt_{l}")
        if l + 1 < n_a:
            proj_w = nxt
        elif n_b > 0:
            attn_wq_bf = nxt[0]
        o_s, s_s = _gla_sample(q, k, lf, v, gate, gon, s0, l, MP, SB, S, f"gla_sample_{l}")
        st_prompt.append(s_p)
        st_sample.append(s_s)
        x = _out_proj(o_p, o_s, wo_bf, x, tm, f"hgrn_out_{l}")
        x = _ffn(x, row(ffn_norm[l]), wg_bf, wu_bf, wd_bf, tm, tf, f"ffn_{l}")

    kdup, vdup = _kv_proj(x, row(kv_norm), bf(_dup_heads(w_k)), bf(_dup_heads(w_v)), head_gain(k_norm),
                          cos_t, sin_t, tm, tn)
    KW = ATTN_KV_HEADS * ATTN_HEAD_DIM
    cache_kdup = bf(_dup_heads(cache_k_win.reshape(SB, WINDOW, KW)))
    cache_vdup = bf(_dup_heads(cache_v_win.reshape(SB, WINDOW, KW)))
    for j in range(n_b):
        l = n_a + j
        if attn_wq_bf is None:
            attn_wq_bf = bf(attn_wq[j])
        q = _q_proj(x, row(attn_norm[j]), attn_wq_bf, head_gain(q_norm[j]), cos_t, sin_t, tm, tn,
                    f"q_proj_{j}")
        sk = sinks[j].astype(F32)
        riders = ffn_riders(l) + [(attn_wo32, j)] + ([(attn_wq32, j + 1)] if j + 1 < n_b else [])
        a_p, wg_bf, wu_bf, wd_bf, wo_bf, *nxt = _swa_prompt(q, kdup, vdup, sk, B, T, riders, f"swa_prompt_{j}")
        attn_wq_bf = nxt[0] if nxt else None
        a_s = _swa_sample(q, kdup, vdup, cache_kdup, cache_vdup, sk, MP, SB, S, f"swa_sample_{j}")
        x = _out_proj(a_p, a_s, wo_bf, x, tm, f"attn_out_{j}")
        last = j == n_b - 1
        x = _ffn(x, row(ffn_norm[l]), wg_bf, wu_bf, wd_bf, tm, tf, f"ffn_{l}",
                 prompt_rows=MP if last else None)

    y_prompt = x[0].reshape(B, T, D)
    y_sample = x[1].reshape(SB, S, D)
    k_win_p = _undup_heads(kdup[:MP].reshape(B, T, -1)[:, T - WINDOW:], (B, WINDOW))
    v_win_p = _undup_heads(vdup[:MP].reshape(B, T, -1)[:, T - WINDOW:], (B, WINDOW))
    k_new_s = _undup_heads(kdup[MP:].reshape(SB, S, -1), (SB, S))
    v_new_s = _undup_heads(vdup[MP:].reshape(SB, S, -1), (SB, S))
    k_win_s = jnp.concatenate([cache_k_win[:, S:].astype(F32), k_new_s], axis=1)
    v_win_s = jnp.concatenate([cache_v_win[:, S:].astype(F32), v_new_s], axis=1)
    return (y_prompt, y_sample, jnp.stack(st_prompt), jnp.stack(st_sample),
            k_win_p, v_win_p, k_win_s, v_win_s)
```

```python
import functools
import math

import numpy as np
import jax
import jax.numpy as jnp
from jax import lax
from jax.experimental import pallas as pl
from jax.experimental.pallas import tpu as pltpu

HGRN_HEADS = 16
HEAD_DK = 128
ATTN_HEAD_DIM = 64
ATTN_KV_HEADS = 8
GQA_GROUP = 4
WINDOW = 128
ROPE_THETA = 10000.0
ATTN_SCALE = ATTN_HEAD_DIM ** -0.5
LOG2E = math.log2(math.e)
RMS_EPS = 1e-6
MASK_VALUE = -1e30
MIN_FORGET = 1e-30
PAST_LEN = 16384

LANES = 128
VMEM_LIMIT_BYTES = 56 * 1024 * 1024

GLA_CHUNK = 128
SAMPLE_PAD = 16
GLA_HEADS_PER_ITER = 16


def _heads_per_iter(n_heads):
    return math.gcd(n_heads, GLA_HEADS_PER_ITER)

F32 = jnp.float32
BF16 = jnp.bfloat16


def _cparams(sem):
    return pltpu.CompilerParams(dimension_semantics=sem, vmem_limit_bytes=VMEM_LIMIT_BYTES)


def _dot(a, b):
    return jnp.dot(a, b, preferred_element_type=F32)


def _dot_nt(a, b):
    return lax.dot_general(a, b, (((1,), (1,)), ((), ())), preferred_element_type=F32)


def _dot_tn(a, b):
    return lax.dot_general(a, b, (((0,), (0,)), ((), ())), preferred_element_type=F32)


def _wdot(a, w_ref):
    return _dot(a, w_ref[...].astype(BF16))


def _wspec_cols(w, layer, tn):
    k = w.shape[-2]
    if w.ndim == 3:
        return pl.BlockSpec((None, k, tn), lambda i, n: (layer, 0, n))
    return pl.BlockSpec((k, tn), lambda i, n: (0, n))


def _sigmoid(x):
    return 1.0 / (1.0 + jnp.exp(-x))


def _rider_specs(riders, steps_per_batch, n_steps):
    in_specs, out_specs, out_shapes = [], [], []
    for w, layer in riders:
        _, R, N = w.shape
        rb = next(r for r in range(16, R + 1, 16) if R % r == 0 and R // r <= n_steps)
        last = R // rb - 1
        blk = lambda b, c, last=last: jnp.minimum(b * steps_per_batch + c, last)
        in_specs.append(pl.BlockSpec((None, rb, N), lambda b, c, layer=layer, blk=blk: (layer, blk(b, c), 0)))
        out_specs.append(pl.BlockSpec((rb, N), lambda b, c, blk=blk: (blk(b, c), 0)))
        out_shapes.append(jax.ShapeDtypeStruct((R, N), BF16))
    return in_specs, out_specs, out_shapes


def _run_riders(in_refs, out_refs):
    for i_ref, o_ref in zip(in_refs, out_refs):
        o_ref[...] = i_ref[...].astype(BF16)


def _rms_store(x, g_ref, h_scr):
    ms = jnp.mean(x * x, axis=-1, keepdims=True)
    h_scr[...] = (x * lax.rsqrt(ms + RMS_EPS) * g_ref[...]).astype(BF16)


def _rms_to_scratch(x_ref, g_ref, h_scr):
    _rms_store(x_ref[...], g_ref, h_scr)


def _row_split(prompt_rows, sample_rows, tm):
    split = prompt_rows % tm
    assert (prompt_rows + sample_rows) % tm == 0 and tm - split == sample_rows
    return split


def _on_row_tile(split, p_ref, s_ref, fn):
    i = pl.program_id(0)
    last = pl.num_programs(0) - 1

    @pl.when(i < last)
    def _():
        fn(p_ref[...])

    @pl.when(i == last)
    def _():
        fn(jnp.concatenate([p_ref[0:split, :], s_ref[...]], axis=0))


def _hgrn_proj_kernel(layer, split, x_ref, *rest):
    xs_ref, rest = (rest[0], rest[1:]) if split is not None else (None, rest)
    gn_ref, wq_ref, wf_ref, wi_ref, wg_ref, lbl_ref, q_ref, k_ref, lf_ref, v_ref, gt_ref, h_scr = rest

    @pl.when(pl.program_id(1) == 0)
    def _():
        if split is None:
            _rms_to_scratch(x_ref, gn_ref, h_scr)
        else:
            _on_row_tile(split, x_ref, xs_ref, lambda x: _rms_store(x, gn_ref, h_scr))

    h = h_scr[...]
    aq = _wdot(h, wq_ref)
    q_ref[...] = (aq * _sigmoid(aq) * (HEAD_DK ** -0.5)).astype(BF16)

    lg = lbl_ref[...]
    mx = jnp.max(lg, axis=0, keepdims=True)
    e = jnp.exp(lg - mx)
    sm = e / jnp.sum(e, axis=0, keepdims=True)
    cs = sm[0:1]
    for r in range(1, layer + 1):
        cs = cs + sm[r:r + 1]
    lb = cs - sm[0:1]

    z = _wdot(h, wf_ref)
    ez = jnp.exp(-jnp.abs(z))
    r = 1.0 / (1.0 + ez)
    pos = z >= 0.0
    sig_p = jnp.where(pos, r, ez * r)
    sig_n = jnp.where(pos, ez * r, r)
    f = lb + (1.0 - lb) * sig_p
    lf_ref[...] = jnp.log(jnp.maximum(f, MIN_FORGET))
    k_ref[...] = ((1.0 - lb) * sig_n).astype(BF16)

    ag = _wdot(h, wg_ref)
    gt_ref[...] = (ag * _sigmoid(ag)).astype(BF16)
    v_ref[...] = _wdot(h, wi_ref).astype(BF16)


def _hgrn_proj(x, gain, wq, wf, wi, wg, lb_logits, layer, tm, tn):
    N = wq.shape[-1]
    L = lb_logits.shape[0]
    if isinstance(x, tuple):
        xp, xs = x
        D = xp.shape[1]
        M = xp.shape[0] + xs.shape[0]
        split = _row_split(xp.shape[0], xs.shape[0], tm)
        x_args = [xp, xs]
        x_specs = [pl.BlockSpec((tm, D), lambda i, n: (i, 0)), pl.BlockSpec(xs.shape, lambda i, n: (0, 0))]
    else:
        M, D = x.shape
        split, x_args, x_specs = None, [x], [pl.BlockSpec((tm, D), lambda i, n: (i, 0))]
    grid = (M // tm, N // tn)
    wspec = _wspec_cols(wq, layer, tn)
    ospec = pl.BlockSpec((tm, tn), lambda i, n: (i, n))
    return pl.pallas_call(
        functools.partial(_hgrn_proj_kernel, layer, split),
        grid=grid,
        in_specs=x_specs + [
                  pl.BlockSpec((1, D), lambda i, n: (0, 0)),
                  wspec, wspec, wspec, wspec,
                  pl.BlockSpec((L, tn), lambda i, n: (0, n))],
        out_specs=[ospec] * 5,
        out_shape=[jax.ShapeDtypeStruct((M, N), BF16),
                   jax.ShapeDtypeStruct((M, N), BF16),
                   jax.ShapeDtypeStruct((M, N), F32),
                   jax.ShapeDtypeStruct((M, N), BF16),
                   jax.ShapeDtypeStruct((M, N), BF16)],
        scratch_shapes=[pltpu.VMEM((tm, D), BF16)],
        compiler_params=_cparams(("parallel", "arbitrary")),
        name=f"hgrn_proj_{layer}",
    )(*x_args, gain, wq, wf, wi, wg, lb_logits)


def _gla_consts(C):
    nlev = int(math.log2(C))
    tri = np.tril(np.ones((C, C), np.float32))
    ii, jj = np.meshgrid(np.arange(C), np.arange(C), indexing="ij")
    x = ii ^ jj
    lev = np.floor(np.log2(np.maximum(x, 1))).astype(np.int32)
    lmap = np.where(ii == jj, -1, np.where(ii > jj, lev, -2)).astype(np.int32)
    rows = np.arange(C)
    sgn = np.stack([np.where((rows >> l) & 1 == 1, LOG2E if l == 0 else 1.0, 0.0 if l == 0 else -1.0)
                    for l in range(nlev)])
    sgn = np.broadcast_to(sgn[:, :, None], (nlev, C, LANES)).astype(np.float32)
    return jnp.asarray(tri, BF16), jnp.asarray(lmap, jnp.int32), jnp.asarray(sgn), nlev


SUBLANES = 8


def _level_exponent(l, g, g2, gs_ref, sgn):
    C = g.shape[0]
    h = 1 << l
    if l == 0:
        return g * sgn
    sub = lax.broadcasted_iota(jnp.int32, (SUBLANES, LANES), 0)
    bcast = lambda r: jnp.broadcast_to(gs_ref[pl.ds(r, 1), :], (SUBLANES, LANES))
    pieces = []
    for base in range(0, C, SUBLANES):
        if 2 * h >= SUBLANES:
            pieces.append(bcast((base // (2 * h)) * 2 * h + h - 1))
        else:
            p = bcast(base + h - 1)
            for blk in range(1, SUBLANES // (2 * h)):
                p = jnp.where(sub >= blk * 2 * h, bcast(base + blk * 2 * h + h - 1), p)
            pieces.append(p)
    gr = jnp.concatenate(pieces, axis=0)
    return (g2 - gr) * sgn


BF16_ROWS = 16


def _gla_level(l, q, k, qf, kf, g, g2, gs_ref, sgn_ref, lmap_b, acc):
    C = g.shape[0]
    h = 1 << l
    rb = BF16_ROWS
    if h < rb:
        e = jnp.exp2(_level_exponent(l, g, g2, gs_ref, sgn_ref[l]))
        al = _dot_nt((qf * e).astype(BF16), (kf * e).astype(BF16))
        return [jnp.where(lmap_b[i] == l, al[i * rb:(i + 1) * rb], a) for i, a in enumerate(acc)]
    bcast = lambda r: jnp.broadcast_to(gs_ref[pl.ds(r, 1), :], (h, LANES))
    lhs, rhs, upper_blocks = [], [], []
    for b in range(C // h):
        rows = slice(b * h, (b + 1) * h)
        if b % 2 == 1:
            e = jnp.exp2(g2[rows] - bcast(b * h - 1))
            lhs.append((qf[rows] * e).astype(BF16))
            rhs.append(k[rows])
            upper_blocks += list(range(b * h // rb, (b + 1) * h // rb))
        else:
            e = jnp.exp2(bcast((b + 1) * h - 1) - g2[rows])
            rhs.append((kf[rows] * e).astype(BF16))
    al = _dot_nt(jnp.concatenate(lhs, axis=0), jnp.concatenate(rhs, axis=0))
    acc = list(acc)
    for j, i in enumerate(upper_blocks):
        acc[i] = jnp.where(lmap_b[i] == l, al[j * rb:(j + 1) * rb], acc[i])
    return acc


def _gla_core(chains, tri, lmap, sgn_ref, nlev, gs_scr):
    C = chains[0][0].shape[0]
    rb = BF16_ROWS
    lmap_b = [lmap[i * rb:(i + 1) * rb] for i in range(C // rb)]
    g2s = []
    for u, (q, k, g, v, st) in enumerate(chains):
        ghi = g.astype(BF16)
        glo = (g - ghi.astype(F32)).astype(BF16)
        c2 = _dot(tri, jnp.concatenate([ghi, glo], axis=1))
        g2 = (c2[:, :LANES] + c2[:, LANES:]) * LOG2E
        gs_scr[u] = g2
        g2s.append(g2)
    accs = []
    for (q, k, g, v, st) in chains:
        d = _dot_nt(q, k)
        accs.append([jnp.where(lmap_b[i] == -1, d[i * rb:(i + 1) * rb], 0.0) for i in range(C // rb)])
    qk32 = [(q.astype(F32), k.astype(F32)) for (q, k, g, v, st) in chains]
    for l in range(nlev):
        for u, (q, k, g, v, st) in enumerate(chains):
            qf, kf = qk32[u]
            accs[u] = _gla_level(l, q, k, qf, kf, g, g2s[u], gs_scr.at[u], sgn_ref, lmap_b, accs[u])
    outs = []
    for u, (q, k, g, v, st) in enumerate(chains):
        g2 = g2s[u]
        qf, kf = qk32[u]
        o = _dot(jnp.concatenate(accs[u], axis=0).astype(BF16), v)
        o = o + _dot_nt((qf * jnp.exp2(g2)).astype(BF16), st.astype(BF16))
        glast = g2[C - 1:C, :]
        kdec = (kf * jnp.exp2(glast - g2)).astype(BF16)
        st_new = st * jnp.exp2(glast) + _dot_tn(v, kdec)
        outs.append((o, st_new))
    return outs


def _gla_out(o, gon, gate):
    ms = jnp.mean(o * o, axis=-1, keepdims=True)
    return o * lax.rsqrt(ms + RMS_EPS) * gon * gate


def _gla_prompt_kernel(nlev, n_heads, n_riders, q_ref, k_ref, g_ref, v_ref, gt_ref, tri_ref, lmap_ref,
                       sgn_ref, gon_ref, *rest):
    rider_in, rest = rest[:n_riders], rest[n_riders:]
    o_ref, sfin_ref = rest[0], rest[1]
    rider_out = rest[2:2 + n_riders]
    st_scr, gs_scr = rest[2 + n_riders:]
    _run_riders(rider_in, rider_out)
    c = pl.program_id(1)
    hpi = _heads_per_iter(n_heads)
    n_iter = n_heads // hpi

    @pl.when(c == 0)
    def _():
        st_scr[...] = jnp.zeros_like(st_scr)

    def body(j, carry):
        heads = [j * hpi + u for u in range(hpi)]
        lanes = [pl.ds(pl.multiple_of(h * LANES, LANES), LANES) for h in heads]
        chains = [(q_ref[:, hs], k_ref[:, hs], g_ref[:, hs], v_ref[:, hs], st_scr[h])
                  for h, hs in zip(heads, lanes)]
        res = _gla_core(chains, tri_ref[...], lmap_ref[...], sgn_ref, nlev, gs_scr)
        for h, hs, (o, st_new) in zip(heads, lanes, res):
            st_scr[h] = st_new
            o_ref[:, hs] = _gla_out(o, gon_ref[...], gt_ref[:, hs].astype(F32)).astype(BF16)
        return carry

    lax.fori_loop(0, n_iter, body, 0)

    @pl.when(c == pl.num_programs(1) - 1)
    def _():
        def wb(h, carry):
            sfin_ref[0, h] = st_scr[h].T
            return carry
        lax.fori_loop(0, n_heads, wb, 0)


def _gla_prompt(q, k, g, v, gate, gon, batch, seq, riders, name):
    C = GLA_CHUNK
    W = q.shape[1]
    H = W // LANES
    nc = seq // C
    tri, lmap, sgn, nlev = _gla_consts(C)
    rspec = pl.BlockSpec((C, W), lambda b, c: (b * nc + c, 0))
    r_in, r_out, r_shape = _rider_specs(riders, nc, batch * nc)
    return pl.pallas_call(
        functools.partial(_gla_prompt_kernel, nlev, H, len(riders)),
        grid=(batch, nc),
        in_specs=[rspec, rspec, rspec, rspec, rspec,
                  pl.BlockSpec(tri.shape, lambda b, c: (0, 0)),
                  pl.BlockSpec(lmap.shape, lambda b, c: (0, 0)),
                  pl.BlockSpec(sgn.shape, lambda b, c: (0, 0, 0)),
                  pl.BlockSpec((1, LANES), lambda b, c: (0, 0))] + r_in,
        out_specs=[rspec, pl.BlockSpec((1, H, LANES, LANES), lambda b, c: (b, 0, 0, 0))] + r_out,
        out_shape=[jax.ShapeDtypeStruct((batch * seq, W), BF16),
                   jax.ShapeDtypeStruct((batch, H, LANES, LANES), F32)] + r_shape,
        scratch_shapes=[pltpu.VMEM((H, LANES, LANES), F32),
                        pltpu.VMEM((_heads_per_iter(H), C, LANES), F32)],
        compiler_params=_cparams(("arbitrary", "arbitrary")),
        name=name,
    )(q, k, g, v, gate, tri, lmap, sgn, gon, *[w for w, _ in riders])


def _gla_sample_kernel(nlev, n_heads, n_new, layer, creates_states, q_ref, k_ref, g_ref, v_ref, gt_ref,
                       tri_ref, lmap_ref, sgn_ref, gon_ref, s0_ref, *rest):
    if creates_states:
        o_ref, sall_ref, gs_scr = rest
        for other in range(sall_ref.shape[0]):
            if other != layer:
                sall_ref[other] = jnp.zeros(sall_ref.shape[1:], F32)
        sfin_ref = sall_ref.at[layer]
    else:
        _, o_ref, sfin_ref, gs_scr = rest
    nb = SAMPLE_PAD // n_new
    zpad = jnp.zeros((SAMPLE_PAD - n_new, LANES), F32)
    hpi = _heads_per_iter(n_heads)

    def body(j, carry):
        heads = [j * hpi + u for u in range(hpi)]
        lanes = [pl.ds(pl.multiple_of(h * LANES, LANES), LANES) for h in heads]
        chains = []
        for h, hs in zip(heads, lanes):
            q = q_ref[:, hs].astype(F32)
            k = k_ref[:, hs].astype(F32)
            g = g_ref[:, hs]
            v = v_ref[:, hs].astype(F32)
            for bb in range(nb):
                rs = slice(bb * n_new, (bb + 1) * n_new)
                pad = lambda t: jnp.concatenate([t[rs], zpad], axis=0)
                chains.append((pad(q).astype(BF16), pad(k).astype(BF16), pad(g), pad(v).astype(BF16),
                               s0_ref[bb, h].T))
        res = _gla_core(chains, tri_ref[...], lmap_ref[...], sgn_ref, nlev, gs_scr)
        for u, (h, hs) in enumerate(zip(heads, lanes)):
            gate = gt_ref[:, hs].astype(F32)
            outs = []
            for bb in range(nb):
                o, st_new = res[u * nb + bb]
                sfin_ref[bb, h] = st_new.T
                outs.append(_gla_out(o[0:n_new], gon_ref[...], gate[bb * n_new:(bb + 1) * n_new]))
            o_ref[:, hs] = jnp.concatenate(outs, axis=0).astype(BF16)
        return carry

    lax.fori_loop(0, n_heads // hpi, body, 0)


def _gla_sample(q, k, g, v, gate, gon, s0, layer, row0, n_seq, n_new, name, states=None):
    W = q.shape[1]
    H = W // LANES
    L = s0.shape[0]
    nb = SAMPLE_PAD // n_new
    tri, lmap, sgn, nlev = _gla_consts(SAMPLE_PAD)
    blk0 = row0 // SAMPLE_PAD
    rspec = pl.BlockSpec((SAMPLE_PAD, W), lambda i: (blk0 + i, 0))
    s0spec = pl.BlockSpec((None, nb, H, LANES, LANES), lambda i: (layer, i, 0, 0, 0))
    in_specs = [rspec, rspec, rspec, rspec, rspec,
                pl.BlockSpec(tri.shape, lambda i: (0, 0)),
                pl.BlockSpec(lmap.shape, lambda i: (0, 0)),
                pl.BlockSpec(sgn.shape, lambda i: (0, 0, 0)),
                pl.BlockSpec((1, LANES), lambda i: (0, 0)),
                s0spec]
    args = [q, k, g, v, gate, tri, lmap, sgn, gon, s0]
    if states is None:
        sspec = pl.BlockSpec((L, nb, H, LANES, LANES), lambda i: (0, i, 0, 0, 0))
        aliases = {}
    else:
        sspec = s0spec
        in_specs.append(pl.BlockSpec(memory_space=pl.ANY))
        args.append(states)
        aliases = {len(args) - 1: 1}
    return pl.pallas_call(
        functools.partial(_gla_sample_kernel, nlev, H, n_new, layer, states is None),
        grid=(n_seq // nb,),
        in_specs=in_specs,
        out_specs=[pl.BlockSpec((SAMPLE_PAD, W), lambda i: (i, 0)), sspec],
        out_shape=[jax.ShapeDtypeStruct((n_seq * n_new, W), BF16),
                   jax.ShapeDtypeStruct((L, n_seq, H, LANES, LANES), F32)],
        input_output_aliases=aliases,
        scratch_shapes=[pltpu.VMEM((_heads_per_iter(H) * nb, SAMPLE_PAD, LANES), F32)],
        compiler_params=_cparams(("parallel",)),
        name=name,
    )(*args)


def _out_proj_kernel(split, x_is_pair, ap_ref, as_ref, w_ref, x_ref, *rest):
    xs_ref, o_ref = (rest[0], rest[1]) if x_is_pair else (None, rest[0])
    i = pl.program_id(0)
    last = pl.num_programs(0) - 1
    tail = lambda p_ref, s_ref: jnp.concatenate([p_ref[0:split, :], s_ref[...]], axis=0)

    @pl.when(i < last)
    def _():
        o_ref[...] = x_ref[...] + _wdot(ap_ref[...], w_ref)

    @pl.when(i == last)
    def _():
        x = tail(x_ref, xs_ref) if x_is_pair else x_ref[...]
        o_ref[...] = x + _wdot(tail(ap_ref, as_ref), w_ref)


def _out_proj(a_prompt, a_sample, w, x, tm, name):
    MP, K = a_prompt.shape
    MS = a_sample.shape[0]
    N = w.shape[1]
    M = MP + MS
    split = _row_split(MP, MS, tm)
    x_is_pair = isinstance(x, tuple)
    if x_is_pair:
        x_args = list(x)
        x_specs = [pl.BlockSpec((tm, N), lambda i: (i, 0)), pl.BlockSpec((MS, N), lambda i: (0, 0))]
    else:
        x_args, x_specs = [x], [pl.BlockSpec((tm, N), lambda i: (i, 0))]
    return pl.pallas_call(
        functools.partial(_out_proj_kernel, split, x_is_pair),
        grid=(M // tm,),
        in_specs=[pl.BlockSpec((tm, K), lambda i: (i, 0)),
                  pl.BlockSpec((MS, K), lambda i: (0, 0)),
                  pl.BlockSpec((K, N), lambda i: (0, 0))] + x_specs,
        out_specs=pl.BlockSpec((tm, N), lambda i: (i, 0)),
        out_shape=jax.ShapeDtypeStruct((M, N), F32),
        compiler_params=_cparams(("parallel",)),
        name=name,
    )(a_prompt, a_sample, w, *x_args)


def _ffn_kernel(split, x_ref, gn_ref, wg_ref, wu_ref, wd_ref, o_ref, *rest):
    ys_ref = rest[0] if split is not None else None
    h_scr = rest[-1]
    f = pl.program_id(1)

    @pl.when(f == 0)
    def _():
        _rms_to_scratch(x_ref, gn_ref, h_scr)
        o_ref[...] = x_ref[...]

    h = h_scr[...]
    g = _wdot(h, wg_ref)
    u = _wdot(h, wu_ref)
    a = (g * _sigmoid(g) * u).astype(BF16)
    o_ref[...] += _wdot(a, wd_ref)

    if split is not None:
        @pl.when((f == pl.num_programs(1) - 1) & (pl.program_id(0) == pl.num_programs(0) - 1))
        def _():
            ys_ref[...] = o_ref[split:, :]


def _ffn(x, gain, wg, wu, wd, tm, tf, name, prompt_rows=None):
    M, D = x.shape
    FF = wg.shape[1]
    o_spec = pl.BlockSpec((tm, D), lambda i, f: (i, 0))
    if prompt_rows is None:
        split, out_specs, out_shape = None, o_spec, jax.ShapeDtypeStruct((M, D), F32)
    else:
        split = prompt_rows % tm
        ms = M - prompt_rows
        assert tm - split == ms
        out_specs = [o_spec, pl.BlockSpec((ms, D), lambda i, f: (0, 0))]
        out_shape = [jax.ShapeDtypeStruct((prompt_rows, D), F32), jax.ShapeDtypeStruct((ms, D), F32)]
    return pl.pallas_call(
        functools.partial(_ffn_kernel, split),
        grid=(M // tm, FF // tf),
        in_specs=[pl.BlockSpec((tm, D), lambda i, f: (i, 0)),
                  pl.BlockSpec((1, D), lambda i, f: (0, 0)),
                  pl.BlockSpec((D, tf), lambda i, f: (0, f)),
                  pl.BlockSpec((D, tf), lambda i, f: (0, f)),
                  pl.BlockSpec((tf, D), lambda i, f: (f, 0))],
        out_specs=out_specs,
        out_shape=out_shape,
        scratch_shapes=[pltpu.VMEM((tm, D), BF16)],
        compiler_params=_cparams(("arbitrary", "arbitrary")),
        name=name,
    )(x, gain, wg, wu, wd)


def _head_norm_rope(acc, gain, cos, sin, scale):
    r = lax.broadcasted_iota(jnp.int32, (LANES, LANES), 0)
    c = lax.broadcasted_iota(jnp.int32, (LANES, LANES), 1)
    seg = jnp.where((r >> 6) == (c >> 6), 1.0, 0.0).astype(BF16)
    ss = _dot((acc * acc).astype(BF16), seg)
    y = acc * lax.rsqrt(ss * (1.0 / ATTN_HEAD_DIM) + RMS_EPS) * gain
    lane = lax.broadcasted_iota(jnp.int32, y.shape, 1)
    half = ATTN_HEAD_DIM // 2
    rot = jnp.where((lane & half) == 0, pltpu.roll(y, LANES - half, 1), pltpu.roll(y, half, 1))
    out = y * cos + rot * sin
    return out * scale if scale != 1.0 else out


def _rope_cols(h, w_ref, hg_ref, cos_ref, sin_ref, o_ref, tn, scale):
    for n in range(w_ref.shape[1] // tn):
        acc = _dot(h, w_ref[:, n * tn:(n + 1) * tn])
        for c in range(tn // LANES):
            cs = slice(c * LANES, (c + 1) * LANES)
            os_ = slice(n * tn + c * LANES, n * tn + (c + 1) * LANES)
            o_ref[:, os_] = _head_norm_rope(acc[:, cs], hg_ref[...], cos_ref[...], sin_ref[...],
                                            scale).astype(BF16)


def _q_proj_kernel(tn, x_ref, gn_ref, w_ref, hg_ref, cos_ref, sin_ref, q_ref, h_scr):
    _rms_to_scratch(x_ref, gn_ref, h_scr)
    _rope_cols(h_scr[...], w_ref, hg_ref, cos_ref, sin_ref, q_ref, tn, ATTN_SCALE * LOG2E)


def _q_proj(x, gain, w, head_gain, cos, sin, tm, tn, name):
    M, D = x.shape
    N = w.shape[1]
    return pl.pallas_call(
        functools.partial(_q_proj_kernel, tn),
        grid=(M // tm,),
        in_specs=[pl.BlockSpec((tm, D), lambda i: (i, 0)),
                  pl.BlockSpec((1, D), lambda i: (0, 0)),
                  pl.BlockSpec((D, N), lambda i: (0, 0)),
                  pl.BlockSpec((1, LANES), lambda i: (0, 0)),
                  pl.BlockSpec((tm, LANES), lambda i: (i, 0)),
                  pl.BlockSpec((tm, LANES), lambda i: (i, 0))],
        out_specs=pl.BlockSpec((tm, N), lambda i: (i, 0)),
        out_shape=jax.ShapeDtypeStruct((M, N), BF16),
        scratch_shapes=[pltpu.VMEM((tm, D), BF16)],
        compiler_params=_cparams(("parallel",)),
        name=name,
    )(x, gain, w, head_gain, cos, sin)


def _kv_proj_kernel(tn, x_ref, gn_ref, wk_ref, wv_ref, hg_ref, cos_ref, sin_ref, k_ref, v_ref, h_scr):
    _rms_to_scratch(x_ref, gn_ref, h_scr)
    h = h_scr[...]
    _rope_cols(h, wk_ref, hg_ref, cos_ref, sin_ref, k_ref, tn, 1.0)
    for n in range(wv_ref.shape[1] // tn):
        ns = slice(n * tn, (n + 1) * tn)
        v_ref[:, ns] = _dot(h, wv_ref[:, ns]).astype(BF16)


def _kv_proj(x, gain, wk, wv, head_gain, cos, sin, tm, tn):
    M, D = x.shape
    N = wk.shape[1]
    wspec = pl.BlockSpec((D, N), lambda i: (0, 0))
    ospec = pl.BlockSpec((tm, N), lambda i: (i, 0))
    return pl.pallas_call(
        functools.partial(_kv_proj_kernel, tn),
        grid=(M // tm,),
        in_specs=[pl.BlockSpec((tm, D), lambda i: (i, 0)),
                  pl.BlockSpec((1, D), lambda i: (0, 0)),
                  wspec, wspec,
                  pl.BlockSpec((1, LANES), lambda i: (0, 0)),
                  pl.BlockSpec((tm, LANES), lambda i: (i, 0)),
                  pl.BlockSpec((tm, LANES), lambda i: (i, 0))],
        out_specs=[ospec, ospec],
        out_shape=[jax.ShapeDtypeStruct((M, N), BF16), jax.ShapeDtypeStruct((M, N), BF16)],
        scratch_shapes=[pltpu.VMEM((tm, D), BF16)],
        compiler_params=_cparams(("parallel",)),
        name="kv_proj",
    )(x, gain, wk, wv, head_gain, cos, sin)


def _softmax_sink(s_parts, sink2):
    m = sink2
    for s in s_parts:
        m = jnp.maximum(m, jnp.max(s, axis=-1, keepdims=True))
    den = jnp.exp2(sink2 - m)
    ps = []
    for s in s_parts:
        p = jnp.exp2(s - m)
        den = den + jnp.sum(p, axis=-1, keepdims=True)
        ps.append(p.astype(BF16))
    return ps, den


def _pv(ps, v_parts, den):
    o = None
    for p, v in zip(ps, v_parts):
        pv = _dot(p, v)
        o = pv if o is None else o + pv
    return o / den


def _swa_prompt_kernel(n_riders, sk_ref, q_ref, kp_ref, kc_ref, vp_ref, vc_ref, *rest):
    rider_in, o_ref, rider_out = rest[:n_riders], rest[n_riders], rest[n_riders + 1:]
    _run_riders(rider_in, rider_out)
    i = pl.program_id(1)
    W = WINDOW
    row = lax.broadcasted_iota(jnp.int32, (W, 2 * W), 0)
    col = lax.broadcasted_iota(jnp.int32, (W, 2 * W), 1)
    first_key = jnp.where(i > 0, 0, W)
    mask = (col > row) & (col <= row + W) & (col >= first_key)
    lo = lax.broadcasted_iota(jnp.int32, (W, LANES), 1) < ATTN_HEAD_DIM
    for g in range(ATTN_KV_HEADS):
        gs = slice(g * LANES, (g + 1) * LANES)
        kd = jnp.concatenate([kp_ref[:, gs], kc_ref[:, gs]], axis=0)
        vd = jnp.concatenate([vp_ref[:, gs], vc_ref[:, gs]], axis=0)
        for c in range(GQA_GROUP // 2):
            cs = slice((2 * g + c) * LANES, (2 * g + c + 1) * LANES)
            qc = q_ref[:, cs]
            halves = []
            for par in range(2):
                qm = jnp.where(lo, qc, jnp.zeros_like(qc)) if par == 0 else jnp.where(lo, jnp.zeros_like(qc), qc)
                s = jnp.where(mask, _dot_nt(qm, kd), MASK_VALUE)
                ps, den = _softmax_sink([s], sk_ref[GQA_GROUP * g + 2 * c + par] * LOG2E)
                halves.append(_pv(ps, [vd], den))
            o_ref[:, cs] = jnp.where(lo, halves[0], halves[1]).astype(BF16)


def _swa_prompt(q, kdup, vdup, sinks, batch, seq, riders, name):
    W = WINDOW
    nb = seq // W
    QW = q.shape[1]
    KW = kdup.shape[1]
    prev = lambda b, i: (b * nb + jnp.maximum(i - 1, 0), 0)
    cur = lambda b, i: (b * nb + i, 0)
    r_in, r_out, r_shape = _rider_specs(riders, nb, batch * nb)
    return pl.pallas_call(
        functools.partial(_swa_prompt_kernel, len(riders)),
        grid=(batch, nb),
        in_specs=[pl.BlockSpec(memory_space=pltpu.SMEM),
                  pl.BlockSpec((W, QW), cur),
                  pl.BlockSpec((W, KW), prev), pl.BlockSpec((W, KW), cur),
                  pl.BlockSpec((W, KW), prev), pl.BlockSpec((W, KW), cur)] + r_in,
        out_specs=[pl.BlockSpec((W, QW), cur)] + r_out,
        out_shape=[jax.ShapeDtypeStruct((batch * seq, QW), BF16)] + r_shape,
        compiler_params=_cparams(("arbitrary", "arbitrary")),
        name=name,
    )(sinks, q, kdup, kdup, vdup, vdup, *[w for w, _ in riders])


def _swa_sample_kernel(n_new, sk_ref, q_ref, kn_ref, vn_ref, kc_ref, vc_ref, o_ref):
    W = WINDOW
    nb = SAMPLE_PAD // n_new
    rows = GQA_GROUP * n_new
    qi = lax.broadcasted_iota(jnp.int32, (rows, W), 0) & (n_new - 1)
    mask_c = lax.broadcasted_iota(jnp.int32, (rows, W), 1) > qi
    qi_n = lax.broadcasted_iota(jnp.int32, (rows, SAMPLE_PAD), 0) & (n_new - 1)
    mask_n = lax.broadcasted_iota(jnp.int32, (rows, SAMPLE_PAD), 1) <= qi_n
    hrow = lax.broadcasted_iota(jnp.int32, (rows, 1), 0) // n_new
    lo = lax.broadcasted_iota(jnp.int32, (n_new, LANES), 1) < ATTN_HEAD_DIM
    zq = jnp.zeros((n_new, LANES), F32)
    zpad = jnp.zeros((SAMPLE_PAD - n_new, LANES), F32)
    q = q_ref[...].astype(F32)
    kn = kn_ref[...].astype(F32)
    vn = vn_ref[...].astype(F32)
    chains = [(bb, g) for bb in range(nb) for g in range(ATTN_KV_HEADS)]
    sinks2 = []
    for g in range(ATTN_KV_HEADS):
        sink = jnp.zeros((rows, 1), F32)
        for a in range(GQA_GROUP):
            sink = jnp.where(hrow == a, sk_ref[GQA_GROUP * g + a] * LOG2E, sink)
        sinks2.append(sink)
    scores, values = [], []
    for bb, g in chains:
        rs = slice(bb * n_new, (bb + 1) * n_new)
        gs = slice(g * LANES, (g + 1) * LANES)
        kn_g = jnp.concatenate([kn[rs, gs], zpad], axis=0).astype(BF16)
        vn_g = jnp.concatenate([vn[rs, gs], zpad], axis=0).astype(BF16)
        lhs = []
        for a in range(GQA_GROUP):
            c, par = divmod(a, 2)
            qc = q[rs, (2 * g + c) * LANES:(2 * g + c + 1) * LANES]
            lhs.append(jnp.where(lo, qc, zq) if par == 0 else jnp.where(lo, zq, qc))
        lhs = jnp.concatenate(lhs, axis=0).astype(BF16)
        s_c = jnp.where(mask_c, _dot_nt(lhs, kc_ref[bb, :, gs]), MASK_VALUE)
        s_n = jnp.where(mask_n, _dot_nt(lhs, kn_g), MASK_VALUE)
        scores.append([s_c, s_n])
        values.append([vc_ref[bb, :, gs], vn_g])
    probs = [_softmax_sink(s, sinks2[g]) for s, (bb, g) in zip(scores, chains)]
    outs = [_pv(ps, v, den) for (ps, den), v in zip(probs, values)]
    out_rows = []
    for bb in range(nb):
        chunks = []
        for g in range(ATTN_KV_HEADS):
            o = outs[bb * ATTN_KV_HEADS + g]
            for c in range(GQA_GROUP // 2):
                chunks.append(jnp.where(lo, o[(2 * c) * n_new:(2 * c + 1) * n_new],
                                        o[(2 * c + 1) * n_new:(2 * c + 2) * n_new]))
        out_rows.append(jnp.concatenate(chunks, axis=1))
    o_ref[...] = jnp.concatenate(out_rows, axis=0).astype(BF16)


def _swa_sample(q, kdup, vdup, cache_kdup, cache_vdup, sinks, row0, n_seq, n_new, name):
    W = WINDOW
    QW = q.shape[1]
    KW = kdup.shape[1]
    nb = SAMPLE_PAD // n_new
    blk0 = row0 // SAMPLE_PAD
    rmap = lambda i: (blk0 + i, 0)
    cspec = pl.BlockSpec((nb, W, KW), lambda i: (i, 0, 0))
    return pl.pallas_call(
        functools.partial(_swa_sample_kernel, n_new),
        grid=(n_seq // nb,),
        in_specs=[pl.BlockSpec(memory_space=pltpu.SMEM),
                  pl.BlockSpec((SAMPLE_PAD, QW), rmap),
                  pl.BlockSpec((SAMPLE_PAD, KW), rmap), pl.BlockSpec((SAMPLE_PAD, KW), rmap),
                  cspec, cspec],
        out_specs=pl.BlockSpec((SAMPLE_PAD, QW), lambda i: (i, 0)),
        out_shape=jax.ShapeDtypeStruct((n_seq * n_new, QW), BF16),
        compiler_params=_cparams(("parallel",)),
        name=name,
    )(sinks, q, kdup, vdup, cache_kdup, cache_vdup)


def _dup_heads(w):
    lead = w.shape[:-1]
    w3 = w.reshape(lead + (ATTN_KV_HEADS, ATTN_HEAD_DIM))
    return jnp.concatenate([w3, w3], axis=-1).reshape(lead + (ATTN_KV_HEADS * 2 * ATTN_HEAD_DIM,))


def _undup_heads(a, lead):
    return a.reshape(lead + (ATTN_KV_HEADS, 2 * ATTN_HEAD_DIM))[..., :ATTN_HEAD_DIM].astype(F32)


def _rope_tables(pos):
    half = ATTN_HEAD_DIM // 2
    inv = ROPE_THETA ** (-jnp.arange(half, dtype=F32) / half)
    ang = pos.astype(F32)[:, None] * inv[None, :]
    cos = jnp.cos(ang)
    sin = jnp.sin(ang)
    reps = LANES // ATTN_HEAD_DIM
    cos_t = jnp.tile(jnp.concatenate([cos, cos], axis=-1), (1, reps))
    sin_t = jnp.tile(jnp.concatenate([-sin, sin], axis=-1), (1, reps))
    return cos_t, sin_t


def kernel(x_prompt, x_sample, state_hgrn, cache_k_win, cache_v_win, hgrn_norm, hgrn_wq, hgrn_wf, hgrn_wi, hgrn_wg, hgrn_lb_logits, hgrn_onorm, hgrn_wo, kv_norm, w_k, w_v, k_norm, attn_norm, attn_wq, q_norm, sinks, attn_wo, ffn_norm, w_gate, w_up, w_down):
    B, T, D = x_prompt.shape
    SB, S, _ = x_sample.shape
    MP = B * T
    MS = SB * S
    n_a = hgrn_wq.shape[0]
    n_b = attn_wq.shape[0]
    assert SAMPLE_PAD % S == 0 and SB % (SAMPLE_PAD // S) == 0 and MP % SAMPLE_PAD == 0
    assert T % GLA_CHUNK == 0 and T % WINDOW == 0

    x = (x_prompt.reshape(MP, D).astype(F32), x_sample.reshape(MS, D).astype(F32))
    M = MP + MS
    tm = 768 if M % 768 == 0 else 256
    tn = 512
    tn_f32 = 256
    tf = 512

    bf = lambda w: w.astype(BF16)
    f32 = lambda w: w.astype(F32)
    row = lambda g: g.reshape(1, -1).astype(F32)
    pos = jnp.concatenate([jnp.tile(jnp.arange(T), B), jnp.tile(PAST_LEN + jnp.arange(S), SB)])
    cos_t, sin_t = _rope_tables(pos)
    head_gain = lambda g: jnp.tile(g.astype(F32), LANES // ATTN_HEAD_DIM).reshape(1, LANES)

    w_gate32, w_up32, w_down32 = f32(w_gate), f32(w_up), f32(w_down)
    hgrn_w32 = [f32(hgrn_wq), f32(hgrn_wf), f32(hgrn_wi), f32(hgrn_wg)]
    hgrn_wo32, attn_wq32, attn_wo32 = f32(hgrn_wo), f32(attn_wq), f32(attn_wo)
    ffn_riders = lambda l: [(w_gate32, l), (w_up32, l), (w_down32, l)]

    st_prompt, st_sample = [], None
    s0 = f32(state_hgrn)
    proj_w = hgrn_w32
    attn_wq_bf = None
    for l in range(n_a):
        q, k, lf, v, gate = _hgrn_proj(x, row(hgrn_norm[l]), *proj_w, f32(hgrn_lb_logits), l, tm,
                                       tn_f32 if proj_w[0].ndim == 3 else tn)
        gon = row(hgrn_onorm[l])
        riders = ffn_riders(l) + [(hgrn_wo32, l)]
        if l + 1 < n_a:
            riders += [(w, l + 1) for w in hgrn_w32]
        elif n_b > 0:
            riders += [(attn_wq32, 0)]
        o_p, s_p, wg_bf, wu_bf, wd_bf, wo_bf, *nxt = _gla_prompt(q, k, lf, v, gate, gon, B, T, riders,
                                                                 f"gla_prompt_{l}")
        if l + 1 < n_a:
            proj_w = nxt
        elif n_b > 0:
            attn_wq_bf = nxt[0]
        o_s, st_sample = _gla_sample(q, k, lf, v, gate, gon, s0, l, MP, SB, S, f"gla_sample_{l}",
                                     states=st_sample)
        st_prompt.append(s_p)
        x = _out_proj(o_p, o_s, wo_bf, x, tm, f"hgrn_out_{l}")
        x = _ffn(x, row(ffn_norm[l]), wg_bf, wu_bf, wd_bf, tm, tf, f"ffn_{l}")

    kdup, vdup = _kv_proj(x, row(kv_norm), bf(_dup_heads(w_k)), bf(_dup_heads(w_v)), head_gain(k_norm),
                          cos_t, sin_t, tm, tn)
    KW = ATTN_KV_HEADS * ATTN_HEAD_DIM
    cache_kdup = bf(_dup_heads(cache_k_win.reshape(SB, WINDOW, KW)))
    cache_vdup = bf(_dup_heads(cache_v_win.reshape(SB, WINDOW, KW)))
    for j in range(n_b):
        l = n_a + j
        if attn_wq_bf is None:
            attn_wq_bf = bf(attn_wq[j])
        q = _q_proj(x, row(attn_norm[j]), attn_wq_bf, head_gain(q_norm[j]), cos_t, sin_t, tm, tn,
                    f"q_proj_{j}")
        sk = sinks[j].astype(F32)
        riders = ffn_riders(l) + [(attn_wo32, j)] + ([(attn_wq32, j + 1)] if j + 1 < n_b else [])
        a_p, wg_bf, wu_bf, wd_bf, wo_bf, *nxt = _swa_prompt(q, kdup, vdup, sk, B, T, riders, f"swa_prompt_{j}")
        attn_wq_bf = nxt[0] if nxt else None
        a_s = _swa_sample(q, kdup, vdup, cache_kdup, cache_vdup, sk, MP, SB, S, f"swa_sample_{j}")
        x = _out_proj(a_p, a_s, wo_bf, x, tm, f"attn_out_{j}")
        last = j == n_b - 1
        x = _ffn(x, row(ffn_norm[l]), wg_bf, wu_bf, wd_bf, tm, tf, f"ffn_{l}",
                 prompt_rows=MP if last else None)

    y_prompt = x[0].reshape(B, T, D)
    y_sample = x[1].reshape(SB, S, D)
    k_win_p = _undup_heads(kdup[:MP].reshape(B, T, -1)[:, T - WINDOW:], (B, WINDOW))
    v_win_p = _undup_heads(vdup[:MP].reshape(B, T, -1)[:, T - WINDOW:], (B, WINDOW))
    k_new_s = _undup_heads(kdup[MP:].reshape(SB, S, -1), (SB, S))
    v_new_s = _undup_heads(vdup[MP:].reshape(SB, S, -1), (SB, S))
    k_win_s = jnp.concatenate([cache_k_win[:, S:].astype(F32), k_new_s], axis=1)
    v_win_s = jnp.concatenate([cache_v_win[:, S:].astype(F32), v_new_s], axis=1)
    return (y_prompt, y_sample, jnp.stack(st_prompt), st_sample,
            k_win_p, v_win_p, k_win_s, v_win_s)
```

```python
import functools
import math

import numpy as np
import jax
import jax.numpy as jnp
from jax import lax
from jax.experimental import pallas as pl
from jax.experimental.pallas import tpu as pltpu

HGRN_HEADS = 16
HEAD_DK = 128
ATTN_HEAD_DIM = 64
ATTN_KV_HEADS = 8
GQA_GROUP = 4
WINDOW = 128
ROPE_THETA = 10000.0
ATTN_SCALE = ATTN_HEAD_DIM ** -0.5
LOG2E = math.log2(math.e)
RMS_EPS = 1e-6
MASK_VALUE = -1e30
MIN_FORGET = 1e-30
PAST_LEN = 16384

LANES = 128
VMEM_LIMIT_BYTES = 56 * 1024 * 1024

GLA_CHUNK = 128
SAMPLE_PAD = 16
GLA_HEADS_PER_ITER = 16


def _heads_per_iter(n_heads):
    return math.gcd(n_heads, GLA_HEADS_PER_ITER)

F32 = jnp.float32
BF16 = jnp.bfloat16


def _cparams(sem):
    return pltpu.CompilerParams(dimension_semantics=sem, vmem_limit_bytes=VMEM_LIMIT_BYTES)


def _dot(a, b):
    return jnp.dot(a, b, preferred_element_type=F32)


def _dot_nt(a, b):
    return lax.dot_general(a, b, (((1,), (1,)), ((), ())), preferred_element_type=F32)


def _dot_tn(a, b):
    return lax.dot_general(a, b, (((0,), (0,)), ((), ())), preferred_element_type=F32)


def _wdot(a, w_ref):
    return _dot(a, w_ref[...].astype(BF16))


def _wspec_cols(w, layer, tn):
    k = w.shape[-2]
    if w.ndim == 3:
        return pl.BlockSpec((None, k, tn), lambda i, n: (layer, 0, n))
    return pl.BlockSpec((k, tn), lambda i, n: (0, n))


def _sigmoid(x):
    return 0.5 * jnp.tanh(0.5 * x) + 0.5


def _rider_specs(riders, steps_per_batch, n_steps):
    in_specs, out_specs, out_shapes = [], [], []
    for w, layer in riders:
        _, R, N = w.shape
        rb = next(r for r in range(16, R + 1, 16) if R % r == 0 and R // r <= n_steps)
        last = R // rb - 1
        blk = lambda b, c, last=last: jnp.minimum(b * steps_per_batch + c, last)
        in_specs.append(pl.BlockSpec((None, rb, N), lambda b, c, layer=layer, blk=blk: (layer, blk(b, c), 0)))
        out_specs.append(pl.BlockSpec((rb, N), lambda b, c, blk=blk: (blk(b, c), 0)))
        out_shapes.append(jax.ShapeDtypeStruct((R, N), BF16))
    return in_specs, out_specs, out_shapes


def _run_riders(in_refs, out_refs):
    for i_ref, o_ref in zip(in_refs, out_refs):
        o_ref[...] = i_ref[...].astype(BF16)


def _rms_store(x, g_ref, h_scr):
    ms = jnp.mean(x * x, axis=-1, keepdims=True)
    h_scr[...] = (x * lax.rsqrt(ms + RMS_EPS) * g_ref[...]).astype(BF16)


def _rms_to_scratch(x_ref, g_ref, h_scr):
    _rms_store(x_ref[...], g_ref, h_scr)


def _row_split(prompt_rows, sample_rows, tm):
    split = prompt_rows % tm
    assert (prompt_rows + sample_rows) % tm == 0 and tm - split == sample_rows
    return split


def _on_row_tile(split, p_ref, s_ref, fn):
    i = pl.program_id(0)
    last = pl.num_programs(0) - 1

    @pl.when(i < last)
    def _():
        fn(p_ref[...])

    @pl.when(i == last)
    def _():
        fn(jnp.concatenate([p_ref[0:split, :], s_ref[...]], axis=0))


def _hgrn_proj_kernel(layer, split, x_ref, *rest):
    xs_ref, rest = (rest[0], rest[1:]) if split is not None else (None, rest)
    gn_ref, wq_ref, wf_ref, wi_ref, wg_ref, lbl_ref, q_ref, k_ref, lf_ref, v_ref, gt_ref, h_scr = rest

    @pl.when(pl.program_id(1) == 0)
    def _():
        if split is None:
            _rms_to_scratch(x_ref, gn_ref, h_scr)
        else:
            _on_row_tile(split, x_ref, xs_ref, lambda x: _rms_store(x, gn_ref, h_scr))

    h = h_scr[...]
    aq = _wdot(h, wq_ref)
    q_ref[...] = (aq * _sigmoid(aq) * (HEAD_DK ** -0.5)).astype(BF16)

    lg = lbl_ref[...]
    mx = jnp.max(lg, axis=0, keepdims=True)
    e = jnp.exp(lg - mx)
    sm = e / jnp.sum(e, axis=0, keepdims=True)
    cs = sm[0:1]
    for r in range(1, layer + 1):
        cs = cs + sm[r:r + 1]
    lb = cs - sm[0:1]

    z = _wdot(h, wf_ref)
    ez = jnp.exp(-jnp.abs(z))
    r = 1.0 / (1.0 + ez)
    pos = z >= 0.0
    sig_p = jnp.where(pos, r, ez * r)
    sig_n = jnp.where(pos, ez * r, r)
    f = lb + (1.0 - lb) * sig_p
    lf_ref[...] = jnp.log(jnp.maximum(f, MIN_FORGET))
    k_ref[...] = ((1.0 - lb) * sig_n).astype(BF16)

    ag = _wdot(h, wg_ref)
    gt_ref[...] = (ag * _sigmoid(ag)).astype(BF16)
    v_ref[...] = _wdot(h, wi_ref).astype(BF16)


def _hgrn_proj(x, gain, wq, wf, wi, wg, lb_logits, layer, tm, tn):
    N = wq.shape[-1]
    L = lb_logits.shape[0]
    if isinstance(x, tuple):
        xp, xs = x
        D = xp.shape[1]
        M = xp.shape[0] + xs.shape[0]
        split = _row_split(xp.shape[0], xs.shape[0], tm)
        x_args = [xp, xs]
        x_specs = [pl.BlockSpec((tm, D), lambda i, n: (i, 0)), pl.BlockSpec(xs.shape, lambda i, n: (0, 0))]
    else:
        M, D = x.shape
        split, x_args, x_specs = None, [x], [pl.BlockSpec((tm, D), lambda i, n: (i, 0))]
    grid = (M // tm, N // tn)
    wspec = _wspec_cols(wq, layer, tn)
    ospec = pl.BlockSpec((tm, tn), lambda i, n: (i, n))
    return pl.pallas_call(
        functools.partial(_hgrn_proj_kernel, layer, split),
        grid=grid,
        in_specs=x_specs + [
                  pl.BlockSpec((1, D), lambda i, n: (0, 0)),
                  wspec, wspec, wspec, wspec,
                  pl.BlockSpec((L, tn), lambda i, n: (0, n))],
        out_specs=[ospec] * 5,
        out_shape=[jax.ShapeDtypeStruct((M, N), BF16),
                   jax.ShapeDtypeStruct((M, N), BF16),
                   jax.ShapeDtypeStruct((M, N), F32),
                   jax.ShapeDtypeStruct((M, N), BF16),
                   jax.ShapeDtypeStruct((M, N), BF16)],
        scratch_shapes=[pltpu.VMEM((tm, D), BF16)],
        compiler_params=_cparams(("parallel", "arbitrary")),
        name=f"hgrn_proj_{layer}",
    )(*x_args, gain, wq, wf, wi, wg, lb_logits)


def _gla_consts(C):
    nlev = int(math.log2(C))
    tri = np.tril(np.ones((C, C), np.float32))
    ii, jj = np.meshgrid(np.arange(C), np.arange(C), indexing="ij")
    x = ii ^ jj
    lev = np.floor(np.log2(np.maximum(x, 1))).astype(np.int32)
    lmap = np.where(ii == jj, -1, np.where(ii > jj, lev, -2)).astype(np.int32)
    rows = np.arange(C)
    sgn = np.stack([np.where((rows >> l) & 1 == 1, LOG2E if l == 0 else 1.0, 0.0 if l == 0 else -1.0)
                    for l in range(nlev)])
    sgn = np.broadcast_to(sgn[:, :, None], (nlev, C, LANES)).astype(np.float32)
    return jnp.asarray(tri, BF16), jnp.asarray(lmap, jnp.int32), jnp.asarray(sgn), nlev


SUBLANES = 8


def _level_exponent(l, g, g2, gs_ref, sgn):
    C = g.shape[0]
    h = 1 << l
    if l == 0:
        return g * sgn
    sub = lax.broadcasted_iota(jnp.int32, (SUBLANES, LANES), 0)
    bcast = lambda r: jnp.broadcast_to(gs_ref[pl.ds(r, 1), :], (SUBLANES, LANES))
    pieces = []
    for base in range(0, C, SUBLANES):
        if 2 * h >= SUBLANES:
            pieces.append(bcast((base // (2 * h)) * 2 * h + h - 1))
        else:
            p = bcast(base + h - 1)
            for blk in range(1, SUBLANES // (2 * h)):
                p = jnp.where(sub >= blk * 2 * h, bcast(base + blk * 2 * h + h - 1), p)
            pieces.append(p)
    gr = jnp.concatenate(pieces, axis=0)
    return (g2 - gr) * sgn


BF16_ROWS = 16


def _gla_level(l, q, k, qf, kf, g, g2, gs_ref, sgn_ref, lmap_b, acc):
    C = g.shape[0]
    h = 1 << l
    rb = BF16_ROWS
    if h < rb:
        e = jnp.exp2(_level_exponent(l, g, g2, gs_ref, sgn_ref[l]))
        al = _dot_nt((qf * e).astype(BF16), (kf * e).astype(BF16))
        return [jnp.where(lmap_b[i] == l, al[i * rb:(i + 1) * rb], a) for i, a in enumerate(acc)]
    bcast = lambda r: jnp.broadcast_to(gs_ref[pl.ds(r, 1), :], (h, LANES))
    lhs, rhs, upper_blocks = [], [], []
    for b in range(C // h):
        rows = slice(b * h, (b + 1) * h)
        if b % 2 == 1:
            e = jnp.exp2(g2[rows] - bcast(b * h - 1))
            lhs.append((qf[rows] * e).astype(BF16))
            rhs.append(k[rows])
            upper_blocks += list(range(b * h // rb, (b + 1) * h // rb))
        else:
            e = jnp.exp2(bcast((b + 1) * h - 1) - g2[rows])
            rhs.append((kf[rows] * e).astype(BF16))
    al = _dot_nt(jnp.concatenate(lhs, axis=0), jnp.concatenate(rhs, axis=0))
    acc = list(acc)
    for j, i in enumerate(upper_blocks):
        acc[i] = jnp.where(lmap_b[i] == l, al[j * rb:(j + 1) * rb], acc[i])
    return acc


def _column_broadcast(row):
    n = row.shape[1]
    eye = lax.broadcasted_iota(jnp.int32, (n, n), 0) == lax.broadcasted_iota(jnp.int32, (n, n), 1)
    x = jnp.where(eye, jnp.broadcast_to(row, (n, n)), 0.0)
    xh = x.astype(BF16)
    xl = (x - xh.astype(F32)).astype(BF16)
    ones = jnp.ones((n, n), BF16)
    return _dot(xh, ones) + _dot(xl, ones)


def _gla_core(chains, tri, lmap, sgn_ref, nlev, gs_scr, state_transposed=True):
    C = chains[0][0].shape[0]
    rb = BF16_ROWS
    lmap_b = [lmap[i * rb:(i + 1) * rb] for i in range(C // rb)]
    g2s = []
    for u, (q, k, g, v, st) in enumerate(chains):
        ghi = g.astype(BF16)
        glo = (g - ghi.astype(F32)).astype(BF16)
        c2 = _dot(tri, jnp.concatenate([ghi, glo], axis=1))
        g2 = (c2[:, :LANES] + c2[:, LANES:]) * LOG2E
        gs_scr[u] = g2
        g2s.append(g2)
    accs = []
    for (q, k, g, v, st) in chains:
        d = _dot_nt(q, k)
        accs.append([jnp.where(lmap_b[i] == -1, d[i * rb:(i + 1) * rb], 0.0) for i in range(C // rb)])
    qk32 = [(q.astype(F32), k.astype(F32)) for (q, k, g, v, st) in chains]
    for l in range(nlev):
        for u, (q, k, g, v, st) in enumerate(chains):
            qf, kf = qk32[u]
            accs[u] = _gla_level(l, q, k, qf, kf, g, g2s[u], gs_scr.at[u], sgn_ref, lmap_b, accs[u])
    outs = []
    for u, (q, k, g, v, st) in enumerate(chains):
        g2 = g2s[u]
        qf, kf = qk32[u]
        o = _dot(jnp.concatenate(accs[u], axis=0).astype(BF16), v)
        qdec = (qf * jnp.exp2(g2)).astype(BF16)
        glast = g2[C - 1:C, :]
        kdec = (kf * jnp.exp2(glast - g2)).astype(BF16)
        if state_transposed:
            o = o + _dot_nt(qdec, st.astype(BF16))
            st_new = st * jnp.exp2(glast) + _dot_tn(v, kdec)
        else:
            o = o + _dot(qdec, st.astype(BF16))
            st_new = st * _column_broadcast(jnp.exp2(glast)) + _dot_tn(kdec, v)
        outs.append((o, st_new))
    return outs


def _gla_out(o, gon, gate):
    ms = jnp.mean(o * o, axis=-1, keepdims=True)
    return o * lax.rsqrt(ms + RMS_EPS) * gon * gate


def _gla_prompt_kernel(nlev, n_heads, n_riders, q_ref, k_ref, g_ref, v_ref, gt_ref, tri_ref, lmap_ref,
                       sgn_ref, gon_ref, *rest):
    rider_in, rest = rest[:n_riders], rest[n_riders:]
    o_ref, sfin_ref = rest[0], rest[1]
    rider_out = rest[2:2 + n_riders]
    st_scr, gs_scr = rest[2 + n_riders:]
    _run_riders(rider_in, rider_out)
    c = pl.program_id(1)
    hpi = _heads_per_iter(n_heads)
    n_iter = n_heads // hpi

    @pl.when(c == 0)
    def _():
        st_scr[...] = jnp.zeros_like(st_scr)

    def body(j, carry):
        heads = [j * hpi + u for u in range(hpi)]
        lanes = [pl.ds(pl.multiple_of(h * LANES, LANES), LANES) for h in heads]
        chains = [(q_ref[:, hs], k_ref[:, hs], g_ref[:, hs], v_ref[:, hs], st_scr[h])
                  for h, hs in zip(heads, lanes)]
        res = _gla_core(chains, tri_ref[...], lmap_ref[...], sgn_ref, nlev, gs_scr)
        for h, hs, (o, st_new) in zip(heads, lanes, res):
            st_scr[h] = st_new
            o_ref[:, hs] = _gla_out(o, gon_ref[...], gt_ref[:, hs].astype(F32)).astype(BF16)
        return carry

    lax.fori_loop(0, n_iter, body, 0)

    @pl.when(c == pl.num_programs(1) - 1)
    def _():
        def wb(h, carry):
            sfin_ref[0, h] = st_scr[h].T
            return carry
        lax.fori_loop(0, n_heads, wb, 0)


def _gla_prompt(q, k, g, v, gate, gon, batch, seq, riders, name):
    C = GLA_CHUNK
    W = q.shape[1]
    H = W // LANES
    nc = seq // C
    tri, lmap, sgn, nlev = _gla_consts(C)
    rspec = pl.BlockSpec((C, W), lambda b, c: (b * nc + c, 0))
    r_in, r_out, r_shape = _rider_specs(riders, nc, batch * nc)
    return pl.pallas_call(
        functools.partial(_gla_prompt_kernel, nlev, H, len(riders)),
        grid=(batch, nc),
        in_specs=[rspec, rspec, rspec, rspec, rspec,
                  pl.BlockSpec(tri.shape, lambda b, c: (0, 0)),
                  pl.BlockSpec(lmap.shape, lambda b, c: (0, 0)),
                  pl.BlockSpec(sgn.shape, lambda b, c: (0, 0, 0)),
                  pl.BlockSpec((1, LANES), lambda b, c: (0, 0))] + r_in,
        out_specs=[rspec, pl.BlockSpec((1, H, LANES, LANES), lambda b, c: (b, 0, 0, 0))] + r_out,
        out_shape=[jax.ShapeDtypeStruct((batch * seq, W), BF16),
                   jax.ShapeDtypeStruct((batch, H, LANES, LANES), F32)] + r_shape,
        scratch_shapes=[pltpu.VMEM((H, LANES, LANES), F32),
                        pltpu.VMEM((_heads_per_iter(H), C, LANES), F32)],
        compiler_params=_cparams(("arbitrary", "arbitrary")),
        name=name,
    )(q, k, g, v, gate, tri, lmap, sgn, gon, *[w for w, _ in riders])


def _gla_sample_kernel(nlev, n_heads, n_new, layer, creates_states, q_ref, k_ref, g_ref, v_ref, gt_ref,
                       tri_ref, lmap_ref, sgn_ref, gon_ref, s0_ref, *rest):
    if creates_states:
        o_ref, sall_ref, gs_scr = rest
        for other in range(sall_ref.shape[0]):
            if other != layer:
                sall_ref[other] = jnp.zeros(sall_ref.shape[1:], F32)
        sfin_ref = sall_ref.at[layer]
    else:
        _, o_ref, sfin_ref, gs_scr = rest
    nb = SAMPLE_PAD // n_new
    zpad = jnp.zeros((SAMPLE_PAD - n_new, LANES), F32)
    hpi = _heads_per_iter(n_heads)

    def body(j, carry):
        heads = [j * hpi + u for u in range(hpi)]
        lanes = [pl.ds(pl.multiple_of(h * LANES, LANES), LANES) for h in heads]
        chains = []
        for h, hs in zip(heads, lanes):
            q = q_ref[:, hs].astype(F32)
            k = k_ref[:, hs].astype(F32)
            g = g_ref[:, hs]
            v = v_ref[:, hs].astype(F32)
            for bb in range(nb):
                rs = slice(bb * n_new, (bb + 1) * n_new)
                pad = lambda t: jnp.concatenate([t[rs], zpad], axis=0)
                chains.append((pad(q).astype(BF16), pad(k).astype(BF16), pad(g), pad(v).astype(BF16),
                               s0_ref[bb, h]))
        res = _gla_core(chains, tri_ref[...], lmap_ref[...], sgn_ref, nlev, gs_scr, state_transposed=False)
        for u, (h, hs) in enumerate(zip(heads, lanes)):
            gate = gt_ref[:, hs].astype(F32)
            outs = []
            for bb in range(nb):
                o, st_new = res[u * nb + bb]
                sfin_ref[bb, h] = st_new
                outs.append(_gla_out(o[0:n_new], gon_ref[...], gate[bb * n_new:(bb + 1) * n_new]))
            o_ref[:, hs] = jnp.concatenate(outs, axis=0).astype(BF16)
        return carry

    lax.fori_loop(0, n_heads // hpi, body, 0)


def _gla_sample(q, k, g, v, gate, gon, s0, layer, row0, n_seq, n_new, name, states=None):
    W = q.shape[1]
    H = W // LANES
    L = s0.shape[0]
    nb = SAMPLE_PAD // n_new
    tri, lmap, sgn, nlev = _gla_consts(SAMPLE_PAD)
    blk0 = row0 // SAMPLE_PAD
    rspec = pl.BlockSpec((SAMPLE_PAD, W), lambda i: (blk0 + i, 0))
    s0spec = pl.BlockSpec((None, nb, H, LANES, LANES), lambda i: (layer, i, 0, 0, 0))
    in_specs = [rspec, rspec, rspec, rspec, rspec,
                pl.BlockSpec(tri.shape, lambda i: (0, 0)),
                pl.BlockSpec(lmap.shape, lambda i: (0, 0)),
                pl.BlockSpec(sgn.shape, lambda i: (0, 0, 0)),
                pl.BlockSpec((1, LANES), lambda i: (0, 0)),
                s0spec]
    args = [q, k, g, v, gate, tri, lmap, sgn, gon, s0]
    if states is None:
        sspec = pl.BlockSpec((L, nb, H, LANES, LANES), lambda i: (0, i, 0, 0, 0))
        aliases = {}
    else:
        sspec = s0spec
        in_specs.append(pl.BlockSpec(memory_space=pl.ANY))
        args.append(states)
        aliases = {len(args) - 1: 1}
    return pl.pallas_call(
        functools.partial(_gla_sample_kernel, nlev, H, n_new, layer, states is None),
        grid=(n_seq // nb,),
        in_specs=in_specs,
        out_specs=[pl.BlockSpec((SAMPLE_PAD, W), lambda i: (i, 0)), sspec],
        out_shape=[jax.ShapeDtypeStruct((n_seq * n_new, W), BF16),
                   jax.ShapeDtypeStruct((L, n_seq, H, LANES, LANES), F32)],
        input_output_aliases=aliases,
        scratch_shapes=[pltpu.VMEM((_heads_per_iter(H) * nb, SAMPLE_PAD, LANES), F32)],
        compiler_params=_cparams(("parallel",)),
        name=name,
    )(*args)


def _out_proj_kernel(split, x_is_pair, ap_ref, as_ref, w_ref, x_ref, *rest):
    xs_ref, o_ref = (rest[0], rest[1]) if x_is_pair else (None, rest[0])
    i = pl.program_id(0)
    last = pl.num_programs(0) - 1
    tail = lambda p_ref, s_ref: jnp.concatenate([p_ref[0:split, :], s_ref[...]], axis=0)

    @pl.when(i < last)
    def _():
        o_ref[...] = x_ref[...] + _wdot(ap_ref[...], w_ref)

    @pl.when(i == last)
    def _():
        x = tail(x_ref, xs_ref) if x_is_pair else x_ref[...]
        o_ref[...] = x + _wdot(tail(ap_ref, as_ref), w_ref)


def _out_proj(a_prompt, a_sample, w, x, tm, name):
    MP, K = a_prompt.shape
    MS = a_sample.shape[0]
    N = w.shape[1]
    M = MP + MS
    split = _row_split(MP, MS, tm)
    x_is_pair = isinstance(x, tuple)
    if x_is_pair:
        x_args = list(x)
        x_specs = [pl.BlockSpec((tm, N), lambda i: (i, 0)), pl.BlockSpec((MS, N), lambda i: (0, 0))]
    else:
        x_args, x_specs = [x], [pl.BlockSpec((tm, N), lambda i: (i, 0))]
    return pl.pallas_call(
        functools.partial(_out_proj_kernel, split, x_is_pair),
        grid=(M // tm,),
        in_specs=[pl.BlockSpec((tm, K), lambda i: (i, 0)),
                  pl.BlockSpec((MS, K), lambda i: (0, 0)),
                  pl.BlockSpec((K, N), lambda i: (0, 0))] + x_specs,
        out_specs=pl.BlockSpec((tm, N), lambda i: (i, 0)),
        out_shape=jax.ShapeDtypeStruct((M, N), F32),
        compiler_params=_cparams(("parallel",)),
        name=name,
    )(a_prompt, a_sample, w, *x_args)


def _ffn_kernel(split, x_ref, gn_ref, wg_ref, wu_ref, wd_ref, o_ref, *rest):
    ys_ref = rest[0] if split is not None else None
    h_scr = rest[-1]
    f = pl.program_id(1)

    @pl.when(f == 0)
    def _():
        _rms_to_scratch(x_ref, gn_ref, h_scr)
        o_ref[...] = x_ref[...]

    h = h_scr[...]
    g = _wdot(h, wg_ref)
    u = _wdot(h, wu_ref)
    a = (g * _sigmoid(g) * u).astype(BF16)
    o_ref[...] += _wdot(a, wd_ref)

    if split is not None:
        @pl.when((f == pl.num_programs(1) - 1) & (pl.program_id(0) == pl.num_programs(0) - 1))
        def _():
            ys_ref[...] = o_ref[split:, :]


def _ffn(x, gain, wg, wu, wd, tm, tf, name, prompt_rows=None):
    M, D = x.shape
    FF = wg.shape[1]
    o_spec = pl.BlockSpec((tm, D), lambda i, f: (i, 0))
    if prompt_rows is None:
        split, out_specs, out_shape = None, o_spec, jax.ShapeDtypeStruct((M, D), F32)
    else:
        split = prompt_rows % tm
        ms = M - prompt_rows
        assert tm - split == ms
        out_specs = [o_spec, pl.BlockSpec((ms, D), lambda i, f: (0, 0))]
        out_shape = [jax.ShapeDtypeStruct((prompt_rows, D), F32), jax.ShapeDtypeStruct((ms, D), F32)]
    return pl.pallas_call(
        functools.partial(_ffn_kernel, split),
        grid=(M // tm, FF // tf),
        in_specs=[pl.BlockSpec((tm, D), lambda i, f: (i, 0)),
                  pl.BlockSpec((1, D), lambda i, f: (0, 0)),
                  pl.BlockSpec((D, tf), lambda i, f: (0, f)),
                  pl.BlockSpec((D, tf), lambda i, f: (0, f)),
                  pl.BlockSpec((tf, D), lambda i, f: (f, 0))],
        out_specs=out_specs,
        out_shape=out_shape,
        scratch_shapes=[pltpu.VMEM((tm, D), BF16)],
        compiler_params=_cparams(("arbitrary", "arbitrary")),
        name=name,
    )(x, gain, wg, wu, wd)


def _head_norm_rope(acc, gain, cos, sin, scale):
    r = lax.broadcasted_iota(jnp.int32, (LANES, LANES), 0)
    c = lax.broadcasted_iota(jnp.int32, (LANES, LANES), 1)
    seg = jnp.where((r >> 6) == (c >> 6), 1.0, 0.0).astype(BF16)
    ss = _dot((acc * acc).astype(BF16), seg)
    y = acc * lax.rsqrt(ss * (1.0 / ATTN_HEAD_DIM) + RMS_EPS) * gain
    lane = lax.broadcasted_iota(jnp.int32, y.shape, 1)
    half = ATTN_HEAD_DIM // 2
    rot = jnp.where((lane & half) == 0, pltpu.roll(y, LANES - half, 1), pltpu.roll(y, half, 1))
    out = y * cos + rot * sin
    return out * scale if scale != 1.0 else out


def _rope_cols(h, w_ref, hg_ref, cos_ref, sin_ref, o_ref, tn, scale):
    for n in range(w_ref.shape[1] // tn):
        acc = _dot(h, w_ref[:, n * tn:(n + 1) * tn])
        for c in range(tn // LANES):
            cs = slice(c * LANES, (c + 1) * LANES)
            os_ = slice(n * tn + c * LANES, n * tn + (c + 1) * LANES)
            o_ref[:, os_] = _head_norm_rope(acc[:, cs], hg_ref[...], cos_ref[...], sin_ref[...],
                                            scale).astype(BF16)


def _q_proj_kernel(tn, x_ref, gn_ref, w_ref, hg_ref, cos_ref, sin_ref, q_ref, h_scr):
    _rms_to_scratch(x_ref, gn_ref, h_scr)
    _rope_cols(h_scr[...], w_ref, hg_ref, cos_ref, sin_ref, q_ref, tn, ATTN_SCALE * LOG2E)


def _q_proj(x, gain, w, head_gain, cos, sin, tm, tn, name):
    M, D = x.shape
    N = w.shape[1]
    return pl.pallas_call(
        functools.partial(_q_proj_kernel, tn),
        grid=(M // tm,),
        in_specs=[pl.BlockSpec((tm, D), lambda i: (i, 0)),
                  pl.BlockSpec((1, D), lambda i: (0, 0)),
                  pl.BlockSpec((D, N), lambda i: (0, 0)),
                  pl.BlockSpec((1, LANES), lambda i: (0, 0)),
                  pl.BlockSpec((tm, LANES), lambda i: (i, 0)),
                  pl.BlockSpec((tm, LANES), lambda i: (i, 0))],
        out_specs=pl.BlockSpec((tm, N), lambda i: (i, 0)),
        out_shape=jax.ShapeDtypeStruct((M, N), BF16),
        scratch_shapes=[pltpu.VMEM((tm, D), BF16)],
        compiler_params=_cparams(("parallel",)),
        name=name,
    )(x, gain, w, head_gain, cos, sin)


def _kv_proj_kernel(tn, x_ref, gn_ref, wk_ref, wv_ref, hg_ref, cos_ref, sin_ref, k_ref, v_ref, h_scr):
    _rms_to_scratch(x_ref, gn_ref, h_scr)
    h = h_scr[...]
    _rope_cols(h, wk_ref, hg_ref, cos_ref, sin_ref, k_ref, tn, 1.0)
    for n in range(wv_ref.shape[1] // tn):
        ns = slice(n * tn, (n + 1) * tn)
        v_ref[:, ns] = _dot(h, wv_ref[:, ns]).astype(BF16)


def _kv_proj(x, gain, wk, wv, head_gain, cos, sin, tm, tn):
    M, D = x.shape
    N = wk.shape[1]
    wspec = pl.BlockSpec((D, N), lambda i: (0, 0))
    ospec = pl.BlockSpec((tm, N), lambda i: (i, 0))
    return pl.pallas_call(
        functools.partial(_kv_proj_kernel, tn),
        grid=(M // tm,),
        in_specs=[pl.BlockSpec((tm, D), lambda i: (i, 0)),
                  pl.BlockSpec((1, D), lambda i: (0, 0)),
                  wspec, wspec,
                  pl.BlockSpec((1, LANES), lambda i: (0, 0)),
                  pl.BlockSpec((tm, LANES), lambda i: (i, 0)),
                  pl.BlockSpec((tm, LANES), lambda i: (i, 0))],
        out_specs=[ospec, ospec],
        out_shape=[jax.ShapeDtypeStruct((M, N), BF16), jax.ShapeDtypeStruct((M, N), BF16)],
        scratch_shapes=[pltpu.VMEM((tm, D), BF16)],
        compiler_params=_cparams(("parallel",)),
        name="kv_proj",
    )(x, gain, wk, wv, head_gain, cos, sin)


def _softmax_sink(s_parts, sink2):
    m = sink2
    for s in s_parts:
        m = jnp.maximum(m, jnp.max(s, axis=-1, keepdims=True))
    den = jnp.exp2(sink2 - m)
    ps = []
    for s in s_parts:
        p = jnp.exp2(s - m)
        den = den + jnp.sum(p, axis=-1, keepdims=True)
        ps.append(p.astype(BF16))
    return ps, den


def _pv(ps, v_parts, den):
    o = None
    for p, v in zip(ps, v_parts):
        pv = _dot(p, v)
        o = pv if o is None else o + pv
    return o / den


def _swa_prompt_kernel(n_riders, sk_ref, q_ref, kp_ref, kc_ref, vp_ref, vc_ref, *rest):
    rider_in, o_ref, rider_out = rest[:n_riders], rest[n_riders], rest[n_riders + 1:]
    _run_riders(rider_in, rider_out)
    i = pl.program_id(1)
    W = WINDOW
    row = lax.broadcasted_iota(jnp.int32, (W, 2 * W), 0)
    col = lax.broadcasted_iota(jnp.int32, (W, 2 * W), 1)
    first_key = jnp.where(i > 0, 0, W)
    mask = (col > row) & (col <= row + W) & (col >= first_key)
    lo = lax.broadcasted_iota(jnp.int32, (W, LANES), 1) < ATTN_HEAD_DIM

    def scores(g):
        gs = slice(g * LANES, (g + 1) * LANES)
        kd = jnp.concatenate([kp_ref[:, gs], kc_ref[:, gs]], axis=0)
        lhs = []
        for c in range(GQA_GROUP // 2):
            qc = q_ref[:, (2 * g + c) * LANES:(2 * g + c + 1) * LANES]
            zero = jnp.zeros_like(qc)
            lhs += [jnp.where(lo, qc, zero), jnp.where(lo, zero, qc)]
        s = _dot_nt(jnp.concatenate(lhs, axis=0), kd)
        return [jnp.where(mask, s[a * W:(a + 1) * W], MASK_VALUE) for a in range(GQA_GROUP)]

    def finish(g, s_heads):
        gs = slice(g * LANES, (g + 1) * LANES)
        vd = jnp.concatenate([vp_ref[:, gs], vc_ref[:, gs]], axis=0)
        ps, dens = [], []
        for a, s in enumerate(s_heads):
            (p,), den = _softmax_sink([s], sk_ref[GQA_GROUP * g + a] * LOG2E)
            ps.append(p)
            dens.append(den)
        o = _dot(jnp.concatenate(ps, axis=0), vd)
        for c in range(GQA_GROUP // 2):
            cs = slice((2 * g + c) * LANES, (2 * g + c + 1) * LANES)
            o0 = o[(2 * c) * W:(2 * c + 1) * W] / dens[2 * c]
            o1 = o[(2 * c + 1) * W:(2 * c + 2) * W] / dens[2 * c + 1]
            o_ref[:, cs] = jnp.where(lo, o0, o1).astype(BF16)

    s_next = scores(0)
    for g in range(ATTN_KV_HEADS):
        s_cur = s_next
        if g + 1 < ATTN_KV_HEADS:
            s_next = scores(g + 1)
        finish(g, s_cur)


def _swa_prompt(q, kdup, vdup, sinks, batch, seq, riders, name):
    W = WINDOW
    nb = seq // W
    QW = q.shape[1]
    KW = kdup.shape[1]
    prev = lambda b, i: (b * nb + jnp.maximum(i - 1, 0), 0)
    cur = lambda b, i: (b * nb + i, 0)
    r_in, r_out, r_shape = _rider_specs(riders, nb, batch * nb)
    return pl.pallas_call(
        functools.partial(_swa_prompt_kernel, len(riders)),
        grid=(batch, nb),
        in_specs=[pl.BlockSpec(memory_space=pltpu.SMEM),
                  pl.BlockSpec((W, QW), cur),
                  pl.BlockSpec((W, KW), prev), pl.BlockSpec((W, KW), cur),
                  pl.BlockSpec((W, KW), prev), pl.BlockSpec((W, KW), cur)] + r_in,
        out_specs=[pl.BlockSpec((W, QW), cur)] + r_out,
        out_shape=[jax.ShapeDtypeStruct((batch * seq, QW), BF16)] + r_shape,
        compiler_params=_cparams(("arbitrary", "arbitrary")),
        name=name,
    )(sinks, q, kdup, kdup, vdup, vdup, *[w for w, _ in riders])


def _swa_sample_kernel(n_new, sk_ref, q_ref, kn_ref, vn_ref, kc_ref, vc_ref, o_ref):
    W = WINDOW
    nb = SAMPLE_PAD // n_new
    rows = GQA_GROUP * n_new
    qi = lax.broadcasted_iota(jnp.int32, (rows, W), 0) & (n_new - 1)
    mask_c = lax.broadcasted_iota(jnp.int32, (rows, W), 1) > qi
    qi_n = lax.broadcasted_iota(jnp.int32, (rows, SAMPLE_PAD), 0) & (n_new - 1)
    mask_n = lax.broadcasted_iota(jnp.int32, (rows, SAMPLE_PAD), 1) <= qi_n
    hrow = lax.broadcasted_iota(jnp.int32, (rows, 1), 0) // n_new
    lo = lax.broadcasted_iota(jnp.int32, (n_new, LANES), 1) < ATTN_HEAD_DIM
    zq = jnp.zeros((n_new, LANES), F32)
    zpad = jnp.zeros((SAMPLE_PAD - n_new, LANES), F32)
    q = q_ref[...].astype(F32)
    kn = kn_ref[...].astype(F32)
    vn = vn_ref[...].astype(F32)
    chains = [(bb, g) for bb in range(nb) for g in range(ATTN_KV_HEADS)]
    sinks2 = []
    for g in range(ATTN_KV_HEADS):
        sink = jnp.zeros((rows, 1), F32)
        for a in range(GQA_GROUP):
            sink = jnp.where(hrow == a, sk_ref[GQA_GROUP * g + a] * LOG2E, sink)
        sinks2.append(sink)
    scores, values = [], []
    for bb, g in chains:
        rs = slice(bb * n_new, (bb + 1) * n_new)
        gs = slice(g * LANES, (g + 1) * LANES)
        kn_g = jnp.concatenate([kn[rs, gs], zpad], axis=0).astype(BF16)
        vn_g = jnp.concatenate([vn[rs, gs], zpad], axis=0).astype(BF16)
        lhs = []
        for a in range(GQA_GROUP):
            c, par = divmod(a, 2)
            qc = q[rs, (2 * g + c) * LANES:(2 * g + c + 1) * LANES]
            lhs.append(jnp.where(lo, qc, zq) if par == 0 else jnp.where(lo, zq, qc))
        lhs = jnp.concatenate(lhs, axis=0).astype(BF16)
        s_c = jnp.where(mask_c, _dot_nt(lhs, kc_ref[bb, :, gs]), MASK_VALUE)
        s_n = jnp.where(mask_n, _dot_nt(lhs, kn_g), MASK_VALUE)
        scores.append([s_c, s_n])
        values.append([vc_ref[bb, :, gs], vn_g])
    probs = [_softmax_sink(s, sinks2[g]) for s, (bb, g) in zip(scores, chains)]
    outs = [_pv(ps, v, den) for (ps, den), v in zip(probs, values)]
    out_rows = []
    for bb in range(nb):
        chunks = []
        for g in range(ATTN_KV_HEADS):
            o = outs[bb * ATTN_KV_HEADS + g]
            for c in range(GQA_GROUP // 2):
                chunks.append(jnp.where(lo, o[(2 * c) * n_new:(2 * c + 1) * n_new],
                                        o[(2 * c + 1) * n_new:(2 * c + 2) * n_new]))
        out_rows.append(jnp.concatenate(chunks, axis=1))
    o_ref[...] = jnp.concatenate(out_rows, axis=0).astype(BF16)


def _swa_sample(q, kdup, vdup, cache_kdup, cache_vdup, sinks, row0, n_seq, n_new, name):
    W = WINDOW
    QW = q.shape[1]
    KW = kdup.shape[1]
    nb = SAMPLE_PAD // n_new
    blk0 = row0 // SAMPLE_PAD
    rmap = lambda i: (blk0 + i, 0)
    cspec = pl.BlockSpec((nb, W, KW), lambda i: (i, 0, 0))
    return pl.pallas_call(
        functools.partial(_swa_sample_kernel, n_new),
        grid=(n_seq // nb,),
        in_specs=[pl.BlockSpec(memory_space=pltpu.SMEM),
                  pl.BlockSpec((SAMPLE_PAD, QW), rmap),
                  pl.BlockSpec((SAMPLE_PAD, KW), rmap), pl.BlockSpec((SAMPLE_PAD, KW), rmap),
                  cspec, cspec],
        out_specs=pl.BlockSpec((SAMPLE_PAD, QW), lambda i: (i, 0)),
        out_shape=jax.ShapeDtypeStruct((n_seq * n_new, QW), BF16),
        compiler_params=_cparams(("parallel",)),
        name=name,
    )(sinks, q, kdup, vdup, cache_kdup, cache_vdup)


def _dup_heads(w):
    lead = w.shape[:-1]
    w3 = w.reshape(lead + (ATTN_KV_HEADS, ATTN_HEAD_DIM))
    return jnp.concatenate([w3, w3], axis=-1).reshape(lead + (ATTN_KV_HEADS * 2 * ATTN_HEAD_DIM,))


def _undup_heads(a, lead):
    return a.reshape(lead + (ATTN_KV_HEADS, 2 * ATTN_HEAD_DIM))[..., :ATTN_HEAD_DIM].astype(F32)


def _rope_tables(pos):
    half = ATTN_HEAD_DIM // 2
    inv = ROPE_THETA ** (-jnp.arange(half, dtype=F32) / half)
    ang = pos.astype(F32)[:, None] * inv[None, :]
    cos = jnp.cos(ang)
    sin = jnp.sin(ang)
    reps = LANES // ATTN_HEAD_DIM
    cos_t = jnp.tile(jnp.concatenate([cos, cos], axis=-1), (1, reps))
    sin_t = jnp.tile(jnp.concatenate([-sin, sin], axis=-1), (1, reps))
    return cos_t, sin_t


def kernel(x_prompt, x_sample, state_hgrn, cache_k_win, cache_v_win, hgrn_norm, hgrn_wq, hgrn_wf, hgrn_wi, hgrn_wg, hgrn_lb_logits, hgrn_onorm, hgrn_wo, kv_norm, w_k, w_v, k_norm, attn_norm, attn_wq, q_norm, sinks, attn_wo, ffn_norm, w_gate, w_up, w_down):
    B, T, D = x_prompt.shape
    SB, S, _ = x_sample.shape
    MP = B * T
    MS = SB * S
    n_a = hgrn_wq.shape[0]
    n_b = attn_wq.shape[0]
    assert SAMPLE_PAD % S == 0 and SB % (SAMPLE_PAD // S) == 0 and MP % SAMPLE_PAD == 0
    assert T % GLA_CHUNK == 0 and T % WINDOW == 0

    x = (x_prompt.reshape(MP, D).astype(F32), x_sample.reshape(MS, D).astype(F32))
    M = MP + MS
    tm = 768 if M % 768 == 0 else 256
    tn = 512
    tn_f32 = 256
    tf = 512

    bf = lambda w: w.astype(BF16)
    f32 = lambda w: w.astype(F32)
    row = lambda g: g.reshape(1, -1).astype(F32)
    pos = jnp.concatenate([jnp.tile(jnp.arange(T), B), jnp.tile(PAST_LEN + jnp.arange(S), SB)])
    cos_t, sin_t = _rope_tables(pos)
    head_gain = lambda g: jnp.tile(g.astype(F32), LANES // ATTN_HEAD_DIM).reshape(1, LANES)

    w_gate32, w_up32, w_down32 = f32(w_gate), f32(w_up), f32(w_down)
    hgrn_w32 = [f32(hgrn_wq), f32(hgrn_wf), f32(hgrn_wi), f32(hgrn_wg)]
    hgrn_wo32, attn_wq32, attn_wo32 = f32(hgrn_wo), f32(attn_wq), f32(attn_wo)
    ffn_riders = lambda l: [(w_gate32, l), (w_up32, l), (w_down32, l)]

    st_prompt, st_sample = [], None
    s0 = f32(state_hgrn)
    proj_w = hgrn_w32
    attn_wq_bf = None
    for l in range(n_a):
        q, k, lf, v, gate = _hgrn_proj(x, row(hgrn_norm[l]), *proj_w, f32(hgrn_lb_logits), l, tm,
                                       tn_f32 if proj_w[0].ndim == 3 else tn)
        gon = row(hgrn_onorm[l])
        riders = ffn_riders(l) + [(hgrn_wo32, l)]
        if l + 1 < n_a:
            riders += [(w, l + 1) for w in hgrn_w32]
        elif n_b > 0:
            riders += [(attn_wq32, 0)]
        o_p, s_p, wg_bf, wu_bf, wd_bf, wo_bf, *nxt = _gla_prompt(q, k, lf, v, gate, gon, B, T, riders,
                                                                 f"gla_prompt_{l}")
        if l + 1 < n_a:
            proj_w = nxt
        elif n_b > 0:
            attn_wq_bf = nxt[0]
        o_s, st_sample = _gla_sample(q, k, lf, v, gate, gon, s0, l, MP, SB, S, f"gla_sample_{l}",
                                     states=st_sample)
        st_prompt.append(s_p)
        x = _out_proj(o_p, o_s, wo_bf, x, tm, f"hgrn_out_{l}")
        x = _ffn(x, row(ffn_norm[l]), wg_bf, wu_bf, wd_bf, tm, tf, f"ffn_{l}")

    kdup, vdup = _kv_proj(x, row(kv_norm), bf(_dup_heads(w_k)), bf(_dup_heads(w_v)), head_gain(k_norm),
                          cos_t, sin_t, tm, tn)
    KW = ATTN_KV_HEADS * ATTN_HEAD_DIM
    cache_kdup = bf(_dup_heads(cache_k_win.reshape(SB, WINDOW, KW)))
    cache_vdup = bf(_dup_heads(cache_v_win.reshape(SB, WINDOW, KW)))
    for j in range(n_b):
        l = n_a + j
        if attn_wq_bf is None:
            attn_wq_bf = bf(attn_wq[j])
        q = _q_proj(x, row(attn_norm[j]), attn_wq_bf, head_gain(q_norm[j]), cos_t, sin_t, tm, tn,
                    f"q_proj_{j}")
        sk = sinks[j].astype(F32)
        riders = ffn_riders(l) + [(attn_wo32, j)] + ([(attn_wq32, j + 1)] if j + 1 < n_b else [])
        a_p, wg_bf, wu_bf, wd_bf, wo_bf, *nxt = _swa_prompt(q, kdup, vdup, sk, B, T, riders, f"swa_prompt_{j}")
        attn_wq_bf = nxt[0] if nxt else None
        a_s = _swa_sample(q, kdup, vdup, cache_kdup, cache_vdup, sk, MP, SB, S, f"swa_sample_{j}")
        x = _out_proj(a_p, a_s, wo_bf, x, tm, f"attn_out_{j}")
        last = j == n_b - 1
        x = _ffn(x, row(ffn_norm[l]), wg_bf, wu_bf, wd_bf, tm, tf, f"ffn_{l}",
                 prompt_rows=MP if last else None)

    y_prompt = x[0].reshape(B, T, D)
    y_sample = x[1].reshape(SB, S, D)
    k_win_p = _undup_heads(kdup[:MP].reshape(B, T, -1)[:, T - WINDOW:], (B, WINDOW))
    v_win_p = _undup_heads(vdup[:MP].reshape(B, T, -1)[:, T - WINDOW:], (B, WINDOW))
    k_new_s = _undup_heads(kdup[MP:].reshape(SB, S, -1), (SB, S))
    v_new_s = _undup_heads(vdup[MP:].reshape(SB, S, -1), (SB, S))
    k_win_s = jnp.concatenate([cache_k_win[:, S:].astype(F32), k_new_s], axis=1)
    v_win_s = jnp.concatenate([cache_v_win[:, S:].astype(F32), v_new_s], axis=1)
    return (y_prompt, y_sample, jnp.stack(st_prompt), st_sample,
            k_win_p, v_win_p, k_win_s, v_win_s)
```

```python
import functools
import math

import numpy as np
import jax
import jax.numpy as jnp
from jax import lax
from jax.experimental import pallas as pl
from jax.experimental.pallas import tpu as pltpu

HGRN_HEADS = 16
HEAD_DK = 128
ATTN_HEAD_DIM = 64
ATTN_KV_HEADS = 8
GQA_GROUP = 4
WINDOW = 128
ROPE_THETA = 10000.0
ATTN_SCALE = ATTN_HEAD_DIM ** -0.5
LOG2E = math.log2(math.e)
RMS_EPS = 1e-6
MASK_VALUE = -1e30
MIN_FORGET = 1e-30
PAST_LEN = 16384

LANES = 128
VMEM_LIMIT_BYTES = 56 * 1024 * 1024

GLA_CHUNK = 128
SAMPLE_PAD = 16
GLA_HEADS_PER_ITER = 16


def _heads_per_iter(n_heads):
    return math.gcd(n_heads, GLA_HEADS_PER_ITER)

F32 = jnp.float32
BF16 = jnp.bfloat16


def _cparams(sem):
    return pltpu.CompilerParams(dimension_semantics=sem, vmem_limit_bytes=VMEM_LIMIT_BYTES)


def _dot(a, b):
    return jnp.dot(a, b, preferred_element_type=F32)


def _dot_nt(a, b):
    return lax.dot_general(a, b, (((1,), (1,)), ((), ())), preferred_element_type=F32)


def _dot_tn(a, b):
    return lax.dot_general(a, b, (((0,), (0,)), ((), ())), preferred_element_type=F32)


def _wdot(a, w_ref):
    return _dot(a, w_ref[...].astype(BF16))


def _wspec_cols(w, layer, tn):
    k = w.shape[-2]
    if w.ndim == 3:
        return pl.BlockSpec((None, k, tn), lambda i, n: (layer, 0, n))
    return pl.BlockSpec((k, tn), lambda i, n: (0, n))


def _sigmoid(x):
    return 0.5 * jnp.tanh(0.5 * x) + 0.5


def _rider_specs(riders, steps_per_batch, n_steps):
    in_specs, out_specs, out_shapes = [], [], []
    for w, layer in riders:
        _, R, N = w.shape
        rb = next(r for r in range(16, R + 1, 16) if R % r == 0 and R // r <= n_steps)
        last = R // rb - 1
        blk = lambda b, c, last=last: jnp.minimum(b * steps_per_batch + c, last)
        in_specs.append(pl.BlockSpec((None, rb, N), lambda b, c, layer=layer, blk=blk: (layer, blk(b, c), 0)))
        out_specs.append(pl.BlockSpec((rb, N), lambda b, c, blk=blk: (blk(b, c), 0)))
        out_shapes.append(jax.ShapeDtypeStruct((R, N), BF16))
    return in_specs, out_specs, out_shapes


def _run_riders(in_refs, out_refs):
    for i_ref, o_ref in zip(in_refs, out_refs):
        o_ref[...] = i_ref[...].astype(BF16)


def _rms_store(x, g_ref, h_scr):
    ms = jnp.mean(x * x, axis=-1, keepdims=True)
    h_scr[...] = (x * lax.rsqrt(ms + RMS_EPS) * g_ref[...]).astype(BF16)


def _rms_to_scratch(x_ref, g_ref, h_scr):
    _rms_store(x_ref[...], g_ref, h_scr)


def _row_split(prompt_rows, sample_rows, tm):
    split = prompt_rows % tm
    assert (prompt_rows + sample_rows) % tm == 0 and tm - split == sample_rows
    return split


def _on_row_tile(split, p_ref, s_ref, fn):
    i = pl.program_id(0)
    last = pl.num_programs(0) - 1

    @pl.when(i < last)
    def _():
        fn(p_ref[...])

    @pl.when(i == last)
    def _():
        fn(jnp.concatenate([p_ref[0:split, :], s_ref[...]], axis=0))


def _hgrn_proj_kernel(layer, split, x_ref, *rest):
    xs_ref, rest = (rest[0], rest[1:]) if split is not None else (None, rest)
    gn_ref, wq_ref, wf_ref, wi_ref, wg_ref, lbl_ref, q_ref, k_ref, lf_ref, v_ref, gt_ref, h_scr = rest

    @pl.when(pl.program_id(1) == 0)
    def _():
        if split is None:
            _rms_to_scratch(x_ref, gn_ref, h_scr)
        else:
            _on_row_tile(split, x_ref, xs_ref, lambda x: _rms_store(x, gn_ref, h_scr))

    h = h_scr[...]
    aq = _wdot(h, wq_ref)
    q_ref[...] = (aq * _sigmoid(aq) * (HEAD_DK ** -0.5)).astype(BF16)

    lg = lbl_ref[...]
    mx = jnp.max(lg, axis=0, keepdims=True)
    e = jnp.exp(lg - mx)
    sm = e / jnp.sum(e, axis=0, keepdims=True)
    cs = sm[0:1]
    for r in range(1, layer + 1):
        cs = cs + sm[r:r + 1]
    lb = cs - sm[0:1]

    z = _wdot(h, wf_ref)
    ez = jnp.exp(-jnp.abs(z))
    r = 1.0 / (1.0 + ez)
    pos = z >= 0.0
    sig_p = jnp.where(pos, r, ez * r)
    sig_n = jnp.where(pos, ez * r, r)
    f = lb + (1.0 - lb) * sig_p
    lf_ref[...] = jnp.log(jnp.maximum(f, MIN_FORGET))
    k_ref[...] = ((1.0 - lb) * sig_n).astype(BF16)

    ag = _wdot(h, wg_ref)
    gt_ref[...] = (ag * _sigmoid(ag)).astype(BF16)
    v_ref[...] = _wdot(h, wi_ref).astype(BF16)


def _hgrn_proj(x, gain, wq, wf, wi, wg, lb_logits, layer, tm, tn):
    N = wq.shape[-1]
    L = lb_logits.shape[0]
    if isinstance(x, tuple):
        xp, xs = x
        D = xp.shape[1]
        M = xp.shape[0] + xs.shape[0]
        split = _row_split(xp.shape[0], xs.shape[0], tm)
        x_args = [xp, xs]
        x_specs = [pl.BlockSpec((tm, D), lambda i, n: (i, 0)), pl.BlockSpec(xs.shape, lambda i, n: (0, 0))]
    else:
        M, D = x.shape
        split, x_args, x_specs = None, [x], [pl.BlockSpec((tm, D), lambda i, n: (i, 0))]
    grid = (M // tm, N // tn)
    wspec = _wspec_cols(wq, layer, tn)
    ospec = pl.BlockSpec((tm, tn), lambda i, n: (i, n))
    return pl.pallas_call(
        functools.partial(_hgrn_proj_kernel, layer, split),
        grid=grid,
        in_specs=x_specs + [
                  pl.BlockSpec((1, D), lambda i, n: (0, 0)),
                  wspec, wspec, wspec, wspec,
                  pl.BlockSpec((L, tn), lambda i, n: (0, n))],
        out_specs=[ospec] * 5,
        out_shape=[jax.ShapeDtypeStruct((M, N), BF16),
                   jax.ShapeDtypeStruct((M, N), BF16),
                   jax.ShapeDtypeStruct((M, N), F32),
                   jax.ShapeDtypeStruct((M, N), BF16),
                   jax.ShapeDtypeStruct((M, N), BF16)],
        scratch_shapes=[pltpu.VMEM((tm, D), BF16)],
        compiler_params=_cparams(("parallel", "arbitrary")),
        name=f"hgrn_proj_{layer}",
    )(*x_args, gain, wq, wf, wi, wg, lb_logits)


def _gla_consts(C):
    nlev = int(math.log2(C))
    tri = np.tril(np.ones((C, C), np.float32))
    ii, jj = np.meshgrid(np.arange(C), np.arange(C), indexing="ij")
    x = ii ^ jj
    lev = np.floor(np.log2(np.maximum(x, 1))).astype(np.int32)
    lmap = np.where(ii == jj, -1, np.where(ii > jj, lev, -2)).astype(np.int32)
    rows = np.arange(C)
    sgn = np.stack([np.where((rows >> l) & 1 == 1, LOG2E if l == 0 else 1.0, 0.0 if l == 0 else -1.0)
                    for l in range(nlev)])
    sgn = np.broadcast_to(sgn[:, :, None], (nlev, C, LANES)).astype(np.float32)
    return jnp.asarray(tri, BF16), jnp.asarray(lmap, jnp.int32), jnp.asarray(sgn), nlev


SUBLANES = 8


def _level_exponent(l, g, g2, gs_ref, sgn):
    C = g.shape[0]
    h = 1 << l
    if l == 0:
        return g * sgn
    sub = lax.broadcasted_iota(jnp.int32, (SUBLANES, LANES), 0)
    bcast = lambda r: jnp.broadcast_to(gs_ref[pl.ds(r, 1), :], (SUBLANES, LANES))
    pieces = []
    for base in range(0, C, SUBLANES):
        if 2 * h >= SUBLANES:
            pieces.append(bcast((base // (2 * h)) * 2 * h + h - 1))
        else:
            p = bcast(base + h - 1)
            for blk in range(1, SUBLANES // (2 * h)):
                p = jnp.where(sub >= blk * 2 * h, bcast(base + blk * 2 * h + h - 1), p)
            pieces.append(p)
    gr = jnp.concatenate(pieces, axis=0)
    return (g2 - gr) * sgn


BF16_ROWS = 16


def _gla_level(l, q, k, qf, kf, g, g2, gs_ref, sgn_ref, lmap_b, acc):
    C = g.shape[0]
    h = 1 << l
    rb = BF16_ROWS
    if h < rb:
        e = jnp.exp2(_level_exponent(l, g, g2, gs_ref, sgn_ref[l]))
        al = _dot_nt((qf * e).astype(BF16), (kf * e).astype(BF16))
        return [jnp.where(lmap_b[i] == l, al[i * rb:(i + 1) * rb], a) for i, a in enumerate(acc)]
    bcast = lambda r: jnp.broadcast_to(gs_ref[pl.ds(r, 1), :], (h, LANES))
    lhs, rhs, upper_blocks = [], [], []
    for b in range(C // h):
        rows = slice(b * h, (b + 1) * h)
        if b % 2 == 1:
            e = jnp.exp2(g2[rows] - bcast(b * h - 1))
            lhs.append((qf[rows] * e).astype(BF16))
            rhs.append(k[rows])
            upper_blocks += list(range(b * h // rb, (b + 1) * h // rb))
        else:
            e = jnp.exp2(bcast((b + 1) * h - 1) - g2[rows])
            rhs.append((kf[rows] * e).astype(BF16))
    al = _dot_nt(jnp.concatenate(lhs, axis=0), jnp.concatenate(rhs, axis=0))
    acc = list(acc)
    for j, i in enumerate(upper_blocks):
        acc[i] = jnp.where(lmap_b[i] == l, al[j * rb:(j + 1) * rb], acc[i])
    return acc


def _column_broadcast(row):
    n = row.shape[1]
    eye = lax.broadcasted_iota(jnp.int32, (n, n), 0) == lax.broadcasted_iota(jnp.int32, (n, n), 1)
    x = jnp.where(eye, jnp.broadcast_to(row, (n, n)), 0.0)
    xh = x.astype(BF16)
    xl = (x - xh.astype(F32)).astype(BF16)
    ones = jnp.ones((n, n), BF16)
    return _dot(xh, ones) + _dot(xl, ones)


def _gla_core(chains, tri, lmap, sgn_ref, nlev, gs_scr, state_transposed=True):
    C = chains[0][0].shape[0]
    rb = BF16_ROWS
    lmap_b = [lmap[i * rb:(i + 1) * rb] for i in range(C // rb)]
    g2s = []
    for u, (q, k, g, v, st) in enumerate(chains):
        ghi = g.astype(BF16)
        glo = (g - ghi.astype(F32)).astype(BF16)
        c2 = _dot(tri, jnp.concatenate([ghi, glo], axis=1))
        g2 = (c2[:, :LANES] + c2[:, LANES:]) * LOG2E
        gs_scr[u] = g2
        g2s.append(g2)
    accs = []
    for (q, k, g, v, st) in chains:
        d = _dot_nt(q, k)
        accs.append([jnp.where(lmap_b[i] == -1, d[i * rb:(i + 1) * rb], 0.0) for i in range(C // rb)])
    qk32 = [(q.astype(F32), k.astype(F32)) for (q, k, g, v, st) in chains]
    for l in range(nlev):
        for u, (q, k, g, v, st) in enumerate(chains):
            qf, kf = qk32[u]
            accs[u] = _gla_level(l, q, k, qf, kf, g, g2s[u], gs_scr.at[u], sgn_ref, lmap_b, accs[u])
    outs = []
    for u, (q, k, g, v, st) in enumerate(chains):
        g2 = g2s[u]
        qf, kf = qk32[u]
        o = _dot(jnp.concatenate(accs[u], axis=0).astype(BF16), v)
        qdec = (qf * jnp.exp2(g2)).astype(BF16)
        glast = g2[C - 1:C, :]
        kdec = (kf * jnp.exp2(glast - g2)).astype(BF16)
        if state_transposed:
            o = o + _dot_nt(qdec, st.astype(BF16))
            st_new = st * jnp.exp2(glast) + _dot_tn(v, kdec)
        else:
            o = o + _dot(qdec, st.astype(BF16))
            st_new = st * _column_broadcast(jnp.exp2(glast)) + _dot_tn(kdec, v)
        outs.append((o, st_new))
    return outs


def _gla_out(o, gon, gate):
    ms = jnp.mean(o * o, axis=-1, keepdims=True)
    return o * lax.rsqrt(ms + RMS_EPS) * gon * gate


def _gla_prompt_kernel(nlev, n_heads, n_riders, q_ref, k_ref, g_ref, v_ref, gt_ref, tri_ref, lmap_ref,
                       sgn_ref, gon_ref, *rest):
    rider_in, rest = rest[:n_riders], rest[n_riders:]
    o_ref, sfin_ref = rest[0], rest[1]
    rider_out = rest[2:2 + n_riders]
    st_scr, gs_scr = rest[2 + n_riders:]
    _run_riders(rider_in, rider_out)
    c = pl.program_id(1)
    hpi = _heads_per_iter(n_heads)
    n_iter = n_heads // hpi

    @pl.when(c == 0)
    def _():
        st_scr[...] = jnp.zeros_like(st_scr)

    def body(j, carry):
        heads = [j * hpi + u for u in range(hpi)]
        lanes = [pl.ds(pl.multiple_of(h * LANES, LANES), LANES) for h in heads]
        chains = [(q_ref[:, hs], k_ref[:, hs], g_ref[:, hs], v_ref[:, hs], st_scr[h])
                  for h, hs in zip(heads, lanes)]
        res = _gla_core(chains, tri_ref[...], lmap_ref[...], sgn_ref, nlev, gs_scr)
        for h, hs, (o, st_new) in zip(heads, lanes, res):
            st_scr[h] = st_new
            o_ref[:, hs] = _gla_out(o, gon_ref[...], gt_ref[:, hs].astype(F32)).astype(BF16)
        return carry

    lax.fori_loop(0, n_iter, body, 0)

    @pl.when(c == pl.num_programs(1) - 1)
    def _():
        def wb(h, carry):
            sfin_ref[0, h] = st_scr[h].T
            return carry
        lax.fori_loop(0, n_heads, wb, 0)


def _gla_prompt(q, k, g, v, gate, gon, batch, seq, riders, name):
    C = GLA_CHUNK
    W = q.shape[1]
    H = W // LANES
    nc = seq // C
    tri, lmap, sgn, nlev = _gla_consts(C)
    rspec = pl.BlockSpec((C, W), lambda b, c: (b * nc + c, 0))
    r_in, r_out, r_shape = _rider_specs(riders, nc, batch * nc)
    return pl.pallas_call(
        functools.partial(_gla_prompt_kernel, nlev, H, len(riders)),
        grid=(batch, nc),
        in_specs=[rspec, rspec, rspec, rspec, rspec,
                  pl.BlockSpec(tri.shape, lambda b, c: (0, 0)),
                  pl.BlockSpec(lmap.shape, lambda b, c: (0, 0)),
                  pl.BlockSpec(sgn.shape, lambda b, c: (0, 0, 0)),
                  pl.BlockSpec((1, LANES), lambda b, c: (0, 0))] + r_in,
        out_specs=[rspec, pl.BlockSpec((1, H, LANES, LANES), lambda b, c: (b, 0, 0, 0))] + r_out,
        out_shape=[jax.ShapeDtypeStruct((batch * seq, W), BF16),
                   jax.ShapeDtypeStruct((batch, H, LANES, LANES), F32)] + r_shape,
        scratch_shapes=[pltpu.VMEM((H, LANES, LANES), F32),
                        pltpu.VMEM((_heads_per_iter(H), C, LANES), F32)],
        compiler_params=_cparams(("arbitrary", "arbitrary")),
        name=name,
    )(q, k, g, v, gate, tri, lmap, sgn, gon, *[w for w, _ in riders])


def _gla_sample_kernel(nlev, n_heads, n_new, layer, creates_states, q_ref, k_ref, g_ref, v_ref, gt_ref,
                       tri_ref, lmap_ref, sgn_ref, gon_ref, s0_ref, *rest):
    if creates_states:
        o_ref, sall_ref, gs_scr = rest
        for other in range(sall_ref.shape[0]):
            if other != layer:
                sall_ref[other] = jnp.zeros(sall_ref.shape[1:], F32)
        sfin_ref = sall_ref.at[layer]
    else:
        _, o_ref, sfin_ref, gs_scr = rest
    nb = SAMPLE_PAD // n_new
    zpad = jnp.zeros((SAMPLE_PAD - n_new, LANES), F32)
    hpi = _heads_per_iter(n_heads)

    def body(j, carry):
        heads = [j * hpi + u for u in range(hpi)]
        lanes = [pl.ds(pl.multiple_of(h * LANES, LANES), LANES) for h in heads]
        chains = []
        for h, hs in zip(heads, lanes):
            q = q_ref[:, hs].astype(F32)
            k = k_ref[:, hs].astype(F32)
            g = g_ref[:, hs]
            v = v_ref[:, hs].astype(F32)
            for bb in range(nb):
                rs = slice(bb * n_new, (bb + 1) * n_new)
                pad = lambda t: jnp.concatenate([t[rs], zpad], axis=0)
                chains.append((pad(q).astype(BF16), pad(k).astype(BF16), pad(g), pad(v).astype(BF16),
                               s0_ref[bb, h]))
        res = _gla_core(chains, tri_ref[...], lmap_ref[...], sgn_ref, nlev, gs_scr, state_transposed=False)
        for u, (h, hs) in enumerate(zip(heads, lanes)):
            gate = gt_ref[:, hs].astype(F32)
            outs = []
            for bb in range(nb):
                o, st_new = res[u * nb + bb]
                sfin_ref[bb, h] = st_new
                outs.append(_gla_out(o[0:n_new], gon_ref[...], gate[bb * n_new:(bb + 1) * n_new]))
            o_ref[:, hs] = jnp.concatenate(outs, axis=0).astype(BF16)
        return carry

    lax.fori_loop(0, n_heads // hpi, body, 0)


def _gla_sample(q, k, g, v, gate, gon, s0, layer, row0, n_seq, n_new, name, states=None):
    W = q.shape[1]
    H = W // LANES
    L = s0.shape[0]
    nb = SAMPLE_PAD // n_new
    tri, lmap, sgn, nlev = _gla_consts(SAMPLE_PAD)
    blk0 = row0 // SAMPLE_PAD
    rspec = pl.BlockSpec((SAMPLE_PAD, W), lambda i: (blk0 + i, 0))
    s0spec = pl.BlockSpec((None, nb, H, LANES, LANES), lambda i: (layer, i, 0, 0, 0))
    in_specs = [rspec, rspec, rspec, rspec, rspec,
                pl.BlockSpec(tri.shape, lambda i: (0, 0)),
                pl.BlockSpec(lmap.shape, lambda i: (0, 0)),
                pl.BlockSpec(sgn.shape, lambda i: (0, 0, 0)),
                pl.BlockSpec((1, LANES), lambda i: (0, 0)),
                s0spec]
    args = [q, k, g, v, gate, tri, lmap, sgn, gon, s0]
    if states is None:
        sspec = pl.BlockSpec((L, nb, H, LANES, LANES), lambda i: (0, i, 0, 0, 0))
        aliases = {}
    else:
        sspec = s0spec
        in_specs.append(pl.BlockSpec(memory_space=pl.ANY))
        args.append(states)
        aliases = {len(args) - 1: 1}
    return pl.pallas_call(
        functools.partial(_gla_sample_kernel, nlev, H, n_new, layer, states is None),
        grid=(n_seq // nb,),
        in_specs=in_specs,
        out_specs=[pl.BlockSpec((SAMPLE_PAD, W), lambda i: (i, 0)), sspec],
        out_shape=[jax.ShapeDtypeStruct((n_seq * n_new, W), BF16),
                   jax.ShapeDtypeStruct((L, n_seq, H, LANES, LANES), F32)],
        input_output_aliases=aliases,
        scratch_shapes=[pltpu.VMEM((_heads_per_iter(H) * nb, SAMPLE_PAD, LANES), F32)],
        compiler_params=_cparams(("parallel",)),
        name=name,
    )(*args)


def _out_proj_kernel(split, x_is_pair, ap_ref, as_ref, w_ref, x_ref, *rest):
    xs_ref, o_ref = (rest[0], rest[1]) if x_is_pair else (None, rest[0])
    i = pl.program_id(0)
    last = pl.num_programs(0) - 1
    tail = lambda p_ref, s_ref: jnp.concatenate([p_ref[0:split, :], s_ref[...]], axis=0)

    @pl.when(i < last)
    def _():
        o_ref[...] = x_ref[...] + _wdot(ap_ref[...], w_ref)

    @pl.when(i == last)
    def _():
        x = tail(x_ref, xs_ref) if x_is_pair else x_ref[...]
        o_ref[...] = x + _wdot(tail(ap_ref, as_ref), w_ref)


def _out_proj(a_prompt, a_sample, w, x, tm, name):
    MP, K = a_prompt.shape
    MS = a_sample.shape[0]
    N = w.shape[1]
    M = MP + MS
    split = _row_split(MP, MS, tm)
    x_is_pair = isinstance(x, tuple)
    if x_is_pair:
        x_args = list(x)
        x_specs = [pl.BlockSpec((tm, N), lambda i: (i, 0)), pl.BlockSpec((MS, N), lambda i: (0, 0))]
    else:
        x_args, x_specs = [x], [pl.BlockSpec((tm, N), lambda i: (i, 0))]
    return pl.pallas_call(
        functools.partial(_out_proj_kernel, split, x_is_pair),
        grid=(M // tm,),
        in_specs=[pl.BlockSpec((tm, K), lambda i: (i, 0)),
                  pl.BlockSpec((MS, K), lambda i: (0, 0)),
                  pl.BlockSpec((K, N), lambda i: (0, 0))] + x_specs,
        out_specs=pl.BlockSpec((tm, N), lambda i: (i, 0)),
        out_shape=jax.ShapeDtypeStruct((M, N), F32),
        compiler_params=_cparams(("parallel",)),
        name=name,
    )(a_prompt, a_sample, w, *x_args)


def _ffn_kernel(split, x_ref, gn_ref, wg_ref, wu_ref, wd_ref, o_ref, *rest):
    ys_ref = rest[0] if split is not None else None
    h_scr = rest[-1]
    f = pl.program_id(1)

    @pl.when(f == 0)
    def _():
        _rms_to_scratch(x_ref, gn_ref, h_scr)
        o_ref[...] = x_ref[...]

    h = h_scr[...]
    g = _wdot(h, wg_ref)
    u = _wdot(h, wu_ref)
    a = (g * _sigmoid(g) * u).astype(BF16)
    o_ref[...] += _wdot(a, wd_ref)

    if split is not None:
        @pl.when((f == pl.num_programs(1) - 1) & (pl.program_id(0) == pl.num_programs(0) - 1))
        def _():
            ys_ref[...] = o_ref[split:, :]


def _ffn(x, gain, wg, wu, wd, tm, tf, name, prompt_rows=None):
    M, D = x.shape
    FF = wg.shape[1]
    o_spec = pl.BlockSpec((tm, D), lambda i, f: (i, 0))
    if prompt_rows is None:
        split, out_specs, out_shape = None, o_spec, jax.ShapeDtypeStruct((M, D), F32)
    else:
        split = prompt_rows % tm
        ms = M - prompt_rows
        assert tm - split == ms
        out_specs = [o_spec, pl.BlockSpec((ms, D), lambda i, f: (0, 0))]
        out_shape = [jax.ShapeDtypeStruct((prompt_rows, D), F32), jax.ShapeDtypeStruct((ms, D), F32)]
    return pl.pallas_call(
        functools.partial(_ffn_kernel, split),
        grid=(M // tm, FF // tf),
        in_specs=[pl.BlockSpec((tm, D), lambda i, f: (i, 0)),
                  pl.BlockSpec((1, D), lambda i, f: (0, 0)),
                  pl.BlockSpec((D, tf), lambda i, f: (0, f)),
                  pl.BlockSpec((D, tf), lambda i, f: (0, f)),
                  pl.BlockSpec((tf, D), lambda i, f: (f, 0))],
        out_specs=out_specs,
        out_shape=out_shape,
        scratch_shapes=[pltpu.VMEM((tm, D), BF16)],
        compiler_params=_cparams(("arbitrary", "arbitrary")),
        name=name,
    )(x, gain, wg, wu, wd)


MXU_COLS = 256


def _head_mean_square(acc):
    n = acc.shape[1]
    r = lax.broadcasted_iota(jnp.int32, (n, n), 0)
    c = lax.broadcasted_iota(jnp.int32, (n, n), 1)
    seg = jnp.where((r >> 6) == (c >> 6), 1.0 / ATTN_HEAD_DIM, 0.0).astype(BF16)
    return _dot((acc * acc).astype(BF16), seg)


def _norm_rope(acc, ms, gain, cos, sin, scale):
    y = acc * lax.rsqrt(ms + RMS_EPS) * gain
    lane = lax.broadcasted_iota(jnp.int32, y.shape, 1)
    half = ATTN_HEAD_DIM // 2
    rot = jnp.where((lane & half) == 0, pltpu.roll(y, LANES - half, 1), pltpu.roll(y, half, 1))
    out = y * cos + rot * sin
    return out * scale if scale != 1.0 else out


def _rope_cols(h, w_ref, hg_ref, cos_ref, sin_ref, o_ref, tn, scale):
    for n in range(w_ref.shape[1] // tn):
        acc = _dot(h, w_ref[:, n * tn:(n + 1) * tn])
        for c2 in range(tn // MXU_COLS):
            a2 = acc[:, c2 * MXU_COLS:(c2 + 1) * MXU_COLS]
            ms2 = _head_mean_square(a2)
            for c in range(MXU_COLS // LANES):
                cs = slice(c * LANES, (c + 1) * LANES)
                col = n * tn + c2 * MXU_COLS + c * LANES
                o_ref[:, col:col + LANES] = _norm_rope(a2[:, cs], ms2[:, cs], hg_ref[...], cos_ref[...],
                                                       sin_ref[...], scale).astype(BF16)


def _q_proj_kernel(tn, x_ref, gn_ref, w_ref, hg_ref, cos_ref, sin_ref, q_ref, h_scr):
    _rms_to_scratch(x_ref, gn_ref, h_scr)
    _rope_cols(h_scr[...], w_ref, hg_ref, cos_ref, sin_ref, q_ref, tn, ATTN_SCALE * LOG2E)


def _q_proj(x, gain, w, head_gain, cos, sin, tm, tn, name):
    M, D = x.shape
    N = w.shape[1]
    return pl.pallas_call(
        functools.partial(_q_proj_kernel, tn),
        grid=(M // tm,),
        in_specs=[pl.BlockSpec((tm, D), lambda i: (i, 0)),
                  pl.BlockSpec((1, D), lambda i: (0, 0)),
                  pl.BlockSpec((D, N), lambda i: (0, 0)),
                  pl.BlockSpec((1, LANES), lambda i: (0, 0)),
                  pl.BlockSpec((tm, LANES), lambda i: (i, 0)),
                  pl.BlockSpec((tm, LANES), lambda i: (i, 0))],
        out_specs=pl.BlockSpec((tm, N), lambda i: (i, 0)),
        out_shape=jax.ShapeDtypeStruct((M, N), BF16),
        scratch_shapes=[pltpu.VMEM((tm, D), BF16)],
        compiler_params=_cparams(("parallel",)),
        name=name,
    )(x, gain, w, head_gain, cos, sin)


def _kv_proj_kernel(tn, x_ref, gn_ref, wk_ref, wv_ref, hg_ref, cos_ref, sin_ref, k_ref, v_ref, h_scr):
    _rms_to_scratch(x_ref, gn_ref, h_scr)
    h = h_scr[...]
    _rope_cols(h, wk_ref, hg_ref, cos_ref, sin_ref, k_ref, tn, 1.0)
    for n in range(wv_ref.shape[1] // tn):
        ns = slice(n * tn, (n + 1) * tn)
        v_ref[:, ns] = _dot(h, wv_ref[:, ns]).astype(BF16)


def _kv_proj(x, gain, wk, wv, head_gain, cos, sin, tm, tn):
    M, D = x.shape
    N = wk.shape[1]
    wspec = pl.BlockSpec((D, N), lambda i: (0, 0))
    ospec = pl.BlockSpec((tm, N), lambda i: (i, 0))
    return pl.pallas_call(
        functools.partial(_kv_proj_kernel, tn),
        grid=(M // tm,),
        in_specs=[pl.BlockSpec((tm, D), lambda i: (i, 0)),
                  pl.BlockSpec((1, D), lambda i: (0, 0)),
                  wspec, wspec,
                  pl.BlockSpec((1, LANES), lambda i: (0, 0)),
                  pl.BlockSpec((tm, LANES), lambda i: (i, 0)),
                  pl.BlockSpec((tm, LANES), lambda i: (i, 0))],
        out_specs=[ospec, ospec],
        out_shape=[jax.ShapeDtypeStruct((M, N), BF16), jax.ShapeDtypeStruct((M, N), BF16)],
        scratch_shapes=[pltpu.VMEM((tm, D), BF16)],
        compiler_params=_cparams(("parallel",)),
        name="kv_proj",
    )(x, gain, wk, wv, head_gain, cos, sin)


def _softmax_sink(s_parts, sink2):
    m = sink2
    for s in s_parts:
        m = jnp.maximum(m, jnp.max(s, axis=-1, keepdims=True))
    den = jnp.exp2(sink2 - m)
    ps = []
    for s in s_parts:
        p = jnp.exp2(s - m)
        den = den + jnp.sum(p, axis=-1, keepdims=True)
        ps.append(p.astype(BF16))
    return ps, den


def _pv(ps, v_parts, den):
    o = None
    for p, v in zip(ps, v_parts):
        pv = _dot(p, v)
        o = pv if o is None else o + pv
    return o / den


def _swa_prompt_kernel(n_riders, sk_ref, q_ref, kp_ref, kc_ref, vp_ref, vc_ref, *rest):
    rider_in, o_ref, rider_out = rest[:n_riders], rest[n_riders], rest[n_riders + 1:]
    _run_riders(rider_in, rider_out)
    i = pl.program_id(1)
    W = WINDOW
    row = lax.broadcasted_iota(jnp.int32, (W, W), 0)
    col = lax.broadcasted_iota(jnp.int32, (W, W), 1)
    mask_prev = col > row + jnp.where(i > 0, 0, W)
    mask_cur = col <= row
    lo = lax.broadcasted_iota(jnp.int32, (W, LANES), 1) < ATTN_HEAD_DIM
    sink_col = col == 0
    first_row = lax.broadcasted_iota(jnp.int32, (W, LANES), 0) == 0

    def scores(g):
        gs = slice(g * LANES, (g + 1) * LANES)
        kd = jnp.concatenate([kp_ref[:, gs], kc_ref[:, gs]], axis=0)
        lhs = []
        for c in range(GQA_GROUP // 2):
            qc = q_ref[:, (2 * g + c) * LANES:(2 * g + c + 1) * LANES]
            zero = jnp.zeros_like(qc)
            lhs += [jnp.where(lo, qc, zero), jnp.where(lo, zero, qc)]
        s = _dot_nt(jnp.concatenate(lhs, axis=0), kd)
        out = []
        for a in range(GQA_GROUP):
            sa = s[a * W:(a + 1) * W]
            fill = jnp.where(sink_col, sk_ref[GQA_GROUP * g + a] * LOG2E, MASK_VALUE)
            out.append(jnp.concatenate([jnp.where(mask_prev, sa[:, :W], fill),
                                        jnp.where(mask_cur, sa[:, W:], MASK_VALUE)], axis=1))
        return out

    def finish(g, s_heads):
        gs = slice(g * LANES, (g + 1) * LANES)
        vprev = vp_ref[:, gs]
        vd = jnp.concatenate([jnp.where(first_row, jnp.zeros_like(vprev), vprev), vc_ref[:, gs]], axis=0)
        ps, dens = [], []
        for s in s_heads:
            m = jnp.max(s, axis=-1, keepdims=True)
            p = jnp.exp2(s - m)
            dens.append(jnp.sum(p, axis=-1, keepdims=True))
            ps.append(p.astype(BF16))
        o = _dot(jnp.concatenate(ps, axis=0), vd)
        for c in range(GQA_GROUP // 2):
            cs = slice((2 * g + c) * LANES, (2 * g + c + 1) * LANES)
            o0 = o[(2 * c) * W:(2 * c + 1) * W] / dens[2 * c]
            o1 = o[(2 * c + 1) * W:(2 * c + 2) * W] / dens[2 * c + 1]
            o_ref[:, cs] = jnp.where(lo, o0, o1).astype(BF16)

    s_next = scores(0)
    for g in range(ATTN_KV_HEADS):
        s_cur = s_next
        if g + 1 < ATTN_KV_HEADS:
            s_next = scores(g + 1)
        finish(g, s_cur)


def _swa_prompt(q, kdup, vdup, sinks, batch, seq, riders, name):
    W = WINDOW
    nb = seq // W
    QW = q.shape[1]
    KW = kdup.shape[1]
    prev = lambda b, i: (b * nb + jnp.maximum(i - 1, 0), 0)
    cur = lambda b, i: (b * nb + i, 0)
    r_in, r_out, r_shape = _rider_specs(riders, nb, batch * nb)
    return pl.pallas_call(
        functools.partial(_swa_prompt_kernel, len(riders)),
        grid=(batch, nb),
        in_specs=[pl.BlockSpec(memory_space=pltpu.SMEM),
                  pl.BlockSpec((W, QW), cur),
                  pl.BlockSpec((W, KW), prev), pl.BlockSpec((W, KW), cur),
                  pl.BlockSpec((W, KW), prev), pl.BlockSpec((W, KW), cur)] + r_in,
        out_specs=[pl.BlockSpec((W, QW), cur)] + r_out,
        out_shape=[jax.ShapeDtypeStruct((batch * seq, QW), BF16)] + r_shape,
        compiler_params=_cparams(("arbitrary", "arbitrary")),
        name=name,
    )(sinks, q, kdup, kdup, vdup, vdup, *[w for w, _ in riders])


def _swa_sample_kernel(n_new, sk_ref, q_ref, kn_ref, vn_ref, kc_ref, vc_ref, o_ref):
    W = WINDOW
    nb = SAMPLE_PAD // n_new
    rows = GQA_GROUP * n_new
    qi = lax.broadcasted_iota(jnp.int32, (rows, W), 0) & (n_new - 1)
    mask_c = lax.broadcasted_iota(jnp.int32, (rows, W), 1) > qi
    qi_n = lax.broadcasted_iota(jnp.int32, (rows, SAMPLE_PAD), 0) & (n_new - 1)
    mask_n = lax.broadcasted_iota(jnp.int32, (rows, SAMPLE_PAD), 1) <= qi_n
    hrow = lax.broadcasted_iota(jnp.int32, (rows, 1), 0) // n_new
    lo = lax.broadcasted_iota(jnp.int32, (n_new, LANES), 1) < ATTN_HEAD_DIM
    zq = jnp.zeros((n_new, LANES), F32)
    zpad = jnp.zeros((SAMPLE_PAD - n_new, LANES), F32)
    q = q_ref[...].astype(F32)
    kn = kn_ref[...].astype(F32)
    vn = vn_ref[...].astype(F32)
    chains = [(bb, g) for bb in range(nb) for g in range(ATTN_KV_HEADS)]
    sinks2 = []
    for g in range(ATTN_KV_HEADS):
        sink = jnp.zeros((rows, 1), F32)
        for a in range(GQA_GROUP):
            sink = jnp.where(hrow == a, sk_ref[GQA_GROUP * g + a] * LOG2E, sink)
        sinks2.append(sink)
    scores, values = [], []
    for bb, g in chains:
        rs = slice(bb * n_new, (bb + 1) * n_new)
        gs = slice(g * LANES, (g + 1) * LANES)
        kn_g = jnp.concatenate([kn[rs, gs], zpad], axis=0).astype(BF16)
        vn_g = jnp.concatenate([vn[rs, gs], zpad], axis=0).astype(BF16)
        lhs = []
        for a in range(GQA_GROUP):
            c, par = divmod(a, 2)
            qc = q[rs, (2 * g + c) * LANES:(2 * g + c + 1) * LANES]
            lhs.append(jnp.where(lo, qc, zq) if par == 0 else jnp.where(lo, zq, qc))
        lhs = jnp.concatenate(lhs, axis=0).astype(BF16)
        s_c = jnp.where(mask_c, _dot_nt(lhs, kc_ref[bb, :, gs]), MASK_VALUE)
        s_n = jnp.where(mask_n, _dot_nt(lhs, kn_g), MASK_VALUE)
        scores.append([s_c, s_n])
        values.append([vc_ref[bb, :, gs], vn_g])
    probs = [_softmax_sink(s, sinks2[g]) for s, (bb, g) in zip(scores, chains)]
    outs = [_pv(ps, v, den) for (ps, den), v in zip(probs, values)]
    out_rows = []
    for bb in range(nb):
        chunks = []
        for g in range(ATTN_KV_HEADS):
            o = outs[bb * ATTN_KV_HEADS + g]
            for c in range(GQA_GROUP // 2):
                chunks.append(jnp.where(lo, o[(2 * c) * n_new:(2 * c + 1) * n_new],
                                        o[(2 * c + 1) * n_new:(2 * c + 2) * n_new]))
        out_rows.append(jnp.concatenate(chunks, axis=1))
    o_ref[...] = jnp.concatenate(out_rows, axis=0).astype(BF16)


def _swa_sample(q, kdup, vdup, cache_kdup, cache_vdup, sinks, row0, n_seq, n_new, name):
    W = WINDOW
    QW = q.shape[1]
    KW = kdup.shape[1]
    nb = SAMPLE_PAD // n_new
    blk0 = row0 // SAMPLE_PAD
    rmap = lambda i: (blk0 + i, 0)
    cspec = pl.BlockSpec((nb, W, KW), lambda i: (i, 0, 0))
    return pl.pallas_call(
        functools.partial(_swa_sample_kernel, n_new),
        grid=(n_seq // nb,),
        in_specs=[pl.BlockSpec(memory_space=pltpu.SMEM),
                  pl.BlockSpec((SAMPLE_PAD, QW), rmap),
                  pl.BlockSpec((SAMPLE_PAD, KW), rmap), pl.BlockSpec((SAMPLE_PAD, KW), rmap),
                  cspec, cspec],
        out_specs=pl.BlockSpec((SAMPLE_PAD, QW), lambda i: (i, 0)),
        out_shape=jax.ShapeDtypeStruct((n_seq * n_new, QW), BF16),
        compiler_params=_cparams(("parallel",)),
        name=name,
    )(sinks, q, kdup, vdup, cache_kdup, cache_vdup)


def _dup_heads(w):
    lead = w.shape[:-1]
    w3 = w.reshape(lead + (ATTN_KV_HEADS, ATTN_HEAD_DIM))
    return jnp.concatenate([w3, w3], axis=-1).reshape(lead + (ATTN_KV_HEADS * 2 * ATTN_HEAD_DIM,))


def _undup_heads(a, lead):
    return a.reshape(lead + (ATTN_KV_HEADS, 2 * ATTN_HEAD_DIM))[..., :ATTN_HEAD_DIM].astype(F32)


def _rope_tables(pos):
    half = ATTN_HEAD_DIM // 2
    inv = ROPE_THETA ** (-jnp.arange(half, dtype=F32) / half)
    ang = pos.astype(F32)[:, None] * inv[None, :]
    cos = jnp.cos(ang)
    sin = jnp.sin(ang)
    reps = LANES // ATTN_HEAD_DIM
    cos_t = jnp.tile(jnp.concatenate([cos, cos], axis=-1), (1, reps))
    sin_t = jnp.tile(jnp.concatenate([-sin, sin], axis=-1), (1, reps))
    return cos_t, sin_t


def kernel(x_prompt, x_sample, state_hgrn, cache_k_win, cache_v_win, hgrn_norm, hgrn_wq, hgrn_wf, hgrn_wi, hgrn_wg, hgrn_lb_logits, hgrn_onorm, hgrn_wo, kv_norm, w_k, w_v, k_norm, attn_norm, attn_wq, q_norm, sinks, attn_wo, ffn_norm, w_gate, w_up, w_down):
    B, T, D = x_prompt.shape
    SB, S, _ = x_sample.shape
    MP = B * T
    MS = SB * S
    n_a = hgrn_wq.shape[0]
    n_b = attn_wq.shape[0]
    assert SAMPLE_PAD % S == 0 and SB % (SAMPLE_PAD // S) == 0 and MP % SAMPLE_PAD == 0
    assert T % GLA_CHUNK == 0 and T % WINDOW == 0

    x = (x_prompt.reshape(MP, D).astype(F32), x_sample.reshape(MS, D).astype(F32))
    M = MP + MS
    tm = 768 if M % 768 == 0 else 256
    tn = 512
    tn_f32 = 256
    tn_rope = 512
    tf = 512

    bf = lambda w: w.astype(BF16)
    f32 = lambda w: w.astype(F32)
    row = lambda g: g.reshape(1, -1).astype(F32)
    pos = jnp.concatenate([jnp.tile(jnp.arange(T), B), jnp.tile(PAST_LEN + jnp.arange(S), SB)])
    cos_t, sin_t = _rope_tables(pos)
    head_gain = lambda g: jnp.tile(g.astype(F32), LANES // ATTN_HEAD_DIM).reshape(1, LANES)

    w_gate32, w_up32, w_down32 = f32(w_gate), f32(w_up), f32(w_down)
    hgrn_w32 = [f32(hgrn_wq), f32(hgrn_wf), f32(hgrn_wi), f32(hgrn_wg)]
    hgrn_wo32, attn_wq32, attn_wo32 = f32(hgrn_wo), f32(attn_wq), f32(attn_wo)
    ffn_riders = lambda l: [(w_gate32, l), (w_up32, l), (w_down32, l)]

    st_prompt, st_sample = [], None
    s0 = f32(state_hgrn)
    proj_w = hgrn_w32
    attn_wq_bf = None
    for l in range(n_a):
        q, k, lf, v, gate = _hgrn_proj(x, row(hgrn_norm[l]), *proj_w, f32(hgrn_lb_logits), l, tm,
                                       tn_f32 if proj_w[0].ndim == 3 else tn)
        gon = row(hgrn_onorm[l])
        riders = ffn_riders(l) + [(hgrn_wo32, l)]
        if l + 1 < n_a:
            riders += [(w, l + 1) for w in hgrn_w32]
        elif n_b > 0:
            riders += [(attn_wq32, 0)]
        o_p, s_p, wg_bf, wu_bf, wd_bf, wo_bf, *nxt = _gla_prompt(q, k, lf, v, gate, gon, B, T, riders,
                                                                 f"gla_prompt_{l}")
        if l + 1 < n_a:
            proj_w = nxt
        elif n_b > 0:
            attn_wq_bf = nxt[0]
        o_s, st_sample = _gla_sample(q, k, lf, v, gate, gon, s0, l, MP, SB, S, f"gla_sample_{l}",
                                     states=st_sample)
        st_prompt.append(s_p)
        x = _out_proj(o_p, o_s, wo_bf, x, tm, f"hgrn_out_{l}")
        x = _ffn(x, row(ffn_norm[l]), wg_bf, wu_bf, wd_bf, tm, tf, f"ffn_{l}")

    kdup, vdup = _kv_proj(x, row(kv_norm), bf(_dup_heads(w_k)), bf(_dup_heads(w_v)), head_gain(k_norm),
                          cos_t, sin_t, tm, tn_rope)
    KW = ATTN_KV_HEADS * ATTN_HEAD_DIM
    cache_kdup = bf(_dup_heads(cache_k_win.reshape(SB, WINDOW, KW)))
    cache_vdup = bf(_dup_heads(cache_v_win.reshape(SB, WINDOW, KW)))
    for j in range(n_b):
        l = n_a + j
        if attn_wq_bf is None:
            attn_wq_bf = bf(attn_wq[j])
        q = _q_proj(x, row(attn_norm[j]), attn_wq_bf, head_gain(q_norm[j]), cos_t, sin_t, tm, tn_rope,
                    f"q_proj_{j}")
        sk = sinks[j].astype(F32)
        riders = ffn_riders(l) + [(attn_wo32, j)] + ([(attn_wq32, j + 1)] if j + 1 < n_b else [])
        a_p, wg_bf, wu_bf, wd_bf, wo_bf, *nxt = _swa_prompt(q, kdup, vdup, sk, B, T, riders, f"swa_prompt_{j}")
        attn_wq_bf = nxt[0] if nxt else None
        a_s = _swa_sample(q, kdup, vdup, cache_kdup, cache_vdup, sk, MP, SB, S, f"swa_sample_{j}")
        x = _out_proj(a_p, a_s, wo_bf, x, tm, f"attn_out_{j}")
        last = j == n_b - 1
        x = _ffn(x, row(ffn_norm[l]), wg_bf, wu_bf, wd_bf, tm, tf, f"ffn_{l}",
                 prompt_rows=MP if last else None)

    y_prompt = x[0].reshape(B, T, D)
    y_sample = x[1].reshape(SB, S, D)
    k_win_p = _undup_heads(kdup[:MP].reshape(B, T, -1)[:, T - WINDOW:], (B, WINDOW))
    v_win_p = _undup_heads(vdup[:MP].reshape(B, T, -1)[:, T - WINDOW:], (B, WINDOW))
    k_new_s = _undup_heads(kdup[MP:].reshape(SB, S, -1), (SB, S))
    v_new_s = _undup_heads(vdup[MP:].reshape(SB, S, -1), (SB, S))
    k_win_s = jnp.concatenate([cache_k_win[:, S:].astype(F32), k_new_s], axis=1)
    v_win_s = jnp.concatenate([cache_v_win[:, S:].astype(F32), v_new_s], axis=1)
    return (y_prompt, y_sample, jnp.stack(st_prompt), st_sample,
            k_win_p, v_win_p, k_win_s, v_win_s)
```

```python
import functools
import math

import numpy as np
import jax
import jax.numpy as jnp
from jax import lax
from jax.experimental import pallas as pl
from jax.experimental.pallas import tpu as pltpu

HGRN_HEADS = 16
HEAD_DK = 128
ATTN_HEAD_DIM = 64
ATTN_KV_HEADS = 8
GQA_GROUP = 4
WINDOW = 128
ROPE_THETA = 10000.0
ATTN_SCALE = ATTN_HEAD_DIM ** -0.5
LOG2E = math.log2(math.e)
RMS_EPS = 1e-6
MASK_VALUE = -1e30
MIN_FORGET = 1e-30
PAST_LEN = 16384

LANES = 128
VMEM_LIMIT_BYTES = 56 * 1024 * 1024

GLA_CHUNK = 128
SAMPLE_PAD = 16
GLA_HEADS_PER_ITER = 16


def _heads_per_iter(n_heads):
    return math.gcd(n_heads, GLA_HEADS_PER_ITER)

F32 = jnp.float32
BF16 = jnp.bfloat16


def _cparams(sem):
    return pltpu.CompilerParams(dimension_semantics=sem, vmem_limit_bytes=VMEM_LIMIT_BYTES)


def _dot(a, b):
    return jnp.dot(a, b, preferred_element_type=F32)


def _dot_nt(a, b):
    return lax.dot_general(a, b, (((1,), (1,)), ((), ())), preferred_element_type=F32)


def _dot_tn(a, b):
    return lax.dot_general(a, b, (((0,), (0,)), ((), ())), preferred_element_type=F32)


def _wdot(a, w_ref):
    return _dot(a, w_ref[...].astype(BF16))


def _wspec_cols(w, layer, tn):
    k = w.shape[-2]
    if w.ndim == 3:
        return pl.BlockSpec((None, k, tn), lambda i, n: (layer, 0, n))
    return pl.BlockSpec((k, tn), lambda i, n: (0, n))


def _sigmoid(x):
    return 0.5 * jnp.tanh(0.5 * x) + 0.5


def _rider_specs(riders, steps_per_batch, n_steps):
    in_specs, out_specs, out_shapes = [], [], []
    for w, layer in riders:
        _, R, N = w.shape
        rb = next(r for r in range(16, R + 1, 16) if R % r == 0 and R // r <= n_steps)
        last = R // rb - 1
        blk = lambda b, c, last=last: jnp.minimum(b * steps_per_batch + c, last)
        in_specs.append(pl.BlockSpec((None, rb, N), lambda b, c, layer=layer, blk=blk: (layer, blk(b, c), 0)))
        out_specs.append(pl.BlockSpec((rb, N), lambda b, c, blk=blk: (blk(b, c), 0)))
        out_shapes.append(jax.ShapeDtypeStruct((R, N), BF16))
    return in_specs, out_specs, out_shapes


def _run_riders(in_refs, out_refs):
    for i_ref, o_ref in zip(in_refs, out_refs):
        o_ref[...] = i_ref[...].astype(BF16)


def _rms_store(x, g_ref, h_scr):
    ms = jnp.mean(x * x, axis=-1, keepdims=True)
    h_scr[...] = (x * lax.rsqrt(ms + RMS_EPS) * g_ref[...]).astype(BF16)


def _rms_to_scratch(x_ref, g_ref, h_scr):
    _rms_store(x_ref[...], g_ref, h_scr)


def _row_split(prompt_rows, sample_rows, tm):
    split = prompt_rows % tm
    assert (prompt_rows + sample_rows) % tm == 0 and tm - split == sample_rows
    return split


def _on_row_tile(split, p_ref, s_ref, fn):
    i = pl.program_id(0)
    last = pl.num_programs(0) - 1

    @pl.when(i < last)
    def _():
        fn(p_ref[...])

    @pl.when(i == last)
    def _():
        fn(jnp.concatenate([p_ref[0:split, :], s_ref[...]], axis=0))


def _hgrn_proj_kernel(layer, split, x_ref, *rest):
    xs_ref, rest = (rest[0], rest[1:]) if split is not None else (None, rest)
    gn_ref, wq_ref, wf_ref, wi_ref, wg_ref, lbl_ref, q_ref, k_ref, lf_ref, v_ref, gt_ref, h_scr = rest

    @pl.when(pl.program_id(1) == 0)
    def _():
        if split is None:
            _rms_to_scratch(x_ref, gn_ref, h_scr)
        else:
            _on_row_tile(split, x_ref, xs_ref, lambda x: _rms_store(x, gn_ref, h_scr))

    h = h_scr[...]
    aq = _wdot(h, wq_ref)
    q_ref[...] = (aq * _sigmoid(aq) * (HEAD_DK ** -0.5)).astype(BF16)

    lg = lbl_ref[...]
    mx = jnp.max(lg, axis=0, keepdims=True)
    e = jnp.exp(lg - mx)
    sm = e / jnp.sum(e, axis=0, keepdims=True)
    cs = sm[0:1]
    for r in range(1, layer + 1):
        cs = cs + sm[r:r + 1]
    lb = cs - sm[0:1]

    z = _wdot(h, wf_ref)
    ez = jnp.exp(-jnp.abs(z))
    r = 1.0 / (1.0 + ez)
    pos = z >= 0.0
    sig_p = jnp.where(pos, r, ez * r)
    sig_n = jnp.where(pos, ez * r, r)
    f = lb + (1.0 - lb) * sig_p
    lf_ref[...] = jnp.log(jnp.maximum(f, MIN_FORGET))
    k_ref[...] = ((1.0 - lb) * sig_n).astype(BF16)

    ag = _wdot(h, wg_ref)
    gt_ref[...] = (ag * _sigmoid(ag)).astype(BF16)
    v_ref[...] = _wdot(h, wi_ref).astype(BF16)


def _hgrn_proj(x, gain, wq, wf, wi, wg, lb_logits, layer, tm, tn):
    N = wq.shape[-1]
    L = lb_logits.shape[0]
    if isinstance(x, tuple):
        xp, xs = x
        D = xp.shape[1]
        M = xp.shape[0] + xs.shape[0]
        split = _row_split(xp.shape[0], xs.shape[0], tm)
        x_args = [xp, xs]
        x_specs = [pl.BlockSpec((tm, D), lambda i, n: (i, 0)), pl.BlockSpec(xs.shape, lambda i, n: (0, 0))]
    else:
        M, D = x.shape
        split, x_args, x_specs = None, [x], [pl.BlockSpec((tm, D), lambda i, n: (i, 0))]
    grid = (M // tm, N // tn)
    wspec = _wspec_cols(wq, layer, tn)
    ospec = pl.BlockSpec((tm, tn), lambda i, n: (i, n))
    return pl.pallas_call(
        functools.partial(_hgrn_proj_kernel, layer, split),
        grid=grid,
        in_specs=x_specs + [
                  pl.BlockSpec((1, D), lambda i, n: (0, 0)),
                  wspec, wspec, wspec, wspec,
                  pl.BlockSpec((L, tn), lambda i, n: (0, n))],
        out_specs=[ospec] * 5,
        out_shape=[jax.ShapeDtypeStruct((M, N), BF16),
                   jax.ShapeDtypeStruct((M, N), BF16),
                   jax.ShapeDtypeStruct((M, N), F32),
                   jax.ShapeDtypeStruct((M, N), BF16),
                   jax.ShapeDtypeStruct((M, N), BF16)],
        scratch_shapes=[pltpu.VMEM((tm, D), BF16)],
        compiler_params=_cparams(("parallel", "arbitrary")),
        name=f"hgrn_proj_{layer}",
    )(*x_args, gain, wq, wf, wi, wg, lb_logits)


def _gla_consts(C):
    nlev = int(math.log2(C))
    tri = np.tril(np.ones((C, C), np.float32))
    ii, jj = np.meshgrid(np.arange(C), np.arange(C), indexing="ij")
    x = ii ^ jj
    lev = np.floor(np.log2(np.maximum(x, 1))).astype(np.int32)
    lmap = np.where(ii == jj, -1, np.where(ii > jj, lev, -2)).astype(np.int32)
    rows = np.arange(C)
    sgn = np.stack([np.where((rows >> l) & 1 == 1, LOG2E if l == 0 else 1.0, 0.0 if l == 0 else -1.0)
                    for l in range(nlev)])
    sgn = np.broadcast_to(sgn[:, :, None], (nlev, C, LANES)).astype(np.float32)
    return jnp.asarray(tri, BF16), jnp.asarray(lmap, jnp.int32), jnp.asarray(sgn), nlev


SUBLANES = 8


def _level_exponent(l, g, g2, gs_ref, sgn):
    C = g.shape[0]
    h = 1 << l
    if l == 0:
        return g * sgn
    sub = lax.broadcasted_iota(jnp.int32, (SUBLANES, LANES), 0)
    bcast = lambda r: jnp.broadcast_to(gs_ref[pl.ds(r, 1), :], (SUBLANES, LANES))
    pieces = []
    for base in range(0, C, SUBLANES):
        if 2 * h >= SUBLANES:
            pieces.append(bcast((base // (2 * h)) * 2 * h + h - 1))
        else:
            p = bcast(base + h - 1)
            for blk in range(1, SUBLANES // (2 * h)):
                p = jnp.where(sub >= blk * 2 * h, bcast(base + blk * 2 * h + h - 1), p)
            pieces.append(p)
    gr = jnp.concatenate(pieces, axis=0)
    return (g2 - gr) * sgn


BF16_ROWS = 16


def _gla_level(l, q, k, qf, kf, g, g2, gs_ref, sgn_ref, lmap_b, acc):
    C = g.shape[0]
    h = 1 << l
    rb = BF16_ROWS
    if h < rb:
        e = jnp.exp2(_level_exponent(l, g, g2, gs_ref, sgn_ref[l]))
        al = _dot_nt((qf * e).astype(BF16), (kf * e).astype(BF16))
        return [jnp.where(lmap_b[i] == l, al[i * rb:(i + 1) * rb], a) for i, a in enumerate(acc)]
    bcast = lambda r: jnp.broadcast_to(gs_ref[pl.ds(r, 1), :], (h, LANES))
    lhs, rhs, upper_blocks = [], [], []
    for b in range(C // h):
        rows = slice(b * h, (b + 1) * h)
        if b % 2 == 1:
            e = jnp.exp2(g2[rows] - bcast(b * h - 1))
            lhs.append((qf[rows] * e).astype(BF16))
            rhs.append(k[rows])
            upper_blocks += list(range(b * h // rb, (b + 1) * h // rb))
        else:
            e = jnp.exp2(bcast((b + 1) * h - 1) - g2[rows])
            rhs.append((kf[rows] * e).astype(BF16))
    al = _dot_nt(jnp.concatenate(lhs, axis=0), jnp.concatenate(rhs, axis=0))
    acc = list(acc)
    for j, i in enumerate(upper_blocks):
        acc[i] = jnp.where(lmap_b[i] == l, al[j * rb:(j + 1) * rb], acc[i])
    return acc


def _column_broadcast(row):
    n = row.shape[1]
    eye = lax.broadcasted_iota(jnp.int32, (n, n), 0) == lax.broadcasted_iota(jnp.int32, (n, n), 1)
    x = jnp.where(eye, jnp.broadcast_to(row, (n, n)), 0.0)
    xh = x.astype(BF16)
    xl = (x - xh.astype(F32)).astype(BF16)
    ones = jnp.ones((n, n), BF16)
    return _dot(xh, ones) + _dot(xl, ones)


def _gla_core(chains, tri, lmap, sgn_ref, nlev, gs_scr, state_transposed=True):
    C = chains[0][0].shape[0]
    rb = BF16_ROWS
    lmap_b = [lmap[i * rb:(i + 1) * rb] for i in range(C // rb)]
    g2s = []
    for u, (q, k, g, v, st) in enumerate(chains):
        ghi = g.astype(BF16)
        glo = (g - ghi.astype(F32)).astype(BF16)
        c2 = _dot(tri, jnp.concatenate([ghi, glo], axis=1))
        g2 = (c2[:, :LANES] + c2[:, LANES:]) * LOG2E
        gs_scr[u] = g2
        g2s.append(g2)
    accs = []
    for (q, k, g, v, st) in chains:
        d = _dot_nt(q, k)
        accs.append([jnp.where(lmap_b[i] == -1, d[i * rb:(i + 1) * rb], 0.0) for i in range(C // rb)])
    qk32 = [(q.astype(F32), k.astype(F32)) for (q, k, g, v, st) in chains]
    for l in range(nlev):
        for u, (q, k, g, v, st) in enumerate(chains):
            qf, kf = qk32[u]
            accs[u] = _gla_level(l, q, k, qf, kf, g, g2s[u], gs_scr.at[u], sgn_ref, lmap_b, accs[u])
    outs = []
    for u, (q, k, g, v, st) in enumerate(chains):
        g2 = g2s[u]
        qf, kf = qk32[u]
        o = _dot(jnp.concatenate(accs[u], axis=0).astype(BF16), v)
        qdec = (qf * jnp.exp2(g2)).astype(BF16)
        glast = g2[C - 1:C, :]
        kdec = (kf * jnp.exp2(glast - g2)).astype(BF16)
        if state_transposed:
            o = o + _dot_nt(qdec, st.astype(BF16))
            st_new = st * jnp.exp2(glast) + _dot_tn(v, kdec)
        else:
            o = o + _dot(qdec, st.astype(BF16))
            st_new = st * _column_broadcast(jnp.exp2(glast)) + _dot_tn(kdec, v)
        outs.append((o, st_new))
    return outs


def _gla_out(o, gon, gate):
    ms = jnp.mean(o * o, axis=-1, keepdims=True)
    return o * lax.rsqrt(ms + RMS_EPS) * gon * gate


def _gla_prompt_kernel(nlev, n_heads, n_riders, q_ref, k_ref, g_ref, v_ref, gt_ref, tri_ref, lmap_ref,
                       sgn_ref, gon_ref, *rest):
    rider_in, rest = rest[:n_riders], rest[n_riders:]
    o_ref, sfin_ref = rest[0], rest[1]
    rider_out = rest[2:2 + n_riders]
    st_scr, gs_scr = rest[2 + n_riders:]
    _run_riders(rider_in, rider_out)
    c = pl.program_id(1)
    hpi = _heads_per_iter(n_heads)
    n_iter = n_heads // hpi

    @pl.when(c == 0)
    def _():
        st_scr[...] = jnp.zeros_like(st_scr)

    def body(j, carry):
        heads = [j * hpi + u for u in range(hpi)]
        lanes = [pl.ds(pl.multiple_of(h * LANES, LANES), LANES) for h in heads]
        chains = [(q_ref[:, hs], k_ref[:, hs], g_ref[:, hs], v_ref[:, hs], st_scr[h])
                  for h, hs in zip(heads, lanes)]
        res = _gla_core(chains, tri_ref[...], lmap_ref[...], sgn_ref, nlev, gs_scr)
        for h, hs, (o, st_new) in zip(heads, lanes, res):
            st_scr[h] = st_new
            o_ref[:, hs] = _gla_out(o, gon_ref[...], gt_ref[:, hs].astype(F32)).astype(BF16)
        return carry

    lax.fori_loop(0, n_iter, body, 0)

    @pl.when(c == pl.num_programs(1) - 1)
    def _():
        def wb(h, carry):
            sfin_ref[0, h] = st_scr[h].T
            return carry
        lax.fori_loop(0, n_heads, wb, 0)


def _gla_prompt(q, k, g, v, gate, gon, batch, seq, riders, name):
    C = GLA_CHUNK
    W = q.shape[1]
    H = W // LANES
    nc = seq // C
    tri, lmap, sgn, nlev = _gla_consts(C)
    rspec = pl.BlockSpec((C, W), lambda b, c: (b * nc + c, 0))
    r_in, r_out, r_shape = _rider_specs(riders, nc, batch * nc)
    return pl.pallas_call(
        functools.partial(_gla_prompt_kernel, nlev, H, len(riders)),
        grid=(batch, nc),
        in_specs=[rspec, rspec, rspec, rspec, rspec,
                  pl.BlockSpec(tri.shape, lambda b, c: (0, 0)),
                  pl.BlockSpec(lmap.shape, lambda b, c: (0, 0)),
                  pl.BlockSpec(sgn.shape, lambda b, c: (0, 0, 0)),
                  pl.BlockSpec((1, LANES), lambda b, c: (0, 0))] + r_in,
        out_specs=[rspec, pl.BlockSpec((1, H, LANES, LANES), lambda b, c: (b, 0, 0, 0))] + r_out,
        out_shape=[jax.ShapeDtypeStruct((batch * seq, W), BF16),
                   jax.ShapeDtypeStruct((batch, H, LANES, LANES), F32)] + r_shape,
        scratch_shapes=[pltpu.VMEM((H, LANES, LANES), F32),
                        pltpu.VMEM((_heads_per_iter(H), C, LANES), F32)],
        compiler_params=_cparams(("arbitrary", "arbitrary")),
        name=name,
    )(q, k, g, v, gate, tri, lmap, sgn, gon, *[w for w, _ in riders])


def _gla_sample_kernel(nlev, n_heads, n_new, layer, creates_states, q_ref, k_ref, g_ref, v_ref, gt_ref,
                       tri_ref, lmap_ref, sgn_ref, gon_ref, s0_ref, *rest):
    if creates_states:
        o_ref, sall_ref, gs_scr = rest
        for other in range(sall_ref.shape[0]):
            if other != layer:
                sall_ref[other] = jnp.zeros(sall_ref.shape[1:], F32)
        sfin_ref = sall_ref.at[layer]
    else:
        _, o_ref, sfin_ref, gs_scr = rest
    nb = SAMPLE_PAD // n_new
    zpad = jnp.zeros((SAMPLE_PAD - n_new, LANES), F32)
    hpi = _heads_per_iter(n_heads)

    def body(j, carry):
        heads = [j * hpi + u for u in range(hpi)]
        lanes = [pl.ds(pl.multiple_of(h * LANES, LANES), LANES) for h in heads]
        chains = []
        for h, hs in zip(heads, lanes):
            q = q_ref[:, hs].astype(F32)
            k = k_ref[:, hs].astype(F32)
            g = g_ref[:, hs]
            v = v_ref[:, hs].astype(F32)
            for bb in range(nb):
                rs = slice(bb * n_new, (bb + 1) * n_new)
                pad = lambda t: jnp.concatenate([t[rs], zpad], axis=0)
                chains.append((pad(q).astype(BF16), pad(k).astype(BF16), pad(g), pad(v).astype(BF16),
                               s0_ref[bb, h]))
        res = _gla_core(chains, tri_ref[...], lmap_ref[...], sgn_ref, nlev, gs_scr, state_transposed=False)
        for u, (h, hs) in enumerate(zip(heads, lanes)):
            gate = gt_ref[:, hs].astype(F32)
            outs = []
            for bb in range(nb):
                o, st_new = res[u * nb + bb]
                sfin_ref[bb, h] = st_new
                outs.append(_gla_out(o[0:n_new], gon_ref[...], gate[bb * n_new:(bb + 1) * n_new]))
            o_ref[:, hs] = jnp.concatenate(outs, axis=0).astype(BF16)
        return carry

    lax.fori_loop(0, n_heads // hpi, body, 0)


def _gla_sample(q, k, g, v, gate, gon, s0, layer, row0, n_seq, n_new, name, states=None):
    W = q.shape[1]
    H = W // LANES
    L = s0.shape[0]
    nb = SAMPLE_PAD // n_new
    tri, lmap, sgn, nlev = _gla_consts(SAMPLE_PAD)
    blk0 = row0 // SAMPLE_PAD
    rspec = pl.BlockSpec((SAMPLE_PAD, W), lambda i: (blk0 + i, 0))
    s0spec = pl.BlockSpec((None, nb, H, LANES, LANES), lambda i: (layer, i, 0, 0, 0))
    in_specs = [rspec, rspec, rspec, rspec, rspec,
                pl.BlockSpec(tri.shape, lambda i: (0, 0)),
                pl.BlockSpec(lmap.shape, lambda i: (0, 0)),
                pl.BlockSpec(sgn.shape, lambda i: (0, 0, 0)),
                pl.BlockSpec((1, LANES), lambda i: (0, 0)),
                s0spec]
    args = [q, k, g, v, gate, tri, lmap, sgn, gon, s0]
    if states is None:
        sspec = pl.BlockSpec((L, nb, H, LANES, LANES), lambda i: (0, i, 0, 0, 0))
        aliases = {}
    else:
        sspec = s0spec
        in_specs.append(pl.BlockSpec(memory_space=pl.ANY))
        args.append(states)
        aliases = {len(args) - 1: 1}
    return pl.pallas_call(
        functools.partial(_gla_sample_kernel, nlev, H, n_new, layer, states is None),
        grid=(n_seq // nb,),
        in_specs=in_specs,
        out_specs=[pl.BlockSpec((SAMPLE_PAD, W), lambda i: (i, 0)), sspec],
        out_shape=[jax.ShapeDtypeStruct((n_seq * n_new, W), BF16),
                   jax.ShapeDtypeStruct((L, n_seq, H, LANES, LANES), F32)],
        input_output_aliases=aliases,
        scratch_shapes=[pltpu.VMEM((_heads_per_iter(H) * nb, SAMPLE_PAD, LANES), F32)],
        compiler_params=_cparams(("parallel",)),
        name=name,
    )(*args)


def _out_proj_kernel(split, x_is_pair, ap_ref, as_ref, w_ref, x_ref, *rest):
    xs_ref, o_ref = (rest[0], rest[1]) if x_is_pair else (None, rest[0])
    i = pl.program_id(0)
    last = pl.num_programs(0) - 1
    tail = lambda p_ref, s_ref: jnp.concatenate([p_ref[0:split, :], s_ref[...]], axis=0)

    @pl.when(i < last)
    def _():
        o_ref[...] = x_ref[...] + _wdot(ap_ref[...], w_ref)

    @pl.when(i == last)
    def _():
        x = tail(x_ref, xs_ref) if x_is_pair else x_ref[...]
        o_ref[...] = x + _wdot(tail(ap_ref, as_ref), w_ref)


def _out_proj(a_prompt, a_sample, w, x, tm, name):
    MP, K = a_prompt.shape
    MS = a_sample.shape[0]
    N = w.shape[1]
    M = MP + MS
    split = _row_split(MP, MS, tm)
    x_is_pair = isinstance(x, tuple)
    if x_is_pair:
        x_args = list(x)
        x_specs = [pl.BlockSpec((tm, N), lambda i: (i, 0)), pl.BlockSpec((MS, N), lambda i: (0, 0))]
    else:
        x_args, x_specs = [x], [pl.BlockSpec((tm, N), lambda i: (i, 0))]
    return pl.pallas_call(
        functools.partial(_out_proj_kernel, split, x_is_pair),
        grid=(M // tm,),
        in_specs=[pl.BlockSpec((tm, K), lambda i: (i, 0)),
                  pl.BlockSpec((MS, K), lambda i: (0, 0)),
                  pl.BlockSpec((K, N), lambda i: (0, 0))] + x_specs,
        out_specs=pl.BlockSpec((tm, N), lambda i: (i, 0)),
        out_shape=jax.ShapeDtypeStruct((M, N), F32),
        compiler_params=_cparams(("parallel",)),
        name=name,
    )(a_prompt, a_sample, w, *x_args)


def _ffn_kernel(split, n_riders, x_ref, gn_ref, wg_ref, wu_ref, wd_ref, *rest):
    rider_in, rest = rest[:n_riders], rest[n_riders:]
    o_ref = rest[0]
    ys_ref = rest[1] if split is not None else None
    n_out = 1 if split is None else 2
    rider_out = rest[n_out:n_out + n_riders]
    h_scr = rest[-1]
    _run_riders(rider_in, rider_out)
    f = pl.program_id(1)

    @pl.when(f == 0)
    def _():
        _rms_to_scratch(x_ref, gn_ref, h_scr)
        o_ref[...] = x_ref[...]

    h = h_scr[...]
    g = _wdot(h, wg_ref)
    u = _wdot(h, wu_ref)
    a = (g * _sigmoid(g) * u).astype(BF16)
    o_ref[...] += _wdot(a, wd_ref)

    if split is not None:
        @pl.when((f == pl.num_programs(1) - 1) & (pl.program_id(0) == pl.num_programs(0) - 1))
        def _():
            ys_ref[...] = o_ref[split:, :]


def _ffn(x, gain, wg, wu, wd, tm, tf, name, prompt_rows=None, riders=()):
    M, D = x.shape
    FF = wg.shape[1]
    nf = FF // tf
    o_spec = pl.BlockSpec((tm, D), lambda i, f: (i, 0))
    r_in, r_out, r_shape = _rider_specs(list(riders), nf, (M // tm) * nf)
    if prompt_rows is None:
        split, out_specs, out_shape = None, [o_spec], [jax.ShapeDtypeStruct((M, D), F32)]
    else:
        split = prompt_rows % tm
        ms = M - prompt_rows
        assert tm - split == ms
        out_specs = [o_spec, pl.BlockSpec((ms, D), lambda i, f: (0, 0))]
        out_shape = [jax.ShapeDtypeStruct((prompt_rows, D), F32), jax.ShapeDtypeStruct((ms, D), F32)]
    return pl.pallas_call(
        functools.partial(_ffn_kernel, split, len(riders)),
        grid=(M // tm, nf),
        in_specs=[pl.BlockSpec((tm, D), lambda i, f: (i, 0)),
                  pl.BlockSpec((1, D), lambda i, f: (0, 0)),
                  pl.BlockSpec((D, tf), lambda i, f: (0, f)),
                  pl.BlockSpec((D, tf), lambda i, f: (0, f)),
                  pl.BlockSpec((tf, D), lambda i, f: (f, 0))] + r_in,
        out_specs=out_specs + r_out,
        out_shape=out_shape + r_shape,
        scratch_shapes=[pltpu.VMEM((tm, D), BF16)],
        compiler_params=_cparams(("arbitrary", "arbitrary")),
        name=name,
    )(x, gain, wg, wu, wd, *[w for w, _ in riders])


MXU_COLS = 256


def _head_mean_square(acc):
    n = acc.shape[1]
    r = lax.broadcasted_iota(jnp.int32, (n, n), 0)
    c = lax.broadcasted_iota(jnp.int32, (n, n), 1)
    seg = jnp.where((r >> 6) == (c >> 6), 1.0 / ATTN_HEAD_DIM, 0.0).astype(BF16)
    return _dot((acc * acc).astype(BF16), seg)


def _norm_rope(acc, ms, gain, cos, sin, scale):
    y = acc * lax.rsqrt(ms + RMS_EPS) * gain
    lane = lax.broadcasted_iota(jnp.int32, y.shape, 1)
    half = ATTN_HEAD_DIM // 2
    rot = jnp.where((lane & half) == 0, pltpu.roll(y, LANES - half, 1), pltpu.roll(y, half, 1))
    out = y * cos + rot * sin
    return out * scale if scale != 1.0 else out


def _rope_cols(h, w_ref, hg_ref, cos_ref, sin_ref, o_ref, tn, scale):
    for n in range(w_ref.shape[1] // tn):
        acc = _dot(h, w_ref[:, n * tn:(n + 1) * tn])
        for c2 in range(tn // MXU_COLS):
            a2 = acc[:, c2 * MXU_COLS:(c2 + 1) * MXU_COLS]
            ms2 = _head_mean_square(a2)
            for c in range(MXU_COLS // LANES):
                cs = slice(c * LANES, (c + 1) * LANES)
                col = n * tn + c2 * MXU_COLS + c * LANES
                o_ref[:, col:col + LANES] = _norm_rope(a2[:, cs], ms2[:, cs], hg_ref[...], cos_ref[...],
                                                       sin_ref[...], scale).astype(BF16)


def _q_proj_kernel(tn, x_ref, gn_ref, w_ref, hg_ref, cos_ref, sin_ref, q_ref, h_scr):
    _rms_to_scratch(x_ref, gn_ref, h_scr)
    _rope_cols(h_scr[...], w_ref, hg_ref, cos_ref, sin_ref, q_ref, tn, ATTN_SCALE * LOG2E)


def _q_proj(x, gain, w, head_gain, cos, sin, tm, tn, name):
    M, D = x.shape
    N = w.shape[1]
    return pl.pallas_call(
        functools.partial(_q_proj_kernel, tn),
        grid=(M // tm,),
        in_specs=[pl.BlockSpec((tm, D), lambda i: (i, 0)),
                  pl.BlockSpec((1, D), lambda i: (0, 0)),
                  pl.BlockSpec((D, N), lambda i: (0, 0)),
                  pl.BlockSpec((1, LANES), lambda i: (0, 0)),
                  pl.BlockSpec((tm, LANES), lambda i: (i, 0)),
                  pl.BlockSpec((tm, LANES), lambda i: (i, 0))],
        out_specs=pl.BlockSpec((tm, N), lambda i: (i, 0)),
        out_shape=jax.ShapeDtypeStruct((M, N), BF16),
        scratch_shapes=[pltpu.VMEM((tm, D), BF16)],
        compiler_params=_cparams(("parallel",)),
        name=name,
    )(x, gain, w, head_gain, cos, sin)


def _kv_proj_kernel(tn, x_ref, gn_ref, wk_ref, wv_ref, hg_ref, cos_ref, sin_ref, k_ref, v_ref, h_scr):
    _rms_to_scratch(x_ref, gn_ref, h_scr)
    h = h_scr[...]
    _rope_cols(h, wk_ref, hg_ref, cos_ref, sin_ref, k_ref, tn, 1.0)
    for n in range(wv_ref.shape[1] // tn):
        ns = slice(n * tn, (n + 1) * tn)
        v_ref[:, ns] = _dot(h, wv_ref[:, ns]).astype(BF16)


def _kv_proj(x, gain, wk, wv, head_gain, cos, sin, tm, tn):
    M, D = x.shape
    N = wk.shape[1]
    wspec = pl.BlockSpec((D, N), lambda i: (0, 0))
    ospec = pl.BlockSpec((tm, N), lambda i: (i, 0))
    return pl.pallas_call(
        functools.partial(_kv_proj_kernel, tn),
        grid=(M // tm,),
        in_specs=[pl.BlockSpec((tm, D), lambda i: (i, 0)),
                  pl.BlockSpec((1, D), lambda i: (0, 0)),
                  wspec, wspec,
                  pl.BlockSpec((1, LANES), lambda i: (0, 0)),
                  pl.BlockSpec((tm, LANES), lambda i: (i, 0)),
                  pl.BlockSpec((tm, LANES), lambda i: (i, 0))],
        out_specs=[ospec, ospec],
        out_shape=[jax.ShapeDtypeStruct((M, N), BF16), jax.ShapeDtypeStruct((M, N), BF16)],
        scratch_shapes=[pltpu.VMEM((tm, D), BF16)],
        compiler_params=_cparams(("parallel",)),
        name="kv_proj",
    )(x, gain, wk, wv, head_gain, cos, sin)


def _softmax_sink(s_parts, sink2):
    m = sink2
    for s in s_parts:
        m = jnp.maximum(m, jnp.max(s, axis=-1, keepdims=True))
    den = jnp.exp2(sink2 - m)
    ps = []
    for s in s_parts:
        p = jnp.exp2(s - m)
        den = den + jnp.sum(p, axis=-1, keepdims=True)
        ps.append(p.astype(BF16))
    return ps, den


def _pv(ps, v_parts, den):
    o = None
    for p, v in zip(ps, v_parts):
        pv = _dot(p, v)
        o = pv if o is None else o + pv
    return o / den


def _swa_prompt_kernel(n_riders, sk_ref, q_ref, kp_ref, kc_ref, vp_ref, vc_ref, *rest):
    rider_in, o_ref, rider_out = rest[:n_riders], rest[n_riders], rest[n_riders + 1:]
    _run_riders(rider_in, rider_out)
    i = pl.program_id(1)
    W = WINDOW
    row = lax.broadcasted_iota(jnp.int32, (W, W), 0)
    col = lax.broadcasted_iota(jnp.int32, (W, W), 1)
    mask_prev = col > row + jnp.where(i > 0, 0, W)
    mask_cur = col <= row
    lo = lax.broadcasted_iota(jnp.int32, (W, LANES), 1) < ATTN_HEAD_DIM
    sink_col = col == 0
    first_row = lax.broadcasted_iota(jnp.int32, (W, LANES), 0) == 0

    def scores(g):
        gs = slice(g * LANES, (g + 1) * LANES)
        kd = jnp.concatenate([kp_ref[:, gs], kc_ref[:, gs]], axis=0)
        lhs = []
        for c in range(GQA_GROUP // 2):
            qc = q_ref[:, (2 * g + c) * LANES:(2 * g + c + 1) * LANES]
            zero = jnp.zeros_like(qc)
            lhs += [jnp.where(lo, qc, zero), jnp.where(lo, zero, qc)]
        s = _dot_nt(jnp.concatenate(lhs, axis=0), kd)
        out = []
        for a in range(GQA_GROUP):
            sa = s[a * W:(a + 1) * W]
            fill = jnp.where(sink_col, sk_ref[GQA_GROUP * g + a] * LOG2E, MASK_VALUE)
            out.append(jnp.concatenate([jnp.where(mask_prev, sa[:, :W], fill),
                                        jnp.where(mask_cur, sa[:, W:], MASK_VALUE)], axis=1))
        return out

    def finish(g, s_heads):
        gs = slice(g * LANES, (g + 1) * LANES)
        vprev = vp_ref[:, gs]
        vd = jnp.concatenate([jnp.where(first_row, jnp.zeros_like(vprev), vprev), vc_ref[:, gs]], axis=0)
        ps, dens = [], []
        for s in s_heads:
            m = jnp.max(s, axis=-1, keepdims=True)
            p = jnp.exp2(s - m)
            dens.append(jnp.sum(p, axis=-1, keepdims=True))
            ps.append(p.astype(BF16))
        o = _dot(jnp.concatenate(ps, axis=0), vd)
        for c in range(GQA_GROUP // 2):
            cs = slice((2 * g + c) * LANES, (2 * g + c + 1) * LANES)
            o0 = o[(2 * c) * W:(2 * c + 1) * W] / dens[2 * c]
            o1 = o[(2 * c + 1) * W:(2 * c + 2) * W] / dens[2 * c + 1]
            o_ref[:, cs] = jnp.where(lo, o0, o1).astype(BF16)

    s_next = scores(0)
    for g in range(ATTN_KV_HEADS):
        s_cur = s_next
        if g + 1 < ATTN_KV_HEADS:
            s_next = scores(g + 1)
        finish(g, s_cur)


def _swa_prompt(q, kdup, vdup, sinks, batch, seq, riders, name):
    W = WINDOW
    nb = seq // W
    QW = q.shape[1]
    KW = kdup.shape[1]
    prev = lambda b, i: (b * nb + jnp.maximum(i - 1, 0), 0)
    cur = lambda b, i: (b * nb + i, 0)
    r_in, r_out, r_shape = _rider_specs(riders, nb, batch * nb)
    return pl.pallas_call(
        functools.partial(_swa_prompt_kernel, len(riders)),
        grid=(batch, nb),
        in_specs=[pl.BlockSpec(memory_space=pltpu.SMEM),
                  pl.BlockSpec((W, QW), cur),
                  pl.BlockSpec((W, KW), prev), pl.BlockSpec((W, KW), cur),
                  pl.BlockSpec((W, KW), prev), pl.BlockSpec((W, KW), cur)] + r_in,
        out_specs=[pl.BlockSpec((W, QW), cur)] + r_out,
        out_shape=[jax.ShapeDtypeStruct((batch * seq, QW), BF16)] + r_shape,
        compiler_params=_cparams(("arbitrary", "arbitrary")),
        name=name,
    )(sinks, q, kdup, kdup, vdup, vdup, *[w for w, _ in riders])


def _swa_sample_kernel(n_new, sk_ref, q_ref, kn_ref, vn_ref, kc_ref, vc_ref, o_ref):
    W = WINDOW
    nb = SAMPLE_PAD // n_new
    rows = GQA_GROUP * n_new
    qi = lax.broadcasted_iota(jnp.int32, (rows, W), 0) & (n_new - 1)
    mask_c = lax.broadcasted_iota(jnp.int32, (rows, W), 1) > qi
    qi_n = lax.broadcasted_iota(jnp.int32, (rows, SAMPLE_PAD), 0) & (n_new - 1)
    mask_n = lax.broadcasted_iota(jnp.int32, (rows, SAMPLE_PAD), 1) <= qi_n
    hrow = lax.broadcasted_iota(jnp.int32, (rows, 1), 0) // n_new
    lo = lax.broadcasted_iota(jnp.int32, (n_new, LANES), 1) < ATTN_HEAD_DIM
    zq = jnp.zeros((n_new, LANES), F32)
    zpad = jnp.zeros((SAMPLE_PAD - n_new, LANES), F32)
    q = q_ref[...].astype(F32)
    kn = kn_ref[...].astype(F32)
    vn = vn_ref[...].astype(F32)
    chains = [(bb, g) for bb in range(nb) for g in range(ATTN_KV_HEADS)]
    sinks2 = []
    for g in range(ATTN_KV_HEADS):
        sink = jnp.zeros((rows, 1), F32)
        for a in range(GQA_GROUP):
            sink = jnp.where(hrow == a, sk_ref[GQA_GROUP * g + a] * LOG2E, sink)
        sinks2.append(sink)
    scores, values = [], []
    for bb, g in chains:
        rs = slice(bb * n_new, (bb + 1) * n_new)
        gs = slice(g * LANES, (g + 1) * LANES)
        kn_g = jnp.concatenate([kn[rs, gs], zpad], axis=0).astype(BF16)
        vn_g = jnp.concatenate([vn[rs, gs], zpad], axis=0).astype(BF16)
        lhs = []
        for a in range(GQA_GROUP):
            c, par = divmod(a, 2)
            qc = q[rs, (2 * g + c) * LANES:(2 * g + c + 1) * LANES]
            lhs.append(jnp.where(lo, qc, zq) if par == 0 else jnp.where(lo, zq, qc))
        lhs = jnp.concatenate(lhs, axis=0).astype(BF16)
        s_c = jnp.where(mask_c, _dot_nt(lhs, kc_ref[bb, :, gs]), MASK_VALUE)
        s_n = jnp.where(mask_n, _dot_nt(lhs, kn_g), MASK_VALUE)
        scores.append([s_c, s_n])
        values.append([vc_ref[bb, :, gs], vn_g])
    probs = [_softmax_sink(s, sinks2[g]) for s, (bb, g) in zip(scores, chains)]
    outs = [_pv(ps, v, den) for (ps, den), v in zip(probs, values)]
    out_rows = []
    for bb in range(nb):
        chunks = []
        for g in range(ATTN_KV_HEADS):
            o = outs[bb * ATTN_KV_HEADS + g]
            for c in range(GQA_GROUP // 2):
                chunks.append(jnp.where(lo, o[(2 * c) * n_new:(2 * c + 1) * n_new],
                                        o[(2 * c + 1) * n_new:(2 * c + 2) * n_new]))
        out_rows.append(jnp.concatenate(chunks, axis=1))
    o_ref[...] = jnp.concatenate(out_rows, axis=0).astype(BF16)


def _swa_sample(q, kdup, vdup, cache_kdup, cache_vdup, sinks, row0, n_seq, n_new, name):
    W = WINDOW
    QW = q.shape[1]
    KW = kdup.shape[1]
    nb = SAMPLE_PAD // n_new
    blk0 = row0 // SAMPLE_PAD
    rmap = lambda i: (blk0 + i, 0)
    cspec = pl.BlockSpec((nb, W, KW), lambda i: (i, 0, 0))
    return pl.pallas_call(
        functools.partial(_swa_sample_kernel, n_new),
        grid=(n_seq // nb,),
        in_specs=[pl.BlockSpec(memory_space=pltpu.SMEM),
                  pl.BlockSpec((SAMPLE_PAD, QW), rmap),
                  pl.BlockSpec((SAMPLE_PAD, KW), rmap), pl.BlockSpec((SAMPLE_PAD, KW), rmap),
                  cspec, cspec],
        out_specs=pl.BlockSpec((SAMPLE_PAD, QW), lambda i: (i, 0)),
        out_shape=jax.ShapeDtypeStruct((n_seq * n_new, QW), BF16),
        compiler_params=_cparams(("parallel",)),
        name=name,
    )(sinks, q, kdup, vdup, cache_kdup, cache_vdup)


def _dup_heads(w):
    lead = w.shape[:-1]
    w3 = w.reshape(lead + (ATTN_KV_HEADS, ATTN_HEAD_DIM))
    return jnp.concatenate([w3, w3], axis=-1).reshape(lead + (ATTN_KV_HEADS * 2 * ATTN_HEAD_DIM,))


def _undup_heads(a, lead):
    return a.reshape(lead + (ATTN_KV_HEADS, 2 * ATTN_HEAD_DIM))[..., :ATTN_HEAD_DIM].astype(F32)


def _rope_tables(pos):
    half = ATTN_HEAD_DIM // 2
    inv = ROPE_THETA ** (-jnp.arange(half, dtype=F32) / half)
    ang = pos.astype(F32)[:, None] * inv[None, :]
    cos = jnp.cos(ang)
    sin = jnp.sin(ang)
    reps = LANES // ATTN_HEAD_DIM
    cos_t = jnp.tile(jnp.concatenate([cos, cos], axis=-1), (1, reps))
    sin_t = jnp.tile(jnp.concatenate([-sin, sin], axis=-1), (1, reps))
    return cos_t, sin_t


def kernel(x_prompt, x_sample, state_hgrn, cache_k_win, cache_v_win, hgrn_norm, hgrn_wq, hgrn_wf, hgrn_wi, hgrn_wg, hgrn_lb_logits, hgrn_onorm, hgrn_wo, kv_norm, w_k, w_v, k_norm, attn_norm, attn_wq, q_norm, sinks, attn_wo, ffn_norm, w_gate, w_up, w_down):
    B, T, D = x_prompt.shape
    SB, S, _ = x_sample.shape
    MP = B * T
    MS = SB * S
    n_a = hgrn_wq.shape[0]
    n_b = attn_wq.shape[0]
    assert SAMPLE_PAD % S == 0 and SB % (SAMPLE_PAD // S) == 0 and MP % SAMPLE_PAD == 0
    assert T % GLA_CHUNK == 0 and T % WINDOW == 0

    x = (x_prompt.reshape(MP, D).astype(F32), x_sample.reshape(MS, D).astype(F32))
    M = MP + MS
    tm = 768 if M % 768 == 0 else 256
    tn = 512
    tn_f32 = 256
    tf = 512

    bf = lambda w: w.astype(BF16)
    f32 = lambda w: w.astype(F32)
    row = lambda g: g.reshape(1, -1).astype(F32)
    pos = jnp.concatenate([jnp.tile(jnp.arange(T), B), jnp.tile(PAST_LEN + jnp.arange(S), SB)])
    cos_t, sin_t = _rope_tables(pos)
    head_gain = lambda g: jnp.tile(g.astype(F32), LANES // ATTN_HEAD_DIM).reshape(1, LANES)

    w_gate32, w_up32, w_down32 = f32(w_gate), f32(w_up), f32(w_down)
    hgrn_w32 = [f32(hgrn_wq), f32(hgrn_wf), f32(hgrn_wi), f32(hgrn_wg)]
    hgrn_wo32, attn_wq32, attn_wo32 = f32(hgrn_wo), f32(attn_wq), f32(attn_wo)
    ffn_riders = lambda l: [(w_gate32, l), (w_up32, l), (w_down32, l)]

    depth = n_a + n_b
    ffn_w = None
    st_prompt, st_sample = [], None
    s0 = f32(state_hgrn)
    proj_w = hgrn_w32
    attn_wq_bf = None
    for l in range(n_a):
        q, k, lf, v, gate = _hgrn_proj(x, row(hgrn_norm[l]), *proj_w, f32(hgrn_lb_logits), l, tm,
                                       tn_f32 if proj_w[0].ndim == 3 else tn)
        gon = row(hgrn_onorm[l])
        riders = [(hgrn_wo32, l)] + (ffn_riders(l) if ffn_w is None else [])
        if l + 1 < n_a:
            riders += [(w, l + 1) for w in hgrn_w32]
        elif n_b > 0:
            riders += [(attn_wq32, 0)]
        o_p, s_p, wo_bf, *cast = _gla_prompt(q, k, lf, v, gate, gon, B, T, riders, f"gla_prompt_{l}")
        if ffn_w is None:
            ffn_w, cast = cast[:3], cast[3:]
        if l + 1 < n_a:
            proj_w = cast
        elif n_b > 0:
            attn_wq_bf = cast[0]
        o_s, st_sample = _gla_sample(q, k, lf, v, gate, gon, s0, l, MP, SB, S, f"gla_sample_{l}",
                                     states=st_sample)
        st_prompt.append(s_p)
        x = _out_proj(o_p, o_s, wo_bf, x, tm, f"hgrn_out_{l}")
        x, *ffn_next = _ffn(x, row(ffn_norm[l]), *ffn_w, tm, tf, f"ffn_{l}",
                            riders=ffn_riders(l + 1) if l + 1 < depth else ())
        ffn_w = ffn_next

    kdup, vdup = _kv_proj(x, row(kv_norm), bf(_dup_heads(w_k)), bf(_dup_heads(w_v)), head_gain(k_norm),
                          cos_t, sin_t, tm, tn)
    KW = ATTN_KV_HEADS * ATTN_HEAD_DIM
    cache_kdup = bf(_dup_heads(cache_k_win.reshape(SB, WINDOW, KW)))
    cache_vdup = bf(_dup_heads(cache_v_win.reshape(SB, WINDOW, KW)))
    for j in range(n_b):
        l = n_a + j
        if attn_wq_bf is None:
            attn_wq_bf = bf(attn_wq[j])
        q = _q_proj(x, row(attn_norm[j]), attn_wq_bf, head_gain(q_norm[j]), cos_t, sin_t, tm, tn,
                    f"q_proj_{j}")
        sk = sinks[j].astype(F32)
        riders = [(attn_wo32, j)] + (ffn_riders(l) if ffn_w is None else []) + (
            [(attn_wq32, j + 1)] if j + 1 < n_b else [])
        a_p, wo_bf, *cast = _swa_prompt(q, kdup, vdup, sk, B, T, riders, f"swa_prompt_{j}")
        if ffn_w is None:
            ffn_w, cast = cast[:3], cast[3:]
        attn_wq_bf = cast[0] if cast else None
        a_s = _swa_sample(q, kdup, vdup, cache_kdup, cache_vdup, sk, MP, SB, S, f"swa_sample_{j}")
        x = _out_proj(a_p, a_s, wo_bf, x, tm, f"attn_out_{j}")
        last = j == n_b - 1
        res = _ffn(x, row(ffn_norm[l]), *ffn_w, tm, tf, f"ffn_{l}", prompt_rows=MP if last else None,
                   riders=ffn_riders(l + 1) if l + 1 < depth else ())
        if last:
            x, ffn_w = res[:2], res[2:]
        else:
            x, ffn_w = res[0], res[1:]

    y_prompt = x[0].reshape(B, T, D)
    y_sample = x[1].reshape(SB, S, D)
    k_win_p = _undup_heads(kdup[:MP].reshape(B, T, -1)[:, T - WINDOW:], (B, WINDOW))
    v_win_p = _undup_heads(vdup[:MP].reshape(B, T, -1)[:, T - WINDOW:], (B, WINDOW))
    k_new_s = _undup_heads(kdup[MP:].reshape(SB, S, -1), (SB, S))
    v_new_s = _undup_heads(vdup[MP:].reshape(SB, S, -1), (SB, S))
    k_win_s = jnp.concatenate([cache_k_win[:, S:].astype(F32), k_new_s], axis=1)
    v_win_s = jnp.concatenate([cache_v_win[:, S:].astype(F32), v_new_s], axis=1)
    return (y_prompt, y_sample, jnp.stack(st_prompt), st_sample,
            k_win_p, v_win_p, k_win_s, v_win_s)
```

```python
import functools
import math

import numpy as np
import jax
import jax.numpy as jnp
from jax import lax
from jax.experimental import pallas as pl
from jax.experimental.pallas import tpu as pltpu

HGRN_HEADS = 16
HEAD_DK = 128
ATTN_HEAD_DIM = 64
ATTN_KV_HEADS = 8
GQA_GROUP = 4
WINDOW = 128
ROPE_THETA = 10000.0
ATTN_SCALE = ATTN_HEAD_DIM ** -0.5
LOG2E = math.log2(math.e)
RMS_EPS = 1e-6
MASK_VALUE = -1e30
MIN_FORGET = 1e-30
PAST_LEN = 16384

LANES = 128
VMEM_LIMIT_BYTES = 56 * 1024 * 1024

GLA_CHUNK = 128
SAMPLE_PAD = 16
GLA_HEADS_PER_ITER = 16


def _heads_per_iter(n_heads):
    return math.gcd(n_heads, GLA_HEADS_PER_ITER)

F32 = jnp.float32
BF16 = jnp.bfloat16


def _cparams(sem):
    return pltpu.CompilerParams(dimension_semantics=sem, vmem_limit_bytes=VMEM_LIMIT_BYTES)


def _dot(a, b):
    return jnp.dot(a, b, preferred_element_type=F32)


def _dot_nt(a, b):
    return lax.dot_general(a, b, (((1,), (1,)), ((), ())), preferred_element_type=F32)


def _dot_tn(a, b):
    return lax.dot_general(a, b, (((0,), (0,)), ((), ())), preferred_element_type=F32)


def _wdot(a, w_ref):
    return _dot(a, w_ref[...].astype(BF16))


def _wspec_cols(w, layer, tn):
    k = w.shape[-2]
    if w.ndim == 3:
        return pl.BlockSpec((None, k, tn), lambda i, n: (layer, 0, n))
    return pl.BlockSpec((k, tn), lambda i, n: (0, n))


def _sigmoid(x):
    return 0.5 * jnp.tanh(0.5 * x) + 0.5


def _rider_specs(riders, steps_per_batch, n_steps):
    in_specs, out_specs, out_shapes = [], [], []
    for w, layer in riders:
        _, R, N = w.shape
        rb = next(r for r in range(16, R + 1, 16) if R % r == 0 and R // r <= n_steps)
        last = R // rb - 1
        blk = lambda b, c, last=last: jnp.minimum(b * steps_per_batch + c, last)
        in_specs.append(pl.BlockSpec((None, rb, N), lambda b, c, layer=layer, blk=blk: (layer, blk(b, c), 0)))
        out_specs.append(pl.BlockSpec((rb, N), lambda b, c, blk=blk: (blk(b, c), 0)))
        out_shapes.append(jax.ShapeDtypeStruct((R, N), BF16))
    return in_specs, out_specs, out_shapes


def _run_riders(in_refs, out_refs):
    for i_ref, o_ref in zip(in_refs, out_refs):
        o_ref[...] = i_ref[...].astype(BF16)


def _cast_kernel(n, *refs):
    _run_riders(refs[:n], refs[n:])


def _cast_weights(riders, n_steps, name):
    r_in, r_out, r_shape = _rider_specs(riders, n_steps, n_steps)
    return pl.pallas_call(
        functools.partial(_cast_kernel, len(riders)),
        grid=(1, n_steps),
        in_specs=r_in, out_specs=r_out, out_shape=r_shape,
        compiler_params=_cparams(("arbitrary", "arbitrary")),
        name=name,
    )(*[w for w, _ in riders])


def _rms_store(x, g_ref, h_scr):
    ms = jnp.mean(x * x, axis=-1, keepdims=True)
    h_scr[...] = (x * lax.rsqrt(ms + RMS_EPS) * g_ref[...]).astype(BF16)


def _rms_to_scratch(x_ref, g_ref, h_scr):
    _rms_store(x_ref[...], g_ref, h_scr)


def _row_split(prompt_rows, sample_rows, tm):
    split = prompt_rows % tm
    assert (prompt_rows + sample_rows) % tm == 0 and tm - split == sample_rows
    return split


def _on_row_tile(split, p_ref, s_ref, fn):
    i = pl.program_id(0)
    last = pl.num_programs(0) - 1

    @pl.when(i < last)
    def _():
        fn(p_ref[...])

    @pl.when(i == last)
    def _():
        fn(jnp.concatenate([p_ref[0:split, :], s_ref[...]], axis=0))


def _hgrn_proj_kernel(layer, split, x_ref, *rest):
    xs_ref, rest = (rest[0], rest[1:]) if split is not None else (None, rest)
    gn_ref, wq_ref, wf_ref, wi_ref, wg_ref, lbl_ref, q_ref, k_ref, lf_ref, v_ref, gt_ref, h_scr = rest

    @pl.when(pl.program_id(1) == 0)
    def _():
        if split is None:
            _rms_to_scratch(x_ref, gn_ref, h_scr)
        else:
            _on_row_tile(split, x_ref, xs_ref, lambda x: _rms_store(x, gn_ref, h_scr))

    h = h_scr[...]
    aq = _wdot(h, wq_ref)
    q_ref[...] = (aq * _sigmoid(aq) * (HEAD_DK ** -0.5)).astype(BF16)

    lg = lbl_ref[...]
    mx = jnp.max(lg, axis=0, keepdims=True)
    e = jnp.exp(lg - mx)
    sm = e / jnp.sum(e, axis=0, keepdims=True)
    cs = sm[0:1]
    for r in range(1, layer + 1):
        cs = cs + sm[r:r + 1]
    lb = cs - sm[0:1]

    z = _wdot(h, wf_ref)
    ez = jnp.exp(-jnp.abs(z))
    r = 1.0 / (1.0 + ez)
    pos = z >= 0.0
    sig_p = jnp.where(pos, r, ez * r)
    sig_n = jnp.where(pos, ez * r, r)
    f = lb + (1.0 - lb) * sig_p
    lf_ref[...] = jnp.log(jnp.maximum(f, MIN_FORGET))
    k_ref[...] = ((1.0 - lb) * sig_n).astype(BF16)

    ag = _wdot(h, wg_ref)
    gt_ref[...] = (ag * _sigmoid(ag)).astype(BF16)
    v_ref[...] = _wdot(h, wi_ref).astype(BF16)


def _hgrn_proj(x, gain, wq, wf, wi, wg, lb_logits, layer, tm, tn):
    N = wq.shape[-1]
    L = lb_logits.shape[0]
    if isinstance(x, tuple):
        xp, xs = x
        D = xp.shape[1]
        M = xp.shape[0] + xs.shape[0]
        split = _row_split(xp.shape[0], xs.shape[0], tm)
        x_args = [xp, xs]
        x_specs = [pl.BlockSpec((tm, D), lambda i, n: (i, 0)), pl.BlockSpec(xs.shape, lambda i, n: (0, 0))]
    else:
        M, D = x.shape
        split, x_args, x_specs = None, [x], [pl.BlockSpec((tm, D), lambda i, n: (i, 0))]
    grid = (M // tm, N // tn)
    wspec = _wspec_cols(wq, layer, tn)
    ospec = pl.BlockSpec((tm, tn), lambda i, n: (i, n))
    return pl.pallas_call(
        functools.partial(_hgrn_proj_kernel, layer, split),
        grid=grid,
        in_specs=x_specs + [
                  pl.BlockSpec((1, D), lambda i, n: (0, 0)),
                  wspec, wspec, wspec, wspec,
                  pl.BlockSpec((L, tn), lambda i, n: (0, n))],
        out_specs=[ospec] * 5,
        out_shape=[jax.ShapeDtypeStruct((M, N), BF16),
                   jax.ShapeDtypeStruct((M, N), BF16),
                   jax.ShapeDtypeStruct((M, N), F32),
                   jax.ShapeDtypeStruct((M, N), BF16),
                   jax.ShapeDtypeStruct((M, N), BF16)],
        scratch_shapes=[pltpu.VMEM((tm, D), BF16)],
        compiler_params=_cparams(("parallel", "arbitrary")),
        name=f"hgrn_proj_{layer}",
    )(*x_args, gain, wq, wf, wi, wg, lb_logits)


def _gla_consts(C):
    nlev = int(math.log2(C))
    tri = np.tril(np.ones((C, C), np.float32))
    ii, jj = np.meshgrid(np.arange(C), np.arange(C), indexing="ij")
    x = ii ^ jj
    lev = np.floor(np.log2(np.maximum(x, 1))).astype(np.int32)
    lmap = np.where(ii == jj, -1, np.where(ii > jj, lev, -2)).astype(np.int32)
    rows = np.arange(C)
    sgn = np.stack([np.where((rows >> l) & 1 == 1, LOG2E if l == 0 else 1.0, 0.0 if l == 0 else -1.0)
                    for l in range(nlev)])
    sgn = np.broadcast_to(sgn[:, :, None], (nlev, C, LANES)).astype(np.float32)
    return jnp.asarray(tri, BF16), jnp.asarray(lmap, jnp.int32), jnp.asarray(sgn), nlev


SUBLANES = 8


def _level_exponent(l, g, g2, gs_ref, sgn):
    C = g.shape[0]
    h = 1 << l
    if l == 0:
        return g * sgn
    sub = lax.broadcasted_iota(jnp.int32, (SUBLANES, LANES), 0)
    bcast = lambda r: jnp.broadcast_to(gs_ref[pl.ds(r, 1), :], (SUBLANES, LANES))
    pieces = []
    for base in range(0, C, SUBLANES):
        if 2 * h >= SUBLANES:
            pieces.append(bcast((base // (2 * h)) * 2 * h + h - 1))
        else:
            p = bcast(base + h - 1)
            for blk in range(1, SUBLANES // (2 * h)):
                p = jnp.where(sub >= blk * 2 * h, bcast(base + blk * 2 * h + h - 1), p)
            pieces.append(p)
    gr = jnp.concatenate(pieces, axis=0)
    return (g2 - gr) * sgn


BF16_ROWS = 16


def _gla_level(l, q, k, qf, kf, g, g2, gs_ref, sgn_ref, lmap_b, acc):
    C = g.shape[0]
    h = 1 << l
    rb = BF16_ROWS
    if h < rb:
        e = jnp.exp2(_level_exponent(l, g, g2, gs_ref, sgn_ref[l]))
        al = _dot_nt((qf * e).astype(BF16), (kf * e).astype(BF16))
        return [jnp.where(lmap_b[i] == l, al[i * rb:(i + 1) * rb], a) for i, a in enumerate(acc)]
    bcast = lambda r: jnp.broadcast_to(gs_ref[pl.ds(r, 1), :], (h, LANES))
    lhs, rhs, upper_blocks = [], [], []
    for b in range(C // h):
        rows = slice(b * h, (b + 1) * h)
        if b % 2 == 1:
            e = jnp.exp2(g2[rows] - bcast(b * h - 1))
            lhs.append((qf[rows] * e).astype(BF16))
            rhs.append(k[rows])
            upper_blocks += list(range(b * h // rb, (b + 1) * h // rb))
        else:
            e = jnp.exp2(bcast((b + 1) * h - 1) - g2[rows])
            rhs.append((kf[rows] * e).astype(BF16))
    al = _dot_nt(jnp.concatenate(lhs, axis=0), jnp.concatenate(rhs, axis=0))
    acc = list(acc)
    for j, i in enumerate(upper_blocks):
        acc[i] = jnp.where(lmap_b[i] == l, al[j * rb:(j + 1) * rb], acc[i])
    return acc


def _column_broadcast(row):
    n = row.shape[1]
    eye = lax.broadcasted_iota(jnp.int32, (n, n), 0) == lax.broadcasted_iota(jnp.int32, (n, n), 1)
    x = jnp.where(eye, jnp.broadcast_to(row, (n, n)), 0.0)
    xh = x.astype(BF16)
    xl = (x - xh.astype(F32)).astype(BF16)
    ones = jnp.ones((n, n), BF16)
    return _dot(xh, ones) + _dot(xl, ones)


def _gla_core(chains, tri, lmap, sgn_ref, nlev, gs_scr, state_transposed=True):
    C = chains[0][0].shape[0]
    rb = BF16_ROWS
    lmap_b = [lmap[i * rb:(i + 1) * rb] for i in range(C // rb)]
    g2s = []
    for u, (q, k, g, v, st) in enumerate(chains):
        ghi = g.astype(BF16)
        glo = (g - ghi.astype(F32)).astype(BF16)
        c2 = _dot(tri, jnp.concatenate([ghi, glo], axis=1))
        g2 = (c2[:, :LANES] + c2[:, LANES:]) * LOG2E
        gs_scr[u] = g2
        g2s.append(g2)
    accs = []
    for (q, k, g, v, st) in chains:
        d = _dot_nt(q, k)
        accs.append([jnp.where(lmap_b[i] == -1, d[i * rb:(i + 1) * rb], 0.0) for i in range(C // rb)])
    qk32 = [(q.astype(F32), k.astype(F32)) for (q, k, g, v, st) in chains]
    for l in range(nlev):
        for u, (q, k, g, v, st) in enumerate(chains):
            qf, kf = qk32[u]
            accs[u] = _gla_level(l, q, k, qf, kf, g, g2s[u], gs_scr.at[u], sgn_ref, lmap_b, accs[u])
    outs = []
    for u, (q, k, g, v, st) in enumerate(chains):
        g2 = g2s[u]
        qf, kf = qk32[u]
        o = _dot(jnp.concatenate(accs[u], axis=0).astype(BF16), v)
        qdec = (qf * jnp.exp2(g2)).astype(BF16)
        glast = g2[C - 1:C, :]
        kdec = (kf * jnp.exp2(glast - g2)).astype(BF16)
        if state_transposed:
            o = o + _dot_nt(qdec, st.astype(BF16))
            st_new = st * jnp.exp2(glast) + _dot_tn(v, kdec)
        else:
            o = o + _dot(qdec, st.astype(BF16))
            st_new = st * _column_broadcast(jnp.exp2(glast)) + _dot_tn(kdec, v)
        outs.append((o, st_new))
    return outs


def _gla_out(o, gon, gate):
    ms = jnp.mean(o * o, axis=-1, keepdims=True)
    return o * lax.rsqrt(ms + RMS_EPS) * gon * gate


def _gla_prompt_kernel(nlev, n_heads, n_riders, q_ref, k_ref, g_ref, v_ref, gt_ref, tri_ref, lmap_ref,
                       sgn_ref, gon_ref, *rest):
    rider_in, rest = rest[:n_riders], rest[n_riders:]
    o_ref, sfin_ref = rest[0], rest[1]
    rider_out = rest[2:2 + n_riders]
    st_scr, gs_scr = rest[2 + n_riders:]
    _run_riders(rider_in, rider_out)
    c = pl.program_id(1)
    hpi = _heads_per_iter(n_heads)
    n_iter = n_heads // hpi

    @pl.when(c == 0)
    def _():
        st_scr[...] = jnp.zeros_like(st_scr)

    def body(j, carry):
        heads = [j * hpi + u for u in range(hpi)]
        lanes = [pl.ds(pl.multiple_of(h * LANES, LANES), LANES) for h in heads]
        chains = [(q_ref[:, hs], k_ref[:, hs], g_ref[:, hs], v_ref[:, hs], st_scr[h])
                  for h, hs in zip(heads, lanes)]
        res = _gla_core(chains, tri_ref[...], lmap_ref[...], sgn_ref, nlev, gs_scr)
        for h, hs, (o, st_new) in zip(heads, lanes, res):
            st_scr[h] = st_new
            o_ref[:, hs] = _gla_out(o, gon_ref[...], gt_ref[:, hs].astype(F32)).astype(BF16)
        return carry

    lax.fori_loop(0, n_iter, body, 0)

    @pl.when(c == pl.num_programs(1) - 1)
    def _():
        def wb(h, carry):
            sfin_ref[0, h] = st_scr[h].T
            return carry
        lax.fori_loop(0, n_heads, wb, 0)


def _gla_prompt(q, k, g, v, gate, gon, batch, seq, riders, name):
    C = GLA_CHUNK
    W = q.shape[1]
    H = W // LANES
    nc = seq // C
    tri, lmap, sgn, nlev = _gla_consts(C)
    rspec = pl.BlockSpec((C, W), lambda b, c: (b * nc + c, 0))
    r_in, r_out, r_shape = _rider_specs(riders, nc, batch * nc)
    return pl.pallas_call(
        functools.partial(_gla_prompt_kernel, nlev, H, len(riders)),
        grid=(batch, nc),
        in_specs=[rspec, rspec, rspec, rspec, rspec,
                  pl.BlockSpec(tri.shape, lambda b, c: (0, 0)),
                  pl.BlockSpec(lmap.shape, lambda b, c: (0, 0)),
                  pl.BlockSpec(sgn.shape, lambda b, c: (0, 0, 0)),
                  pl.BlockSpec((1, LANES), lambda b, c: (0, 0))] + r_in,
        out_specs=[rspec, pl.BlockSpec((1, H, LANES, LANES), lambda b, c: (b, 0, 0, 0))] + r_out,
        out_shape=[jax.ShapeDtypeStruct((batch * seq, W), BF16),
                   jax.ShapeDtypeStruct((batch, H, LANES, LANES), F32)] + r_shape,
        scratch_shapes=[pltpu.VMEM((H, LANES, LANES), F32),
                        pltpu.VMEM((_heads_per_iter(H), C, LANES), F32)],
        compiler_params=_cparams(("arbitrary", "arbitrary")),
        name=name,
    )(q, k, g, v, gate, tri, lmap, sgn, gon, *[w for w, _ in riders])


def _gla_sample_kernel(nlev, n_heads, n_new, layer, creates_states, q_ref, k_ref, g_ref, v_ref, gt_ref,
                       tri_ref, lmap_ref, sgn_ref, gon_ref, s0_ref, *rest):
    if creates_states:
        o_ref, sall_ref, gs_scr = rest
        for other in range(sall_ref.shape[0]):
            if other != layer:
                sall_ref[other] = jnp.zeros(sall_ref.shape[1:], F32)
        sfin_ref = sall_ref.at[layer]
    else:
        _, o_ref, sfin_ref, gs_scr = rest
    nb = SAMPLE_PAD // n_new
    zpad = jnp.zeros((SAMPLE_PAD - n_new, LANES), F32)
    hpi = _heads_per_iter(n_heads)

    def body(j, carry):
        heads = [j * hpi + u for u in range(hpi)]
        lanes = [pl.ds(pl.multiple_of(h * LANES, LANES), LANES) for h in heads]
        chains = []
        for h, hs in zip(heads, lanes):
            q = q_ref[:, hs].astype(F32)
            k = k_ref[:, hs].astype(F32)
            g = g_ref[:, hs]
            v = v_ref[:, hs].astype(F32)
            for bb in range(nb):
                rs = slice(bb * n_new, (bb + 1) * n_new)
                pad = lambda t: jnp.concatenate([t[rs], zpad], axis=0)
                chains.append((pad(q).astype(BF16), pad(k).astype(BF16), pad(g), pad(v).astype(BF16),
                               s0_ref[bb, h]))
        res = _gla_core(chains, tri_ref[...], lmap_ref[...], sgn_ref, nlev, gs_scr, state_transposed=False)
        for u, (h, hs) in enumerate(zip(heads, lanes)):
            gate = gt_ref[:, hs].astype(F32)
            outs = []
            for bb in range(nb):
                o, st_new = res[u * nb + bb]
                sfin_ref[bb, h] = st_new
                outs.append(_gla_out(o[0:n_new], gon_ref[...], gate[bb * n_new:(bb + 1) * n_new]))
            o_ref[:, hs] = jnp.concatenate(outs, axis=0).astype(BF16)
        return carry

    lax.fori_loop(0, n_heads // hpi, body, 0)


def _gla_sample(q, k, g, v, gate, gon, s0, layer, row0, n_seq, n_new, name, states=None):
    W = q.shape[1]
    H = W // LANES
    L = s0.shape[0]
    nb = SAMPLE_PAD // n_new
    tri, lmap, sgn, nlev = _gla_consts(SAMPLE_PAD)
    blk0 = row0 // SAMPLE_PAD
    rspec = pl.BlockSpec((SAMPLE_PAD, W), lambda i: (blk0 + i, 0))
    s0spec = pl.BlockSpec((None, nb, H, LANES, LANES), lambda i: (layer, i, 0, 0, 0))
    in_specs = [rspec, rspec, rspec, rspec, rspec,
                pl.BlockSpec(tri.shape, lambda i: (0, 0)),
                pl.BlockSpec(lmap.shape, lambda i: (0, 0)),
                pl.BlockSpec(sgn.shape, lambda i: (0, 0, 0)),
                pl.BlockSpec((1, LANES), lambda i: (0, 0)),
                s0spec]
    args = [q, k, g, v, gate, tri, lmap, sgn, gon, s0]
    if states is None:
        sspec = pl.BlockSpec((L, nb, H, LANES, LANES), lambda i: (0, i, 0, 0, 0))
        aliases = {}
    else:
        sspec = s0spec
        in_specs.append(pl.BlockSpec(memory_space=pl.ANY))
        args.append(states)
        aliases = {len(args) - 1: 1}
    return pl.pallas_call(
        functools.partial(_gla_sample_kernel, nlev, H, n_new, layer, states is None),
        grid=(n_seq // nb,),
        in_specs=in_specs,
        out_specs=[pl.BlockSpec((SAMPLE_PAD, W), lambda i: (i, 0)), sspec],
        out_shape=[jax.ShapeDtypeStruct((n_seq * n_new, W), BF16),
                   jax.ShapeDtypeStruct((L, n_seq, H, LANES, LANES), F32)],
        input_output_aliases=aliases,
        scratch_shapes=[pltpu.VMEM((_heads_per_iter(H) * nb, SAMPLE_PAD, LANES), F32)],
        compiler_params=_cparams(("parallel",)),
        name=name,
    )(*args)


def _out_proj_kernel(split, x_is_pair, ap_ref, as_ref, w_ref, x_ref, *rest):
    xs_ref, o_ref = (rest[0], rest[1]) if x_is_pair else (None, rest[0])
    i = pl.program_id(0)
    last = pl.num_programs(0) - 1
    tail = lambda p_ref, s_ref: jnp.concatenate([p_ref[0:split, :], s_ref[...]], axis=0)

    @pl.when(i < last)
    def _():
        o_ref[...] = x_ref[...] + _wdot(ap_ref[...], w_ref)

    @pl.when(i == last)
    def _():
        x = tail(x_ref, xs_ref) if x_is_pair else x_ref[...]
        o_ref[...] = x + _wdot(tail(ap_ref, as_ref), w_ref)


def _out_proj(a_prompt, a_sample, w, x, tm, name):
    MP, K = a_prompt.shape
    MS = a_sample.shape[0]
    N = w.shape[1]
    M = MP + MS
    split = _row_split(MP, MS, tm)
    x_is_pair = isinstance(x, tuple)
    if x_is_pair:
        x_args = list(x)
        x_specs = [pl.BlockSpec((tm, N), lambda i: (i, 0)), pl.BlockSpec((MS, N), lambda i: (0, 0))]
    else:
        x_args, x_specs = [x], [pl.BlockSpec((tm, N), lambda i: (i, 0))]
    return pl.pallas_call(
        functools.partial(_out_proj_kernel, split, x_is_pair),
        grid=(M // tm,),
        in_specs=[pl.BlockSpec((tm, K), lambda i: (i, 0)),
                  pl.BlockSpec((MS, K), lambda i: (0, 0)),
                  pl.BlockSpec((K, N), lambda i: (0, 0))] + x_specs,
        out_specs=pl.BlockSpec((tm, N), lambda i: (i, 0)),
        out_shape=jax.ShapeDtypeStruct((M, N), F32),
        compiler_params=_cparams(("parallel",)),
        name=name,
    )(a_prompt, a_sample, w, *x_args)


def _ffn_kernel(split, x_ref, gn_ref, wg_ref, wu_ref, wd_ref, o_ref, *rest):
    ys_ref = rest[0] if split is not None else None
    h_scr = rest[-1]
    f = pl.program_id(1)

    @pl.when(f == 0)
    def _():
        _rms_to_scratch(x_ref, gn_ref, h_scr)
        o_ref[...] = x_ref[...]

    h = h_scr[...]
    g = _wdot(h, wg_ref)
    u = _wdot(h, wu_ref)
    a = (g * _sigmoid(g) * u).astype(BF16)
    o_ref[...] += _wdot(a, wd_ref)

    if split is not None:
        @pl.when((f == pl.num_programs(1) - 1) & (pl.program_id(0) == pl.num_programs(0) - 1))
        def _():
            ys_ref[...] = o_ref[split:, :]


def _ffn(x, gain, wg, wu, wd, tm, tf, name, prompt_rows=None):
    M, D = x.shape
    FF = wg.shape[1]
    o_spec = pl.BlockSpec((tm, D), lambda i, f: (i, 0))
    if prompt_rows is None:
        split, out_specs, out_shape = None, o_spec, jax.ShapeDtypeStruct((M, D), F32)
    else:
        split = prompt_rows % tm
        ms = M - prompt_rows
        assert tm - split == ms
        out_specs = [o_spec, pl.BlockSpec((ms, D), lambda i, f: (0, 0))]
        out_shape = [jax.ShapeDtypeStruct((prompt_rows, D), F32), jax.ShapeDtypeStruct((ms, D), F32)]
    return pl.pallas_call(
        functools.partial(_ffn_kernel, split),
        grid=(M // tm, FF // tf),
        in_specs=[pl.BlockSpec((tm, D), lambda i, f: (i, 0)),
                  pl.BlockSpec((1, D), lambda i, f: (0, 0)),
                  pl.BlockSpec((D, tf), lambda i, f: (0, f)),
                  pl.BlockSpec((D, tf), lambda i, f: (0, f)),
                  pl.BlockSpec((tf, D), lambda i, f: (f, 0))],
        out_specs=out_specs,
        out_shape=out_shape,
        scratch_shapes=[pltpu.VMEM((tm, D), BF16)],
        compiler_params=_cparams(("arbitrary", "arbitrary")),
        name=name,
    )(x, gain, wg, wu, wd)


MXU_COLS = 256


def _head_mean_square(acc):
    n = acc.shape[1]
    r = lax.broadcasted_iota(jnp.int32, (n, n), 0)
    c = lax.broadcasted_iota(jnp.int32, (n, n), 1)
    seg = jnp.where((r >> 6) == (c >> 6), 1.0 / ATTN_HEAD_DIM, 0.0).astype(BF16)
    return _dot((acc * acc).astype(BF16), seg)


def _norm_rope(acc, ms, gain, cos, sin, scale):
    y = acc * lax.rsqrt(ms + RMS_EPS) * gain
    lane = lax.broadcasted_iota(jnp.int32, y.shape, 1)
    half = ATTN_HEAD_DIM // 2
    rot = jnp.where((lane & half) == 0, pltpu.roll(y, LANES - half, 1), pltpu.roll(y, half, 1))
    out = y * cos + rot * sin
    return out * scale if scale != 1.0 else out


def _rope_cols(h, w_ref, hg_ref, cos_ref, sin_ref, o_ref, tn, scale):
    for n in range(w_ref.shape[1] // tn):
        acc = _dot(h, w_ref[:, n * tn:(n + 1) * tn])
        for c2 in range(tn // MXU_COLS):
            a2 = acc[:, c2 * MXU_COLS:(c2 + 1) * MXU_COLS]
            ms2 = _head_mean_square(a2)
            for c in range(MXU_COLS // LANES):
                cs = slice(c * LANES, (c + 1) * LANES)
                col = n * tn + c2 * MXU_COLS + c * LANES
                o_ref[:, col:col + LANES] = _norm_rope(a2[:, cs], ms2[:, cs], hg_ref[...], cos_ref[...],
                                                       sin_ref[...], scale).astype(BF16)


def _q_proj_kernel(tn, x_ref, gn_ref, w_ref, hg_ref, cos_ref, sin_ref, q_ref, h_scr):
    _rms_to_scratch(x_ref, gn_ref, h_scr)
    _rope_cols(h_scr[...], w_ref, hg_ref, cos_ref, sin_ref, q_ref, tn, ATTN_SCALE * LOG2E)


def _q_proj(x, gain, w, head_gain, cos, sin, tm, tn, name):
    M, D = x.shape
    N = w.shape[1]
    return pl.pallas_call(
        functools.partial(_q_proj_kernel, tn),
        grid=(M // tm,),
        in_specs=[pl.BlockSpec((tm, D), lambda i: (i, 0)),
                  pl.BlockSpec((1, D), lambda i: (0, 0)),
                  pl.BlockSpec((D, N), lambda i: (0, 0)),
                  pl.BlockSpec((1, LANES), lambda i: (0, 0)),
                  pl.BlockSpec((tm, LANES), lambda i: (i, 0)),
                  pl.BlockSpec((tm, LANES), lambda i: (i, 0))],
        out_specs=pl.BlockSpec((tm, N), lambda i: (i, 0)),
        out_shape=jax.ShapeDtypeStruct((M, N), BF16),
        scratch_shapes=[pltpu.VMEM((tm, D), BF16)],
        compiler_params=_cparams(("parallel",)),
        name=name,
    )(x, gain, w, head_gain, cos, sin)


def _kv_proj_kernel(tn, x_ref, gn_ref, wk_ref, wv_ref, hg_ref, cos_ref, sin_ref, k_ref, v_ref, h_scr):
    _rms_to_scratch(x_ref, gn_ref, h_scr)
    h = h_scr[...]
    _rope_cols(h, wk_ref, hg_ref, cos_ref, sin_ref, k_ref, tn, 1.0)
    for n in range(wv_ref.shape[1] // tn):
        ns = slice(n * tn, (n + 1) * tn)
        v_ref[:, ns] = _dot(h, wv_ref[:, ns]).astype(BF16)


def _kv_proj(x, gain, wk, wv, head_gain, cos, sin, tm, tn):
    M, D = x.shape
    N = wk.shape[1]
    wspec = pl.BlockSpec((D, N), lambda i: (0, 0))
    ospec = pl.BlockSpec((tm, N), lambda i: (i, 0))
    return pl.pallas_call(
        functools.partial(_kv_proj_kernel, tn),
        grid=(M // tm,),
        in_specs=[pl.BlockSpec((tm, D), lambda i: (i, 0)),
                  pl.BlockSpec((1, D), lambda i: (0, 0)),
                  wspec, wspec,
                  pl.BlockSpec((1, LANES), lambda i: (0, 0)),
                  pl.BlockSpec((tm, LANES), lambda i: (i, 0)),
                  pl.BlockSpec((tm, LANES), lambda i: (i, 0))],
        out_specs=[ospec, ospec],
        out_shape=[jax.ShapeDtypeStruct((M, N), BF16), jax.ShapeDtypeStruct((M, N), BF16)],
        scratch_shapes=[pltpu.VMEM((tm, D), BF16)],
        compiler_params=_cparams(("parallel",)),
        name="kv_proj",
    )(x, gain, wk, wv, head_gain, cos, sin)


def _softmax_sink(s_parts, sink2):
    m = sink2
    for s in s_parts:
        m = jnp.maximum(m, jnp.max(s, axis=-1, keepdims=True))
    den = jnp.exp2(sink2 - m)
    ps = []
    for s in s_parts:
        p = jnp.exp2(s - m)
        den = den + jnp.sum(p, axis=-1, keepdims=True)
        ps.append(p.astype(BF16))
    return ps, den


def _pv(ps, v_parts, den):
    o = None
    for p, v in zip(ps, v_parts):
        pv = _dot(p, v)
        o = pv if o is None else o + pv
    return o / den


def _swa_prompt_kernel(n_riders, sk_ref, q_ref, kp_ref, kc_ref, vp_ref, vc_ref, *rest):
    rider_in, o_ref, rider_out = rest[:n_riders], rest[n_riders], rest[n_riders + 1:]
    _run_riders(rider_in, rider_out)
    i = pl.program_id(1)
    W = WINDOW
    row = lax.broadcasted_iota(jnp.int32, (W, W), 0)
    col = lax.broadcasted_iota(jnp.int32, (W, W), 1)
    mask_prev = col > row + jnp.where(i > 0, 0, W)
    mask_cur = col <= row
    lo = lax.broadcasted_iota(jnp.int32, (W, LANES), 1) < ATTN_HEAD_DIM
    sink_col = col == 0
    first_row = lax.broadcasted_iota(jnp.int32, (W, LANES), 0) == 0

    def scores(g):
        gs = slice(g * LANES, (g + 1) * LANES)
        kd = jnp.concatenate([kp_ref[:, gs], kc_ref[:, gs]], axis=0)
        lhs = []
        for c in range(GQA_GROUP // 2):
            qc = q_ref[:, (2 * g + c) * LANES:(2 * g + c + 1) * LANES]
            zero = jnp.zeros_like(qc)
            lhs += [jnp.where(lo, qc, zero), jnp.where(lo, zero, qc)]
        s = _dot_nt(jnp.concatenate(lhs, axis=0), kd)
        out = []
        for a in range(GQA_GROUP):
            sa = s[a * W:(a + 1) * W]
            fill = jnp.where(sink_col, sk_ref[GQA_GROUP * g + a] * LOG2E, MASK_VALUE)
            out.append(jnp.concatenate([jnp.where(mask_prev, sa[:, :W], fill),
                                        jnp.where(mask_cur, sa[:, W:], MASK_VALUE)], axis=1))
        return out

    def finish(g, s_heads):
        gs = slice(g * LANES, (g + 1) * LANES)
        vprev = vp_ref[:, gs]
        vd = jnp.concatenate([jnp.where(first_row, jnp.zeros_like(vprev), vprev), vc_ref[:, gs]], axis=0)
        ps, dens = [], []
        for s in s_heads:
            m = jnp.max(s, axis=-1, keepdims=True)
            p = jnp.exp2(s - m)
            dens.append(jnp.sum(p, axis=-1, keepdims=True))
            ps.append(p.astype(BF16))
        o = _dot(jnp.concatenate(ps, axis=0), vd)
        for c in range(GQA_GROUP // 2):
            cs = slice((2 * g + c) * LANES, (2 * g + c + 1) * LANES)
            o0 = o[(2 * c) * W:(2 * c + 1) * W] / dens[2 * c]
            o1 = o[(2 * c + 1) * W:(2 * c + 2) * W] / dens[2 * c + 1]
            o_ref[:, cs] = jnp.where(lo, o0, o1).astype(BF16)

    s_next = scores(0)
    for g in range(ATTN_KV_HEADS):
        s_cur = s_next
        if g + 1 < ATTN_KV_HEADS:
            s_next = scores(g + 1)
        finish(g, s_cur)


def _swa_prompt(q, kdup, vdup, sinks, batch, seq, riders, name):
    W = WINDOW
    nb = seq // W
    QW = q.shape[1]
    KW = kdup.shape[1]
    prev = lambda b, i: (b * nb + jnp.maximum(i - 1, 0), 0)
    cur = lambda b, i: (b * nb + i, 0)
    r_in, r_out, r_shape = _rider_specs(riders, nb, batch * nb)
    return pl.pallas_call(
        functools.partial(_swa_prompt_kernel, len(riders)),
        grid=(batch, nb),
        in_specs=[pl.BlockSpec(memory_space=pltpu.SMEM),
                  pl.BlockSpec((W, QW), cur),
                  pl.BlockSpec((W, KW), prev), pl.BlockSpec((W, KW), cur),
                  pl.BlockSpec((W, KW), prev), pl.BlockSpec((W, KW), cur)] + r_in,
        out_specs=[pl.BlockSpec((W, QW), cur)] + r_out,
        out_shape=[jax.ShapeDtypeStruct((batch * seq, QW), BF16)] + r_shape,
        compiler_params=_cparams(("arbitrary", "arbitrary")),
        name=name,
    )(sinks, q, kdup, kdup, vdup, vdup, *[w for w, _ in riders])


def _swa_sample_kernel(n_new, sk_ref, q_ref, kn_ref, vn_ref, kc_ref, vc_ref, o_ref):
    W = WINDOW
    nb = SAMPLE_PAD // n_new
    rows = GQA_GROUP * n_new
    qi = lax.broadcasted_iota(jnp.int32, (rows, W), 0) & (n_new - 1)
    mask_c = lax.broadcasted_iota(jnp.int32, (rows, W), 1) > qi
    qi_n = lax.broadcasted_iota(jnp.int32, (rows, SAMPLE_PAD), 0) & (n_new - 1)
    mask_n = lax.broadcasted_iota(jnp.int32, (rows, SAMPLE_PAD), 1) <= qi_n
    hrow = lax.broadcasted_iota(jnp.int32, (rows, 1), 0) // n_new
    lo = lax.broadcasted_iota(jnp.int32, (n_new, LANES), 1) < ATTN_HEAD_DIM
    zq = jnp.zeros((n_new, LANES), F32)
    zpad = jnp.zeros((SAMPLE_PAD - n_new, LANES), F32)
    q = q_ref[...].astype(F32)
    kn = kn_ref[...].astype(F32)
    vn = vn_ref[...].astype(F32)
    chains = [(bb, g) for bb in range(nb) for g in range(ATTN_KV_HEADS)]
    sinks2 = []
    for g in range(ATTN_KV_HEADS):
        sink = jnp.zeros((rows, 1), F32)
        for a in range(GQA_GROUP):
            sink = jnp.where(hrow == a, sk_ref[GQA_GROUP * g + a] * LOG2E, sink)
        sinks2.append(sink)
    scores, values = [], []
    for bb, g in chains:
        rs = slice(bb * n_new, (bb + 1) * n_new)
        gs = slice(g * LANES, (g + 1) * LANES)
        kn_g = jnp.concatenate([kn[rs, gs], zpad], axis=0).astype(BF16)
        vn_g = jnp.concatenate([vn[rs, gs], zpad], axis=0).astype(BF16)
        lhs = []
        for a in range(GQA_GROUP):
            c, par = divmod(a, 2)
            qc = q[rs, (2 * g + c) * LANES:(2 * g + c + 1) * LANES]
            lhs.append(jnp.where(lo, qc, zq) if par == 0 else jnp.where(lo, zq, qc))
        lhs = jnp.concatenate(lhs, axis=0).astype(BF16)
        s_c = jnp.where(mask_c, _dot_nt(lhs, kc_ref[bb, :, gs]), MASK_VALUE)
        s_n = jnp.where(mask_n, _dot_nt(lhs, kn_g), MASK_VALUE)
        scores.append([s_c, s_n])
        values.append([vc_ref[bb, :, gs], vn_g])
    probs = [_softmax_sink(s, sinks2[g]) for s, (bb, g) in zip(scores, chains)]
    outs = [_pv(ps, v, den) for (ps, den), v in zip(probs, values)]
    out_rows = []
    for bb in range(nb):
        chunks = []
        for g in range(ATTN_KV_HEADS):
            o = outs[bb * ATTN_KV_HEADS + g]
            for c in range(GQA_GROUP // 2):
                chunks.append(jnp.where(lo, o[(2 * c) * n_new:(2 * c + 1) * n_new],
                                        o[(2 * c + 1) * n_new:(2 * c + 2) * n_new]))
        out_rows.append(jnp.concatenate(chunks, axis=1))
    o_ref[...] = jnp.concatenate(out_rows, axis=0).astype(BF16)


def _swa_sample(q, kdup, vdup, cache_kdup, cache_vdup, sinks, row0, n_seq, n_new, name):
    W = WINDOW
    QW = q.shape[1]
    KW = kdup.shape[1]
    nb = SAMPLE_PAD // n_new
    blk0 = row0 // SAMPLE_PAD
    rmap = lambda i: (blk0 + i, 0)
    cspec = pl.BlockSpec((nb, W, KW), lambda i: (i, 0, 0))
    return pl.pallas_call(
        functools.partial(_swa_sample_kernel, n_new),
        grid=(n_seq // nb,),
        in_specs=[pl.BlockSpec(memory_space=pltpu.SMEM),
                  pl.BlockSpec((SAMPLE_PAD, QW), rmap),
                  pl.BlockSpec((SAMPLE_PAD, KW), rmap), pl.BlockSpec((SAMPLE_PAD, KW), rmap),
                  cspec, cspec],
        out_specs=pl.BlockSpec((SAMPLE_PAD, QW), lambda i: (i, 0)),
        out_shape=jax.ShapeDtypeStruct((n_seq * n_new, QW), BF16),
        compiler_params=_cparams(("parallel",)),
        name=name,
    )(sinks, q, kdup, vdup, cache_kdup, cache_vdup)


def _dup_heads(w):
    lead = w.shape[:-1]
    w3 = w.reshape(lead + (ATTN_KV_HEADS, ATTN_HEAD_DIM))
    return jnp.concatenate([w3, w3], axis=-1).reshape(lead + (ATTN_KV_HEADS * 2 * ATTN_HEAD_DIM,))


def _undup_heads(a, lead):
    return a.reshape(lead + (ATTN_KV_HEADS, 2 * ATTN_HEAD_DIM))[..., :ATTN_HEAD_DIM].astype(F32)


def _rope_tables(pos):
    half = ATTN_HEAD_DIM // 2
    inv = ROPE_THETA ** (-jnp.arange(half, dtype=F32) / half)
    ang = pos.astype(F32)[:, None] * inv[None, :]
    cos = jnp.cos(ang)
    sin = jnp.sin(ang)
    reps = LANES // ATTN_HEAD_DIM
    cos_t = jnp.tile(jnp.concatenate([cos, cos], axis=-1), (1, reps))
    sin_t = jnp.tile(jnp.concatenate([-sin, sin], axis=-1), (1, reps))
    return cos_t, sin_t


def kernel(x_prompt, x_sample, state_hgrn, cache_k_win, cache_v_win, hgrn_norm, hgrn_wq, hgrn_wf, hgrn_wi, hgrn_wg, hgrn_lb_logits, hgrn_onorm, hgrn_wo, kv_norm, w_k, w_v, k_norm, attn_norm, attn_wq, q_norm, sinks, attn_wo, ffn_norm, w_gate, w_up, w_down):
    B, T, D = x_prompt.shape
    SB, S, _ = x_sample.shape
    MP = B * T
    MS = SB * S
    n_a = hgrn_wq.shape[0]
    n_b = attn_wq.shape[0]
    assert SAMPLE_PAD % S == 0 and SB % (SAMPLE_PAD // S) == 0 and MP % SAMPLE_PAD == 0
    assert T % GLA_CHUNK == 0 and T % WINDOW == 0

    x = (x_prompt.reshape(MP, D).astype(F32), x_sample.reshape(MS, D).astype(F32))
    M = MP + MS
    tm = 768 if M % 768 == 0 else 256
    tn = 512
    tf = 512

    bf = lambda w: w.astype(BF16)
    f32 = lambda w: w.astype(F32)
    row = lambda g: g.reshape(1, -1).astype(F32)
    pos = jnp.concatenate([jnp.tile(jnp.arange(T), B), jnp.tile(PAST_LEN + jnp.arange(S), SB)])
    cos_t, sin_t = _rope_tables(pos)
    head_gain = lambda g: jnp.tile(g.astype(F32), LANES // ATTN_HEAD_DIM).reshape(1, LANES)

    w_gate32, w_up32, w_down32 = f32(w_gate), f32(w_up), f32(w_down)
    hgrn_w32 = [f32(hgrn_wq), f32(hgrn_wf), f32(hgrn_wi), f32(hgrn_wg)]
    hgrn_wo32, attn_wq32, attn_wo32 = f32(hgrn_wo), f32(attn_wq), f32(attn_wo)
    ffn_riders = lambda l: [(w_gate32, l), (w_up32, l), (w_down32, l)]

    depth = n_a + n_b
    ffn_bf = {}
    hosted = {l: list(range(l * depth // n_a, (l + 1) * depth // n_a)) for l in range(n_a)}
    st_prompt, st_sample = [], None
    s0 = f32(state_hgrn)
    proj_w = _cast_weights([(w, 0) for w in hgrn_w32], 16, "cast_hgrn_0") if n_a else None
    attn_wq_bf = None
    for l in range(n_a):
        q, k, lf, v, gate = _hgrn_proj(x, row(hgrn_norm[l]), *proj_w, f32(hgrn_lb_logits), l, tm, tn)
        gon = row(hgrn_onorm[l])
        riders = [(hgrn_wo32, l)] + [r for fl in hosted[l] for r in ffn_riders(fl)]
        if l + 1 < n_a:
            riders += [(w, l + 1) for w in hgrn_w32]
        elif n_b > 0:
            riders += [(attn_wq32, 0)]
        o_p, s_p, wo_bf, *cast = _gla_prompt(q, k, lf, v, gate, gon, B, T, riders, f"gla_prompt_{l}")
        for fl in hosted[l]:
            ffn_bf[fl], cast = cast[:3], cast[3:]
        if l + 1 < n_a:
            proj_w = cast
        elif n_b > 0:
            attn_wq_bf = cast[0]
        o_s, st_sample = _gla_sample(q, k, lf, v, gate, gon, s0, l, MP, SB, S, f"gla_sample_{l}",
                                     states=st_sample)
        st_prompt.append(s_p)
        x = _out_proj(o_p, o_s, wo_bf, x, tm, f"hgrn_out_{l}")
        x = _ffn(x, row(ffn_norm[l]), *ffn_bf.pop(l), tm, tf, f"ffn_{l}")

    kdup, vdup = _kv_proj(x, row(kv_norm), bf(_dup_heads(w_k)), bf(_dup_heads(w_v)), head_gain(k_norm),
                          cos_t, sin_t, tm, tn)
    KW = ATTN_KV_HEADS * ATTN_HEAD_DIM
    cache_kdup = bf(_dup_heads(cache_k_win.reshape(SB, WINDOW, KW)))
    cache_vdup = bf(_dup_heads(cache_v_win.reshape(SB, WINDOW, KW)))
    for j in range(n_b):
        l = n_a + j
        if attn_wq_bf is None:
            attn_wq_bf = bf(attn_wq[j])
        q = _q_proj(x, row(attn_norm[j]), attn_wq_bf, head_gain(q_norm[j]), cos_t, sin_t, tm, tn,
                    f"q_proj_{j}")
        sk = sinks[j].astype(F32)
        own_ffn = l not in ffn_bf
        riders = [(attn_wo32, j)] + (ffn_riders(l) if own_ffn else []) + (
            [(attn_wq32, j + 1)] if j + 1 < n_b else [])
        a_p, wo_bf, *cast = _swa_prompt(q, kdup, vdup, sk, B, T, riders, f"swa_prompt_{j}")
        if own_ffn:
            ffn_bf[l], cast = cast[:3], cast[3:]
        attn_wq_bf = cast[0] if cast else None
        a_s = _swa_sample(q, kdup, vdup, cache_kdup, cache_vdup, sk, MP, SB, S, f"swa_sample_{j}")
        x = _out_proj(a_p, a_s, wo_bf, x, tm, f"attn_out_{j}")
        last = j == n_b - 1
        x = _ffn(x, row(ffn_norm[l]), *ffn_bf.pop(l), tm, tf, f"ffn_{l}", prompt_rows=MP if last else None)

    y_prompt = x[0].reshape(B, T, D)
    y_sample = x[1].reshape(SB, S, D)
    k_win_p = _undup_heads(kdup[:MP].reshape(B, T, -1)[:, T - WINDOW:], (B, WINDOW))
    v_win_p = _undup_heads(vdup[:MP].reshape(B, T, -1)[:, T - WINDOW:], (B, WINDOW))
    k_new_s = _undup_heads(kdup[MP:].reshape(SB, S, -1), (SB, S))
    v_new_s = _undup_heads(vdup[MP:].reshape(SB, S, -1), (SB, S))
    k_win_s = jnp.concatenate([cache_k_win[:, S:].astype(F32), k_new_s], axis=1)
    v_win_s = jnp.concatenate([cache_v_win[:, S:].astype(F32), v_new_s], axis=1)
    return (y_prompt, y_sample, jnp.stack(st_prompt), st_sample,
            k_win_p, v_win_p, k_win_s, v_win_s)
```

```python
import functools
import math

import numpy as np
import jax
import jax.numpy as jnp
from jax import lax
from jax.experimental import pallas as pl
from jax.experimental.pallas import tpu as pltpu

HGRN_HEADS = 16
HEAD_DK = 128
ATTN_HEAD_DIM = 64
ATTN_KV_HEADS = 8
GQA_GROUP = 4
WINDOW = 128
ROPE_THETA = 10000.0
ATTN_SCALE = ATTN_HEAD_DIM ** -0.5
LOG2E = math.log2(math.e)
RMS_EPS = 1e-6
MASK_VALUE = -1e30
MIN_FORGET = 1e-30
PAST_LEN = 16384

LANES = 128
VMEM_LIMIT_BYTES = 56 * 1024 * 1024

GLA_CHUNK = 128
SAMPLE_PAD = 16
GLA_HEADS_PER_ITER = 16


def _heads_per_iter(n_heads):
    return math.gcd(n_heads, GLA_HEADS_PER_ITER)

F32 = jnp.float32
BF16 = jnp.bfloat16


def _cparams(sem):
    return pltpu.CompilerParams(dimension_semantics=sem, vmem_limit_bytes=VMEM_LIMIT_BYTES)


def _dot(a, b):
    return jnp.dot(a, b, preferred_element_type=F32)


def _dot_nt(a, b):
    return lax.dot_general(a, b, (((1,), (1,)), ((), ())), preferred_element_type=F32)


def _dot_tn(a, b):
    return lax.dot_general(a, b, (((0,), (0,)), ((), ())), preferred_element_type=F32)


def _wdot(a, w_ref):
    return _dot(a, w_ref[...].astype(BF16))


def _wspec_cols(w, layer, tn):
    k = w.shape[-2]
    if w.ndim == 3:
        return pl.BlockSpec((None, k, tn), lambda i, n: (layer, 0, n))
    return pl.BlockSpec((k, tn), lambda i, n: (0, n))


def _sigmoid(x):
    return 0.5 * jnp.tanh(0.5 * x) + 0.5


def _rider_specs(riders, steps_per_batch, n_steps):
    in_specs, out_specs, out_shapes = [], [], []
    for w, layer in riders:
        _, R, N = w.shape
        rb = next(r for r in range(16, R + 1, 16) if R % r == 0 and R // r <= n_steps)
        last = R // rb - 1
        blk = lambda b, c, last=last: jnp.minimum(b * steps_per_batch + c, last)
        in_specs.append(pl.BlockSpec((None, rb, N), lambda b, c, layer=layer, blk=blk: (layer, blk(b, c), 0)))
        out_specs.append(pl.BlockSpec((rb, N), lambda b, c, blk=blk: (blk(b, c), 0)))
        out_shapes.append(jax.ShapeDtypeStruct((R, N), BF16))
    return in_specs, out_specs, out_shapes


def _run_riders(in_refs, out_refs):
    for i_ref, o_ref in zip(in_refs, out_refs):
        o_ref[...] = i_ref[...].astype(BF16)


def _cast_kernel(n, *refs):
    _run_riders(refs[:n], refs[n:])


def _cast_weights(riders, n_steps, name):
    r_in, r_out, r_shape = _rider_specs(riders, n_steps, n_steps)
    return pl.pallas_call(
        functools.partial(_cast_kernel, len(riders)),
        grid=(1, n_steps),
        in_specs=r_in, out_specs=r_out, out_shape=r_shape,
        compiler_params=_cparams(("arbitrary", "arbitrary")),
        name=name,
    )(*[w for w, _ in riders])


def _rms_store(x, g_ref, h_scr):
    ms = jnp.mean(x * x, axis=-1, keepdims=True)
    h_scr[...] = (x * lax.rsqrt(ms + RMS_EPS) * g_ref[...]).astype(BF16)


def _rms_to_scratch(x_ref, g_ref, h_scr):
    _rms_store(x_ref[...], g_ref, h_scr)


def _row_split(prompt_rows, sample_rows, tm):
    split = prompt_rows % tm
    assert (prompt_rows + sample_rows) % tm == 0 and tm - split == sample_rows
    return split


def _on_row_tile(split, p_ref, s_ref, fn):
    i = pl.program_id(0)
    last = pl.num_programs(0) - 1

    @pl.when(i < last)
    def _():
        fn(p_ref[...])

    @pl.when(i == last)
    def _():
        fn(jnp.concatenate([p_ref[0:split, :], s_ref[...]], axis=0))


def _hgrn_proj_kernel(layer, split, x_ref, *rest):
    xs_ref, rest = (rest[0], rest[1:]) if split is not None else (None, rest)
    gn_ref, wq_ref, wf_ref, wi_ref, wg_ref, lbl_ref, q_ref, k_ref, lf_ref, v_ref, gt_ref, h_scr = rest

    @pl.when(pl.program_id(1) == 0)
    def _():
        if split is None:
            _rms_to_scratch(x_ref, gn_ref, h_scr)
        else:
            _on_row_tile(split, x_ref, xs_ref, lambda x: _rms_store(x, gn_ref, h_scr))

    h = h_scr[...]
    aq = _wdot(h, wq_ref)
    q_ref[...] = (aq * _sigmoid(aq) * (HEAD_DK ** -0.5)).astype(BF16)

    lg = lbl_ref[...]
    mx = jnp.max(lg, axis=0, keepdims=True)
    e = jnp.exp(lg - mx)
    sm = e / jnp.sum(e, axis=0, keepdims=True)
    cs = sm[0:1]
    for r in range(1, layer + 1):
        cs = cs + sm[r:r + 1]
    lb = cs - sm[0:1]

    z = _wdot(h, wf_ref)
    ez = jnp.exp(-jnp.abs(z))
    r = 1.0 / (1.0 + ez)
    pos = z >= 0.0
    sig_p = jnp.where(pos, r, ez * r)
    sig_n = jnp.where(pos, ez * r, r)
    f = lb + (1.0 - lb) * sig_p
    lf_ref[...] = jnp.log(jnp.maximum(f, MIN_FORGET))
    k_ref[...] = ((1.0 - lb) * sig_n).astype(BF16)

    ag = _wdot(h, wg_ref)
    gt_ref[...] = (ag * _sigmoid(ag)).astype(BF16)
    v_ref[...] = _wdot(h, wi_ref).astype(BF16)


def _hgrn_proj(x, gain, wq, wf, wi, wg, lb_logits, layer, tm, tn):
    N = wq.shape[-1]
    L = lb_logits.shape[0]
    if isinstance(x, tuple):
        xp, xs = x
        D = xp.shape[1]
        M = xp.shape[0] + xs.shape[0]
        split = _row_split(xp.shape[0], xs.shape[0], tm)
        x_args = [xp, xs]
        x_specs = [pl.BlockSpec((tm, D), lambda i, n: (i, 0)), pl.BlockSpec(xs.shape, lambda i, n: (0, 0))]
    else:
        M, D = x.shape
        split, x_args, x_specs = None, [x], [pl.BlockSpec((tm, D), lambda i, n: (i, 0))]
    grid = (M // tm, N // tn)
    wspec = _wspec_cols(wq, layer, tn)
    ospec = pl.BlockSpec((tm, tn), lambda i, n: (i, n))
    return pl.pallas_call(
        functools.partial(_hgrn_proj_kernel, layer, split),
        grid=grid,
        in_specs=x_specs + [
                  pl.BlockSpec((1, D), lambda i, n: (0, 0)),
                  wspec, wspec, wspec, wspec,
                  pl.BlockSpec((L, tn), lambda i, n: (0, n))],
        out_specs=[ospec] * 5,
        out_shape=[jax.ShapeDtypeStruct((M, N), BF16),
                   jax.ShapeDtypeStruct((M, N), BF16),
                   jax.ShapeDtypeStruct((M, N), F32),
                   jax.ShapeDtypeStruct((M, N), BF16),
                   jax.ShapeDtypeStruct((M, N), BF16)],
        scratch_shapes=[pltpu.VMEM((tm, D), BF16)],
        compiler_params=_cparams(("parallel", "arbitrary")),
        name=f"hgrn_proj_{layer}",
    )(*x_args, gain, wq, wf, wi, wg, lb_logits)


def _gla_consts(C):
    nlev = int(math.log2(C))
    tri = np.tril(np.ones((C, C), np.float32))
    ii, jj = np.meshgrid(np.arange(C), np.arange(C), indexing="ij")
    x = ii ^ jj
    lev = np.floor(np.log2(np.maximum(x, 1))).astype(np.int32)
    lmap = np.where(ii == jj, -1, np.where(ii > jj, lev, -2)).astype(np.int32)
    rows = np.arange(C)
    sgn = np.stack([np.where((rows >> l) & 1 == 1, LOG2E if l == 0 else 1.0, 0.0 if l == 0 else -1.0)
                    for l in range(nlev)])
    sgn = np.broadcast_to(sgn[:, :, None], (nlev, C, LANES)).astype(np.float32)
    return jnp.asarray(tri, BF16), jnp.asarray(lmap, jnp.int32), jnp.asarray(sgn), nlev


SUBLANES = 8


def _level_exponent(l, g, g2, gs_ref, sgn):
    C = g.shape[0]
    h = 1 << l
    if l == 0:
        return g * sgn
    sub = lax.broadcasted_iota(jnp.int32, (SUBLANES, LANES), 0)
    bcast = lambda r: jnp.broadcast_to(gs_ref[pl.ds(r, 1), :], (SUBLANES, LANES))
    pieces = []
    for base in range(0, C, SUBLANES):
        if 2 * h >= SUBLANES:
            pieces.append(bcast((base // (2 * h)) * 2 * h + h - 1))
        else:
            p = bcast(base + h - 1)
            for blk in range(1, SUBLANES // (2 * h)):
                p = jnp.where(sub >= blk * 2 * h, bcast(base + blk * 2 * h + h - 1), p)
            pieces.append(p)
    gr = jnp.concatenate(pieces, axis=0)
    return (g2 - gr) * sgn


BF16_ROWS = 16


def _gla_level(l, q, k, qf, kf, g, g2, gs_ref, sgn_ref, lmap_b, acc):
    C = g.shape[0]
    h = 1 << l
    rb = BF16_ROWS
    if h < rb:
        e = jnp.exp2(_level_exponent(l, g, g2, gs_ref, sgn_ref[l]))
        al = _dot_nt((qf * e).astype(BF16), (kf * e).astype(BF16))
        return [jnp.where(lmap_b[i] == l, al[i * rb:(i + 1) * rb], a) for i, a in enumerate(acc)]
    bcast = lambda r: jnp.broadcast_to(gs_ref[pl.ds(r, 1), :], (h, LANES))
    lhs, rhs, upper_blocks = [], [], []
    for b in range(C // h):
        rows = slice(b * h, (b + 1) * h)
        if b % 2 == 1:
            e = jnp.exp2(g2[rows] - bcast(b * h - 1))
            lhs.append((qf[rows] * e).astype(BF16))
            rhs.append(k[rows])
            upper_blocks += list(range(b * h // rb, (b + 1) * h // rb))
        else:
            e = jnp.exp2(bcast((b + 1) * h - 1) - g2[rows])
            rhs.append((kf[rows] * e).astype(BF16))
    al = _dot_nt(jnp.concatenate(lhs, axis=0), jnp.concatenate(rhs, axis=0))
    acc = list(acc)
    for j, i in enumerate(upper_blocks):
        acc[i] = jnp.where(lmap_b[i] == l, al[j * rb:(j + 1) * rb], acc[i])
    return acc


def _column_broadcast(row):
    n = row.shape[1]
    eye = lax.broadcasted_iota(jnp.int32, (n, n), 0) == lax.broadcasted_iota(jnp.int32, (n, n), 1)
    x = jnp.where(eye, jnp.broadcast_to(row, (n, n)), 0.0)
    xh = x.astype(BF16)
    xl = (x - xh.astype(F32)).astype(BF16)
    ones = jnp.ones((n, n), BF16)
    return _dot(xh, ones) + _dot(xl, ones)


def _gla_core(chains, tri, lmap, sgn_ref, nlev, gs_scr, state_transposed=True):
    C = chains[0][0].shape[0]
    rb = BF16_ROWS
    lmap_b = [lmap[i * rb:(i + 1) * rb] for i in range(C // rb)]
    g2s = []
    for u, (q, k, g, v, st) in enumerate(chains):
        ghi = g.astype(BF16)
        glo = (g - ghi.astype(F32)).astype(BF16)
        c2 = _dot(tri, jnp.concatenate([ghi, glo], axis=1))
        g2 = (c2[:, :LANES] + c2[:, LANES:]) * LOG2E
        gs_scr[u] = g2
        g2s.append(g2)
    accs = []
    for (q, k, g, v, st) in chains:
        d = _dot_nt(q, k)
        accs.append([jnp.where(lmap_b[i] == -1, d[i * rb:(i + 1) * rb], 0.0) for i in range(C // rb)])
    qk32 = [(q.astype(F32), k.astype(F32)) for (q, k, g, v, st) in chains]
    for l in range(nlev):
        for u, (q, k, g, v, st) in enumerate(chains):
            qf, kf = qk32[u]
            accs[u] = _gla_level(l, q, k, qf, kf, g, g2s[u], gs_scr.at[u], sgn_ref, lmap_b, accs[u])
    outs = []
    for u, (q, k, g, v, st) in enumerate(chains):
        g2 = g2s[u]
        qf, kf = qk32[u]
        o = _dot(jnp.concatenate(accs[u], axis=0).astype(BF16), v)
        qdec = (qf * jnp.exp2(g2)).astype(BF16)
        glast = g2[C - 1:C, :]
        kdec = (kf * jnp.exp2(glast - g2)).astype(BF16)
        if state_transposed:
            o = o + _dot_nt(qdec, st.astype(BF16))
            st_new = st * jnp.exp2(glast) + _dot_tn(v, kdec)
        else:
            o = o + _dot(qdec, st.astype(BF16))
            st_new = st * _column_broadcast(jnp.exp2(glast)) + _dot_tn(kdec, v)
        outs.append((o, st_new))
    return outs


def _gla_out(o, gon, gate):
    ms = jnp.mean(o * o, axis=-1, keepdims=True)
    return o * lax.rsqrt(ms + RMS_EPS) * gon * gate


def _gla_prompt_kernel(nlev, n_heads, n_riders, q_ref, k_ref, g_ref, v_ref, gt_ref, tri_ref, lmap_ref,
                       sgn_ref, gon_ref, *rest):
    rider_in, rest = rest[:n_riders], rest[n_riders:]
    o_ref, sfin_ref = rest[0], rest[1]
    rider_out = rest[2:2 + n_riders]
    st_scr, gs_scr = rest[2 + n_riders:]
    _run_riders(rider_in, rider_out)
    c = pl.program_id(1)
    hpi = _heads_per_iter(n_heads)
    n_iter = n_heads // hpi

    @pl.when(c == 0)
    def _():
        st_scr[...] = jnp.zeros_like(st_scr)

    def body(j, carry):
        heads = [j * hpi + u for u in range(hpi)]
        lanes = [pl.ds(pl.multiple_of(h * LANES, LANES), LANES) for h in heads]
        chains = [(q_ref[:, hs], k_ref[:, hs], g_ref[:, hs], v_ref[:, hs], st_scr[h])
                  for h, hs in zip(heads, lanes)]
        res = _gla_core(chains, tri_ref[...], lmap_ref[...], sgn_ref, nlev, gs_scr)
        for h, hs, (o, st_new) in zip(heads, lanes, res):
            st_scr[h] = st_new
            o_ref[:, hs] = _gla_out(o, gon_ref[...], gt_ref[:, hs].astype(F32)).astype(BF16)
        return carry

    lax.fori_loop(0, n_iter, body, 0)

    @pl.when(c == pl.num_programs(1) - 1)
    def _():
        def wb(h, carry):
            sfin_ref[0, h] = st_scr[h].T
            return carry
        lax.fori_loop(0, n_heads, wb, 0)


def _gla_prompt(q, k, g, v, gate, gon, batch, seq, riders, name):
    C = GLA_CHUNK
    W = q.shape[1]
    H = W // LANES
    nc = seq // C
    tri, lmap, sgn, nlev = _gla_consts(C)
    rspec = pl.BlockSpec((C, W), lambda b, c: (b * nc + c, 0))
    r_in, r_out, r_shape = _rider_specs(riders, nc, batch * nc)
    return pl.pallas_call(
        functools.partial(_gla_prompt_kernel, nlev, H, len(riders)),
        grid=(batch, nc),
        in_specs=[rspec, rspec, rspec, rspec, rspec,
                  pl.BlockSpec(tri.shape, lambda b, c: (0, 0)),
                  pl.BlockSpec(lmap.shape, lambda b, c: (0, 0)),
                  pl.BlockSpec(sgn.shape, lambda b, c: (0, 0, 0)),
                  pl.BlockSpec((1, LANES), lambda b, c: (0, 0))] + r_in,
        out_specs=[rspec, pl.BlockSpec((1, H, LANES, LANES), lambda b, c: (b, 0, 0, 0))] + r_out,
        out_shape=[jax.ShapeDtypeStruct((batch * seq, W), BF16),
                   jax.ShapeDtypeStruct((batch, H, LANES, LANES), F32)] + r_shape,
        scratch_shapes=[pltpu.VMEM((H, LANES, LANES), F32),
                        pltpu.VMEM((_heads_per_iter(H), C, LANES), F32)],
        compiler_params=_cparams(("arbitrary", "arbitrary")),
        name=name,
    )(q, k, g, v, gate, tri, lmap, sgn, gon, *[w for w, _ in riders])


def _gla_sample_kernel(nlev, n_heads, n_new, layer, creates_states, q_ref, k_ref, g_ref, v_ref, gt_ref,
                       tri_ref, lmap_ref, sgn_ref, gon_ref, s0_ref, *rest):
    if creates_states:
        o_ref, sall_ref, gs_scr = rest
        for other in range(sall_ref.shape[0]):
            if other != layer:
                sall_ref[other] = jnp.zeros(sall_ref.shape[1:], F32)
        sfin_ref = sall_ref.at[layer]
    else:
        _, o_ref, sfin_ref, gs_scr = rest
    nb = SAMPLE_PAD // n_new
    zpad = jnp.zeros((SAMPLE_PAD - n_new, LANES), F32)
    hpi = _heads_per_iter(n_heads)

    def body(j, carry):
        heads = [j * hpi + u for u in range(hpi)]
        lanes = [pl.ds(pl.multiple_of(h * LANES, LANES), LANES) for h in heads]
        chains = []
        for h, hs in zip(heads, lanes):
            q = q_ref[:, hs].astype(F32)
            k = k_ref[:, hs].astype(F32)
            g = g_ref[:, hs]
            v = v_ref[:, hs].astype(F32)
            for bb in range(nb):
                rs = slice(bb * n_new, (bb + 1) * n_new)
                pad = lambda t: jnp.concatenate([t[rs], zpad], axis=0)
                chains.append((pad(q).astype(BF16), pad(k).astype(BF16), pad(g), pad(v).astype(BF16),
                               s0_ref[bb, h]))
        res = _gla_core(chains, tri_ref[...], lmap_ref[...], sgn_ref, nlev, gs_scr, state_transposed=False)
        for u, (h, hs) in enumerate(zip(heads, lanes)):
            gate = gt_ref[:, hs].astype(F32)
            outs = []
            for bb in range(nb):
                o, st_new = res[u * nb + bb]
                sfin_ref[bb, h] = st_new
                outs.append(_gla_out(o[0:n_new], gon_ref[...], gate[bb * n_new:(bb + 1) * n_new]))
            o_ref[:, hs] = jnp.concatenate(outs, axis=0).astype(BF16)
        return carry

    lax.fori_loop(0, n_heads // hpi, body, 0)


def _gla_sample(q, k, g, v, gate, gon, s0, layer, row0, n_seq, n_new, name, states=None):
    W = q.shape[1]
    H = W // LANES
    L = s0.shape[0]
    nb = SAMPLE_PAD // n_new
    tri, lmap, sgn, nlev = _gla_consts(SAMPLE_PAD)
    blk0 = row0 // SAMPLE_PAD
    rspec = pl.BlockSpec((SAMPLE_PAD, W), lambda i: (blk0 + i, 0))
    s0spec = pl.BlockSpec((None, nb, H, LANES, LANES), lambda i: (layer, i, 0, 0, 0))
    in_specs = [rspec, rspec, rspec, rspec, rspec,
                pl.BlockSpec(tri.shape, lambda i: (0, 0)),
                pl.BlockSpec(lmap.shape, lambda i: (0, 0)),
                pl.BlockSpec(sgn.shape, lambda i: (0, 0, 0)),
                pl.BlockSpec((1, LANES), lambda i: (0, 0)),
                s0spec]
    args = [q, k, g, v, gate, tri, lmap, sgn, gon, s0]
    if states is None:
        sspec = pl.BlockSpec((L, nb, H, LANES, LANES), lambda i: (0, i, 0, 0, 0))
        aliases = {}
    else:
        sspec = s0spec
        in_specs.append(pl.BlockSpec(memory_space=pl.ANY))
        args.append(states)
        aliases = {len(args) - 1: 1}
    return pl.pallas_call(
        functools.partial(_gla_sample_kernel, nlev, H, n_new, layer, states is None),
        grid=(n_seq // nb,),
        in_specs=in_specs,
        out_specs=[pl.BlockSpec((SAMPLE_PAD, W), lambda i: (i, 0)), sspec],
        out_shape=[jax.ShapeDtypeStruct((n_seq * n_new, W), BF16),
                   jax.ShapeDtypeStruct((L, n_seq, H, LANES, LANES), F32)],
        input_output_aliases=aliases,
        scratch_shapes=[pltpu.VMEM((_heads_per_iter(H) * nb, SAMPLE_PAD, LANES), F32)],
        compiler_params=_cparams(("parallel",)),
        name=name,
    )(*args)


def _out_proj_kernel(split, x_is_pair, ap_ref, as_ref, w_ref, x_ref, *rest):
    xs_ref, o_ref = (rest[0], rest[1]) if x_is_pair else (None, rest[0])
    i = pl.program_id(0)
    last = pl.num_programs(0) - 1
    tail = lambda p_ref, s_ref: jnp.concatenate([p_ref[0:split, :], s_ref[...]], axis=0)

    @pl.when(i < last)
    def _():
        o_ref[...] = x_ref[...] + _wdot(ap_ref[...], w_ref)

    @pl.when(i == last)
    def _():
        x = tail(x_ref, xs_ref) if x_is_pair else x_ref[...]
        o_ref[...] = x + _wdot(tail(ap_ref, as_ref), w_ref)


def _out_proj(a_prompt, a_sample, w, x, tm, name):
    MP, K = a_prompt.shape
    MS = a_sample.shape[0]
    N = w.shape[1]
    M = MP + MS
    split = _row_split(MP, MS, tm)
    x_is_pair = isinstance(x, tuple)
    if x_is_pair:
        x_args = list(x)
        x_specs = [pl.BlockSpec((tm, N), lambda i: (i, 0)), pl.BlockSpec((MS, N), lambda i: (0, 0))]
    else:
        x_args, x_specs = [x], [pl.BlockSpec((tm, N), lambda i: (i, 0))]
    return pl.pallas_call(
        functools.partial(_out_proj_kernel, split, x_is_pair),
        grid=(M // tm,),
        in_specs=[pl.BlockSpec((tm, K), lambda i: (i, 0)),
                  pl.BlockSpec((MS, K), lambda i: (0, 0)),
                  pl.BlockSpec((K, N), lambda i: (0, 0))] + x_specs,
        out_specs=pl.BlockSpec((tm, N), lambda i: (i, 0)),
        out_shape=jax.ShapeDtypeStruct((M, N), F32),
        compiler_params=_cparams(("parallel",)),
        name=name,
    )(a_prompt, a_sample, w, *x_args)


def _ffn_kernel(split, x_ref, gn_ref, wg_ref, wu_ref, wd_ref, o_ref, *rest):
    ys_ref = rest[0] if split is not None else None
    h_scr = rest[-1]
    f = pl.program_id(1)

    @pl.when(f == 0)
    def _():
        _rms_to_scratch(x_ref, gn_ref, h_scr)
        o_ref[...] = x_ref[...]

    h = h_scr[...]
    g = _wdot(h, wg_ref)
    u = _wdot(h, wu_ref)
    a = (g * _sigmoid(g) * u).astype(BF16)
    o_ref[...] += _wdot(a, wd_ref)

    if split is not None:
        @pl.when((f == pl.num_programs(1) - 1) & (pl.program_id(0) == pl.num_programs(0) - 1))
        def _():
            ys_ref[...] = o_ref[split:, :]


def _ffn(x, gain, wg, wu, wd, tm, tf, name, prompt_rows=None):
    M, D = x.shape
    FF = wg.shape[1]
    o_spec = pl.BlockSpec((tm, D), lambda i, f: (i, 0))
    if prompt_rows is None:
        split, out_specs, out_shape = None, o_spec, jax.ShapeDtypeStruct((M, D), F32)
    else:
        split = prompt_rows % tm
        ms = M - prompt_rows
        assert tm - split == ms
        out_specs = [o_spec, pl.BlockSpec((ms, D), lambda i, f: (0, 0))]
        out_shape = [jax.ShapeDtypeStruct((prompt_rows, D), F32), jax.ShapeDtypeStruct((ms, D), F32)]
    return pl.pallas_call(
        functools.partial(_ffn_kernel, split),
        grid=(M // tm, FF // tf),
        in_specs=[pl.BlockSpec((tm, D), lambda i, f: (i, 0)),
                  pl.BlockSpec((1, D), lambda i, f: (0, 0)),
                  pl.BlockSpec((D, tf), lambda i, f: (0, f)),
                  pl.BlockSpec((D, tf), lambda i, f: (0, f)),
                  pl.BlockSpec((tf, D), lambda i, f: (f, 0))],
        out_specs=out_specs,
        out_shape=out_shape,
        scratch_shapes=[pltpu.VMEM((tm, D), BF16)],
        compiler_params=_cparams(("arbitrary", "arbitrary")),
        name=name,
    )(x, gain, wg, wu, wd)


MXU_COLS = 256


def _head_mean_square(acc):
    n = acc.shape[1]
    r = lax.broadcasted_iota(jnp.int32, (n, n), 0)
    c = lax.broadcasted_iota(jnp.int32, (n, n), 1)
    seg = jnp.where((r >> 6) == (c >> 6), 1.0 / ATTN_HEAD_DIM, 0.0).astype(BF16)
    return _dot((acc * acc).astype(BF16), seg)


def _norm_rope(acc, ms, gain, cos, sin, scale):
    y = acc * lax.rsqrt(ms + RMS_EPS) * gain
    lane = lax.broadcasted_iota(jnp.int32, y.shape, 1)
    half = ATTN_HEAD_DIM // 2
    rot = jnp.where((lane & half) == 0, pltpu.roll(y, LANES - half, 1), pltpu.roll(y, half, 1))
    out = y * cos + rot * sin
    return out * scale if scale != 1.0 else out


def _rope_cols(h, w_ref, hg_ref, cos_ref, sin_ref, o_ref, tn, scale):
    for n in range(w_ref.shape[1] // tn):
        acc = _dot(h, w_ref[:, n * tn:(n + 1) * tn])
        for c2 in range(tn // MXU_COLS):
            a2 = acc[:, c2 * MXU_COLS:(c2 + 1) * MXU_COLS]
            ms2 = _head_mean_square(a2)
            for c in range(MXU_COLS // LANES):
                cs = slice(c * LANES, (c + 1) * LANES)
                col = n * tn + c2 * MXU_COLS + c * LANES
                o_ref[:, col:col + LANES] = _norm_rope(a2[:, cs], ms2[:, cs], hg_ref[...], cos_ref[...],
                                                       sin_ref[...], scale).astype(BF16)


def _q_proj_kernel(tn, x_ref, gn_ref, w_ref, hg_ref, cos_ref, sin_ref, q_ref, h_scr):
    _rms_to_scratch(x_ref, gn_ref, h_scr)
    _rope_cols(h_scr[...], w_ref, hg_ref, cos_ref, sin_ref, q_ref, tn, ATTN_SCALE * LOG2E)


def _q_proj(x, gain, w, head_gain, cos, sin, tm, tn, name):
    M, D = x.shape
    N = w.shape[1]
    return pl.pallas_call(
        functools.partial(_q_proj_kernel, tn),
        grid=(M // tm,),
        in_specs=[pl.BlockSpec((tm, D), lambda i: (i, 0)),
                  pl.BlockSpec((1, D), lambda i: (0, 0)),
                  pl.BlockSpec((D, N), lambda i: (0, 0)),
                  pl.BlockSpec((1, LANES), lambda i: (0, 0)),
                  pl.BlockSpec((tm, LANES), lambda i: (i, 0)),
                  pl.BlockSpec((tm, LANES), lambda i: (i, 0))],
        out_specs=pl.BlockSpec((tm, N), lambda i: (i, 0)),
        out_shape=jax.ShapeDtypeStruct((M, N), BF16),
        scratch_shapes=[pltpu.VMEM((tm, D), BF16)],
        compiler_params=_cparams(("parallel",)),
        name=name,
    )(x, gain, w, head_gain, cos, sin)


def _kv_proj_kernel(tn, x_ref, gn_ref, wk_ref, wv_ref, hg_ref, cos_ref, sin_ref, k_ref, v_ref, h_scr):
    _rms_to_scratch(x_ref, gn_ref, h_scr)
    h = h_scr[...]
    _rope_cols(h, wk_ref, hg_ref, cos_ref, sin_ref, k_ref, tn, 1.0)
    for n in range(wv_ref.shape[1] // tn):
        ns = slice(n * tn, (n + 1) * tn)
        v_ref[:, ns] = _dot(h, wv_ref[:, ns]).astype(BF16)


def _kv_proj(x, gain, wk, wv, head_gain, cos, sin, tm, tn):
    M, D = x.shape
    N = wk.shape[1]
    wspec = pl.BlockSpec((D, N), lambda i: (0, 0))
    ospec = pl.BlockSpec((tm, N), lambda i: (i, 0))
    return pl.pallas_call(
        functools.partial(_kv_proj_kernel, tn),
        grid=(M // tm,),
        in_specs=[pl.BlockSpec((tm, D), lambda i: (i, 0)),
                  pl.BlockSpec((1, D), lambda i: (0, 0)),
                  wspec, wspec,
                  pl.BlockSpec((1, LANES), lambda i: (0, 0)),
                  pl.BlockSpec((tm, LANES), lambda i: (i, 0)),
                  pl.BlockSpec((tm, LANES), lambda i: (i, 0))],
        out_specs=[ospec, ospec],
        out_shape=[jax.ShapeDtypeStruct((M, N), BF16), jax.ShapeDtypeStruct((M, N), BF16)],
        scratch_shapes=[pltpu.VMEM((tm, D), BF16)],
        compiler_params=_cparams(("parallel",)),
        name="kv_proj",
    )(x, gain, wk, wv, head_gain, cos, sin)


def _softmax_sink(s_parts, sink2):
    m = sink2
    for s in s_parts:
        m = jnp.maximum(m, jnp.max(s, axis=-1, keepdims=True))
    den = jnp.exp2(sink2 - m)
    ps = []
    for s in s_parts:
        p = jnp.exp2(s - m)
        den = den + jnp.sum(p, axis=-1, keepdims=True)
        ps.append(p.astype(BF16))
    return ps, den


def _pv(ps, v_parts, den):
    o = None
    for p, v in zip(ps, v_parts):
        pv = _dot(p, v)
        o = pv if o is None else o + pv
    return o / den


def _swa_prompt_kernel(sk_ref, q_ref, kp_ref, kc_ref, vp_ref, vc_ref, o_ref):
    i = pl.program_id(1)
    W = WINDOW
    row = lax.broadcasted_iota(jnp.int32, (W, W), 0)
    col = lax.broadcasted_iota(jnp.int32, (W, W), 1)
    mask_prev = col > row + jnp.where(i > 0, 0, W)
    mask_cur = col <= row
    lo = lax.broadcasted_iota(jnp.int32, (W, LANES), 1) < ATTN_HEAD_DIM
    sink_col = col == 0
    first_row = lax.broadcasted_iota(jnp.int32, (W, LANES), 0) == 0

    def scores(g):
        gs = slice(g * LANES, (g + 1) * LANES)
        kd = jnp.concatenate([kp_ref[:, gs], kc_ref[:, gs]], axis=0)
        lhs = []
        for c in range(GQA_GROUP // 2):
            qc = q_ref[:, (2 * g + c) * LANES:(2 * g + c + 1) * LANES]
            zero = jnp.zeros_like(qc)
            lhs += [jnp.where(lo, qc, zero), jnp.where(lo, zero, qc)]
        s = _dot_nt(jnp.concatenate(lhs, axis=0), kd)
        out = []
        for a in range(GQA_GROUP):
            sa = s[a * W:(a + 1) * W]
            fill = jnp.where(sink_col, sk_ref[GQA_GROUP * g + a] * LOG2E, MASK_VALUE)
            out.append(jnp.concatenate([jnp.where(mask_prev, sa[:, :W], fill),
                                        jnp.where(mask_cur, sa[:, W:], MASK_VALUE)], axis=1))
        return out

    def finish(g, s_heads):
        gs = slice(g * LANES, (g + 1) * LANES)
        vprev = vp_ref[:, gs]
        vd = jnp.concatenate([jnp.where(first_row, jnp.zeros_like(vprev), vprev), vc_ref[:, gs]], axis=0)
        ps, dens = [], []
        for s in s_heads:
            m = jnp.max(s, axis=-1, keepdims=True)
            p = jnp.exp2(s - m)
            dens.append(jnp.sum(p, axis=-1, keepdims=True))
            ps.append(p.astype(BF16))
        o = _dot(jnp.concatenate(ps, axis=0), vd)
        for c in range(GQA_GROUP // 2):
            cs = slice((2 * g + c) * LANES, (2 * g + c + 1) * LANES)
            o0 = o[(2 * c) * W:(2 * c + 1) * W] / dens[2 * c]
            o1 = o[(2 * c + 1) * W:(2 * c + 2) * W] / dens[2 * c + 1]
            o_ref[:, cs] = jnp.where(lo, o0, o1).astype(BF16)

    s_next = scores(0)
    for g in range(ATTN_KV_HEADS):
        s_cur = s_next
        if g + 1 < ATTN_KV_HEADS:
            s_next = scores(g + 1)
        finish(g, s_cur)


def _swa_prompt(q, kdup, vdup, sinks, batch, seq, name):
    W = WINDOW
    nb = seq // W
    QW = q.shape[1]
    KW = kdup.shape[1]
    prev = lambda b, i: (b * nb + jnp.maximum(i - 1, 0), 0)
    cur = lambda b, i: (b * nb + i, 0)
    return pl.pallas_call(
        _swa_prompt_kernel,
        grid=(batch, nb),
        in_specs=[pl.BlockSpec(memory_space=pltpu.SMEM),
                  pl.BlockSpec((W, QW), cur),
                  pl.BlockSpec((W, KW), prev), pl.BlockSpec((W, KW), cur),
                  pl.BlockSpec((W, KW), prev), pl.BlockSpec((W, KW), cur)],
        out_specs=pl.BlockSpec((W, QW), cur),
        out_shape=jax.ShapeDtypeStruct((batch * seq, QW), BF16),
        compiler_params=_cparams(("parallel", "arbitrary")),
        name=name,
    )(sinks, q, kdup, kdup, vdup, vdup)


def _swa_sample_kernel(n_new, sk_ref, q_ref, kn_ref, vn_ref, kc_ref, vc_ref, o_ref):
    W = WINDOW
    nb = SAMPLE_PAD // n_new
    rows = GQA_GROUP * n_new
    qi = lax.broadcasted_iota(jnp.int32, (rows, W), 0) & (n_new - 1)
    mask_c = lax.broadcasted_iota(jnp.int32, (rows, W), 1) > qi
    qi_n = lax.broadcasted_iota(jnp.int32, (rows, SAMPLE_PAD), 0) & (n_new - 1)
    mask_n = lax.broadcasted_iota(jnp.int32, (rows, SAMPLE_PAD), 1) <= qi_n
    hrow = lax.broadcasted_iota(jnp.int32, (rows, 1), 0) // n_new
    lo = lax.broadcasted_iota(jnp.int32, (n_new, LANES), 1) < ATTN_HEAD_DIM
    zq = jnp.zeros((n_new, LANES), F32)
    zpad = jnp.zeros((SAMPLE_PAD - n_new, LANES), F32)
    q = q_ref[...].astype(F32)
    kn = kn_ref[...].astype(F32)
    vn = vn_ref[...].astype(F32)
    chains = [(bb, g) for bb in range(nb) for g in range(ATTN_KV_HEADS)]
    sinks2 = []
    for g in range(ATTN_KV_HEADS):
        sink = jnp.zeros((rows, 1), F32)
        for a in range(GQA_GROUP):
            sink = jnp.where(hrow == a, sk_ref[GQA_GROUP * g + a] * LOG2E, sink)
        sinks2.append(sink)
    scores, values = [], []
    for bb, g in chains:
        rs = slice(bb * n_new, (bb + 1) * n_new)
        gs = slice(g * LANES, (g + 1) * LANES)
        kn_g = jnp.concatenate([kn[rs, gs], zpad], axis=0).astype(BF16)
        vn_g = jnp.concatenate([vn[rs, gs], zpad], axis=0).astype(BF16)
        lhs = []
        for a in range(GQA_GROUP):
            c, par = divmod(a, 2)
            qc = q[rs, (2 * g + c) * LANES:(2 * g + c + 1) * LANES]
            lhs.append(jnp.where(lo, qc, zq) if par == 0 else jnp.where(lo, zq, qc))
        lhs = jnp.concatenate(lhs, axis=0).astype(BF16)
        s_c = jnp.where(mask_c, _dot_nt(lhs, kc_ref[bb, :, gs]), MASK_VALUE)
        s_n = jnp.where(mask_n, _dot_nt(lhs, kn_g), MASK_VALUE)
        scores.append([s_c, s_n])
        values.append([vc_ref[bb, :, gs], vn_g])
    probs = [_softmax_sink(s, sinks2[g]) for s, (bb, g) in zip(scores, chains)]
    outs = [_pv(ps, v, den) for (ps, den), v in zip(probs, values)]
    out_rows = []
    for bb in range(nb):
        chunks = []
        for g in range(ATTN_KV_HEADS):
            o = outs[bb * ATTN_KV_HEADS + g]
            for c in range(GQA_GROUP // 2):
                chunks.append(jnp.where(lo, o[(2 * c) * n_new:(2 * c + 1) * n_new],
                                        o[(2 * c + 1) * n_new:(2 * c + 2) * n_new]))
        out_rows.append(jnp.concatenate(chunks, axis=1))
    o_ref[...] = jnp.concatenate(out_rows, axis=0).astype(BF16)


def _swa_sample(q, kdup, vdup, cache_kdup, cache_vdup, sinks, row0, n_seq, n_new, name):
    W = WINDOW
    QW = q.shape[1]
    KW = kdup.shape[1]
    nb = SAMPLE_PAD // n_new
    blk0 = row0 // SAMPLE_PAD
    rmap = lambda i: (blk0 + i, 0)
    cspec = pl.BlockSpec((nb, W, KW), lambda i: (i, 0, 0))
    return pl.pallas_call(
        functools.partial(_swa_sample_kernel, n_new),
        grid=(n_seq // nb,),
        in_specs=[pl.BlockSpec(memory_space=pltpu.SMEM),
                  pl.BlockSpec((SAMPLE_PAD, QW), rmap),
                  pl.BlockSpec((SAMPLE_PAD, KW), rmap), pl.BlockSpec((SAMPLE_PAD, KW), rmap),
                  cspec, cspec],
        out_specs=pl.BlockSpec((SAMPLE_PAD, QW), lambda i: (i, 0)),
        out_shape=jax.ShapeDtypeStruct((n_seq * n_new, QW), BF16),
        compiler_params=_cparams(("parallel",)),
        name=name,
    )(sinks, q, kdup, vdup, cache_kdup, cache_vdup)


def _dup_heads(w):
    lead = w.shape[:-1]
    w3 = w.reshape(lead + (ATTN_KV_HEADS, ATTN_HEAD_DIM))
    return jnp.concatenate([w3, w3], axis=-1).reshape(lead + (ATTN_KV_HEADS * 2 * ATTN_HEAD_DIM,))


def _undup_heads(a, lead):
    return a.reshape(lead + (ATTN_KV_HEADS, 2 * ATTN_HEAD_DIM))[..., :ATTN_HEAD_DIM].astype(F32)


def _rope_tables(pos):
    half = ATTN_HEAD_DIM // 2
    inv = ROPE_THETA ** (-jnp.arange(half, dtype=F32) / half)
    ang = pos.astype(F32)[:, None] * inv[None, :]
    cos = jnp.cos(ang)
    sin = jnp.sin(ang)
    reps = LANES // ATTN_HEAD_DIM
    cos_t = jnp.tile(jnp.concatenate([cos, cos], axis=-1), (1, reps))
    sin_t = jnp.tile(jnp.concatenate([-sin, sin], axis=-1), (1, reps))
    return cos_t, sin_t


def kernel(x_prompt, x_sample, state_hgrn, cache_k_win, cache_v_win, hgrn_norm, hgrn_wq, hgrn_wf, hgrn_wi, hgrn_wg, hgrn_lb_logits, hgrn_onorm, hgrn_wo, kv_norm, w_k, w_v, k_norm, attn_norm, attn_wq, q_norm, sinks, attn_wo, ffn_norm, w_gate, w_up, w_down):
    B, T, D = x_prompt.shape
    SB, S, _ = x_sample.shape
    MP = B * T
    MS = SB * S
    n_a = hgrn_wq.shape[0]
    n_b = attn_wq.shape[0]
    assert SAMPLE_PAD % S == 0 and SB % (SAMPLE_PAD // S) == 0 and MP % SAMPLE_PAD == 0
    assert T % GLA_CHUNK == 0 and T % WINDOW == 0

    x = (x_prompt.reshape(MP, D).astype(F32), x_sample.reshape(MS, D).astype(F32))
    M = MP + MS
    tm = 768 if M % 768 == 0 else 256
    tn = 512
    tf = 512

    bf = lambda w: w.astype(BF16)
    f32 = lambda w: w.astype(F32)
    row = lambda g: g.reshape(1, -1).astype(F32)
    pos = jnp.concatenate([jnp.tile(jnp.arange(T), B), jnp.tile(PAST_LEN + jnp.arange(S), SB)])
    cos_t, sin_t = _rope_tables(pos)
    head_gain = lambda g: jnp.tile(g.astype(F32), LANES // ATTN_HEAD_DIM).reshape(1, LANES)

    w_gate32, w_up32, w_down32 = f32(w_gate), f32(w_up), f32(w_down)
    hgrn_w32 = [f32(hgrn_wq), f32(hgrn_wf), f32(hgrn_wi), f32(hgrn_wg)]
    hgrn_wo32, attn_wq32, attn_wo32 = f32(hgrn_wo), f32(attn_wq), f32(attn_wo)
    ffn_riders = lambda l: [(w_gate32, l), (w_up32, l), (w_down32, l)]

    depth = n_a + n_b
    ffn_bf = {}
    hosted = {l: list(range(l * depth // n_a, (l + 1) * depth // n_a)) for l in range(n_a)}
    st_prompt, st_sample = [], None
    s0 = f32(state_hgrn)
    assert n_a >= 1
    proj_w = _cast_weights([(w, 0) for w in hgrn_w32], 16, "cast_hgrn_0")
    attn_wq_bf, attn_wo_bf = [], []
    for l in range(n_a):
        q, k, lf, v, gate = _hgrn_proj(x, row(hgrn_norm[l]), *proj_w, f32(hgrn_lb_logits), l, tm, tn)
        gon = row(hgrn_onorm[l])
        riders = [(hgrn_wo32, l)] + [r for fl in hosted[l] for r in ffn_riders(fl)]
        if l + 1 < n_a:
            riders += [(w, l + 1) for w in hgrn_w32]
        else:
            riders += [(attn_wq32, j) for j in range(n_b)] + [(attn_wo32, j) for j in range(n_b)]
        o_p, s_p, wo_bf, *cast = _gla_prompt(q, k, lf, v, gate, gon, B, T, riders, f"gla_prompt_{l}")
        for fl in hosted[l]:
            ffn_bf[fl], cast = cast[:3], cast[3:]
        if l + 1 < n_a:
            proj_w = cast
        else:
            attn_wq_bf, attn_wo_bf = cast[:n_b], cast[n_b:]
        o_s, st_sample = _gla_sample(q, k, lf, v, gate, gon, s0, l, MP, SB, S, f"gla_sample_{l}",
                                     states=st_sample)
        st_prompt.append(s_p)
        x = _out_proj(o_p, o_s, wo_bf, x, tm, f"hgrn_out_{l}")
        x = _ffn(x, row(ffn_norm[l]), *ffn_bf.pop(l), tm, tf, f"ffn_{l}")

    kdup, vdup = _kv_proj(x, row(kv_norm), bf(_dup_heads(w_k)), bf(_dup_heads(w_v)), head_gain(k_norm),
                          cos_t, sin_t, tm, tn)
    KW = ATTN_KV_HEADS * ATTN_HEAD_DIM
    cache_kdup = bf(_dup_heads(cache_k_win.reshape(SB, WINDOW, KW)))
    cache_vdup = bf(_dup_heads(cache_v_win.reshape(SB, WINDOW, KW)))
    for j in range(n_b):
        l = n_a + j
        q = _q_proj(x, row(attn_norm[j]), attn_wq_bf[j], head_gain(q_norm[j]), cos_t, sin_t, tm, tn,
                    f"q_proj_{j}")
        sk = sinks[j].astype(F32)
        a_p = _swa_prompt(q, kdup, vdup, sk, B, T, f"swa_prompt_{j}")
        a_s = _swa_sample(q, kdup, vdup, cache_kdup, cache_vdup, sk, MP, SB, S, f"swa_sample_{j}")
        x = _out_proj(a_p, a_s, attn_wo_bf[j], x, tm, f"attn_out_{j}")
        last = j == n_b - 1
        x = _ffn(x, row(ffn_norm[l]), *ffn_bf.pop(l), tm, tf, f"ffn_{l}", prompt_rows=MP if last else None)

    y_prompt = x[0].reshape(B, T, D)
    y_sample = x[1].reshape(SB, S, D)
    k_win_p = _undup_heads(kdup[:MP].reshape(B, T, -1)[:, T - WINDOW:], (B, WINDOW))
    v_win_p = _undup_heads(vdup[:MP].reshape(B, T, -1)[:, T - WINDOW:], (B, WINDOW))
    k_new_s = _undup_heads(kdup[MP:].reshape(SB, S, -1), (SB, S))
    v_new_s = _undup_heads(vdup[MP:].reshape(SB, S, -1), (SB, S))
    k_win_s = jnp.concatenate([cache_k_win[:, S:].astype(F32), k_new_s], axis=1)
    v_win_s = jnp.concatenate([cache_v_win[:, S:].astype(F32), v_new_s], axis=1)
    return (y_prompt, y_sample, jnp.stack(st_prompt), st_sample,
            k_win_p, v_win_p, k_win_s, v_win_s)
```

```python
import functools
import math

import numpy as np
import jax
import jax.numpy as jnp
from jax import lax
from jax.experimental import pallas as pl
from jax.experimental.pallas import tpu as pltpu

HGRN_HEADS = 16
HEAD_DK = 128
ATTN_HEAD_DIM = 64
ATTN_KV_HEADS = 8
GQA_GROUP = 4
WINDOW = 128
ROPE_THETA = 10000.0
ATTN_SCALE = ATTN_HEAD_DIM ** -0.5
LOG2E = math.log2(math.e)
RMS_EPS = 1e-6
MASK_VALUE = -1e30
MIN_FORGET = 1e-30
PAST_LEN = 16384

LANES = 128
VMEM_LIMIT_BYTES = 56 * 1024 * 1024

GLA_CHUNK = 128
SAMPLE_PAD = 16
GLA_HEADS_PER_ITER = 16


def _heads_per_iter(n_heads):
    return math.gcd(n_heads, GLA_HEADS_PER_ITER)

F32 = jnp.float32
BF16 = jnp.bfloat16


def _cparams(sem):
    return pltpu.CompilerParams(dimension_semantics=sem, vmem_limit_bytes=VMEM_LIMIT_BYTES)


def _dot(a, b):
    return jnp.dot(a, b, preferred_element_type=F32)


def _dot_nt(a, b):
    return lax.dot_general(a, b, (((1,), (1,)), ((), ())), preferred_element_type=F32)


def _dot_tn(a, b):
    return lax.dot_general(a, b, (((0,), (0,)), ((), ())), preferred_element_type=F32)


def _wdot(a, w_ref):
    return _dot(a, w_ref[...].astype(BF16))


def _wspec_cols(w, layer, tn):
    k = w.shape[-2]
    if w.ndim == 3:
        return pl.BlockSpec((None, k, tn), lambda i, n: (layer, 0, n))
    return pl.BlockSpec((k, tn), lambda i, n: (0, n))


def _sigmoid(x):
    return 0.5 * jnp.tanh(0.5 * x) + 0.5


def _rider_specs(riders, steps_per_batch, n_steps):
    in_specs, out_specs, out_shapes = [], [], []
    for w, layer in riders:
        _, R, N = w.shape
        rb = next(r for r in range(16, R + 1, 16) if R % r == 0 and R // r <= n_steps)
        last = R // rb - 1
        blk = lambda b, c, last=last: jnp.minimum(b * steps_per_batch + c, last)
        in_specs.append(pl.BlockSpec((None, rb, N), lambda b, c, layer=layer, blk=blk: (layer, blk(b, c), 0)))
        out_specs.append(pl.BlockSpec((rb, N), lambda b, c, blk=blk: (blk(b, c), 0)))
        out_shapes.append(jax.ShapeDtypeStruct((R, N), BF16))
    return in_specs, out_specs, out_shapes


def _run_riders(in_refs, out_refs):
    for i_ref, o_ref in zip(in_refs, out_refs):
        o_ref[...] = i_ref[...].astype(BF16)


def _cast_kernel(n, *refs):
    _run_riders(refs[:n], refs[n:])


def _cast_weights(riders, n_steps, name):
    r_in, r_out, r_shape = _rider_specs(riders, n_steps, n_steps)
    return pl.pallas_call(
        functools.partial(_cast_kernel, len(riders)),
        grid=(1, n_steps),
        in_specs=r_in, out_specs=r_out, out_shape=r_shape,
        compiler_params=_cparams(("arbitrary", "arbitrary")),
        name=name,
    )(*[w for w, _ in riders])


def _rms_store(x, g_ref, h_scr):
    ms = jnp.mean(x * x, axis=-1, keepdims=True)
    h_scr[...] = (x * lax.rsqrt(ms + RMS_EPS) * g_ref[...]).astype(BF16)


def _rms_to_scratch(x_ref, g_ref, h_scr):
    _rms_store(x_ref[...], g_ref, h_scr)


def _row_split(prompt_rows, sample_rows, tm):
    split = prompt_rows % tm
    assert (prompt_rows + sample_rows) % tm == 0 and tm - split == sample_rows
    return split


def _on_row_tile(split, p_ref, s_ref, fn):
    i = pl.program_id(0)
    last = pl.num_programs(0) - 1

    @pl.when(i < last)
    def _():
        fn(p_ref[...])

    @pl.when(i == last)
    def _():
        fn(jnp.concatenate([p_ref[0:split, :], s_ref[...]], axis=0))


def _hgrn_proj_kernel(layer, split, x_ref, *rest):
    xs_ref, rest = (rest[0], rest[1:]) if split is not None else (None, rest)
    gn_ref, wq_ref, wf_ref, wi_ref, wg_ref, lbl_ref, q_ref, k_ref, lf_ref, v_ref, gt_ref, h_scr = rest

    @pl.when(pl.program_id(1) == 0)
    def _():
        if split is None:
            _rms_to_scratch(x_ref, gn_ref, h_scr)
        else:
            _on_row_tile(split, x_ref, xs_ref, lambda x: _rms_store(x, gn_ref, h_scr))

    h = h_scr[...]
    aq = _wdot(h, wq_ref)
    q_ref[...] = (aq * _sigmoid(aq) * (HEAD_DK ** -0.5)).astype(BF16)

    lg = lbl_ref[...]
    mx = jnp.max(lg, axis=0, keepdims=True)
    e = jnp.exp(lg - mx)
    sm = e / jnp.sum(e, axis=0, keepdims=True)
    cs = sm[0:1]
    for r in range(1, layer + 1):
        cs = cs + sm[r:r + 1]
    lb = cs - sm[0:1]

    z = _wdot(h, wf_ref)
    ez = jnp.exp(-jnp.abs(z))
    r = 1.0 / (1.0 + ez)
    pos = z >= 0.0
    sig_p = jnp.where(pos, r, ez * r)
    sig_n = jnp.where(pos, ez * r, r)
    f = lb + (1.0 - lb) * sig_p
    lf_ref[...] = jnp.log(jnp.maximum(f, MIN_FORGET))
    k_ref[...] = ((1.0 - lb) * sig_n).astype(BF16)

    ag = _wdot(h, wg_ref)
    gt_ref[...] = (ag * _sigmoid(ag)).astype(BF16)
    v_ref[...] = _wdot(h, wi_ref).astype(BF16)


def _hgrn_proj(x, gain, wq, wf, wi, wg, lb_logits, layer, tm, tn):
    N = wq.shape[-1]
    L = lb_logits.shape[0]
    if isinstance(x, tuple):
        xp, xs = x
        D = xp.shape[1]
        M = xp.shape[0] + xs.shape[0]
        split = _row_split(xp.shape[0], xs.shape[0], tm)
        x_args = [xp, xs]
        x_specs = [pl.BlockSpec((tm, D), lambda i, n: (i, 0)), pl.BlockSpec(xs.shape, lambda i, n: (0, 0))]
    else:
        M, D = x.shape
        split, x_args, x_specs = None, [x], [pl.BlockSpec((tm, D), lambda i, n: (i, 0))]
    grid = (M // tm, N // tn)
    wspec = _wspec_cols(wq, layer, tn)
    ospec = pl.BlockSpec((tm, tn), lambda i, n: (i, n))
    return pl.pallas_call(
        functools.partial(_hgrn_proj_kernel, layer, split),
        grid=grid,
        in_specs=x_specs + [
                  pl.BlockSpec((1, D), lambda i, n: (0, 0)),
                  wspec, wspec, wspec, wspec,
                  pl.BlockSpec((L, tn), lambda i, n: (0, n))],
        out_specs=[ospec] * 5,
        out_shape=[jax.ShapeDtypeStruct((M, N), BF16),
                   jax.ShapeDtypeStruct((M, N), BF16),
                   jax.ShapeDtypeStruct((M, N), F32),
                   jax.ShapeDtypeStruct((M, N), BF16),
                   jax.ShapeDtypeStruct((M, N), BF16)],
        scratch_shapes=[pltpu.VMEM((tm, D), BF16)],
        compiler_params=_cparams(("parallel", "arbitrary")),
        name=f"hgrn_proj_{layer}",
    )(*x_args, gain, wq, wf, wi, wg, lb_logits)


def _gla_consts(C):
    nlev = int(math.log2(C))
    tri = np.tril(np.ones((C, C), np.float32))
    ii, jj = np.meshgrid(np.arange(C), np.arange(C), indexing="ij")
    x = ii ^ jj
    lev = np.floor(np.log2(np.maximum(x, 1))).astype(np.int32)
    lmap = np.where(ii == jj, -1, np.where(ii > jj, lev, -2)).astype(np.int32)
    rows = np.arange(C)
    sgn = np.stack([np.where((rows >> l) & 1 == 1, LOG2E if l == 0 else 1.0, 0.0 if l == 0 else -1.0)
                    for l in range(nlev)])
    sgn = np.broadcast_to(sgn[:, :, None], (nlev, C, LANES)).astype(np.float32)
    return jnp.asarray(tri, BF16), jnp.asarray(lmap, jnp.int32), jnp.asarray(sgn), nlev


SUBLANES = 8


def _level_exponent(l, g, g2, gs_ref, sgn):
    C = g.shape[0]
    h = 1 << l
    if l == 0:
        return g * sgn
    sub = lax.broadcasted_iota(jnp.int32, (SUBLANES, LANES), 0)
    bcast = lambda r: jnp.broadcast_to(gs_ref[pl.ds(r, 1), :], (SUBLANES, LANES))
    pieces = []
    for base in range(0, C, SUBLANES):
        if 2 * h >= SUBLANES:
            pieces.append(bcast((base // (2 * h)) * 2 * h + h - 1))
        else:
            p = bcast(base + h - 1)
            for blk in range(1, SUBLANES // (2 * h)):
                p = jnp.where(sub >= blk * 2 * h, bcast(base + blk * 2 * h + h - 1), p)
            pieces.append(p)
    gr = jnp.concatenate(pieces, axis=0)
    return (g2 - gr) * sgn


BF16_ROWS = 16


def _gla_level(l, q, k, g, g2, gs_ref, sgn_ref, lmap_b, acc):
    C = g.shape[0]
    h = 1 << l
    rb = BF16_ROWS
    if h < rb:
        e = jnp.exp2(_level_exponent(l, g, g2, gs_ref, sgn_ref[l]))
        al = _dot_nt((q.astype(F32) * e).astype(BF16), (k.astype(F32) * e).astype(BF16))
        return [jnp.where(lmap_b[i] == l, al[i * rb:(i + 1) * rb], a) for i, a in enumerate(acc)]
    bcast = lambda r: jnp.broadcast_to(gs_ref[pl.ds(r, 1), :], (h, LANES))
    lhs, rhs, upper_blocks = [], [], []
    for b in range(C // h):
        rows = slice(b * h, (b + 1) * h)
        if b % 2 == 1:
            e = jnp.exp2(g2[rows] - bcast(b * h - 1))
            lhs.append((q[rows].astype(F32) * e).astype(BF16))
            rhs.append(k[rows])
            upper_blocks += list(range(b * h // rb, (b + 1) * h // rb))
        else:
            e = jnp.exp2(bcast((b + 1) * h - 1) - g2[rows])
            rhs.append((k[rows].astype(F32) * e).astype(BF16))
    al = _dot_nt(jnp.concatenate(lhs, axis=0), jnp.concatenate(rhs, axis=0))
    acc = list(acc)
    for j, i in enumerate(upper_blocks):
        acc[i] = jnp.where(lmap_b[i] == l, al[j * rb:(j + 1) * rb], acc[i])
    return acc


def _column_broadcast(row):
    n = row.shape[1]
    eye = lax.broadcasted_iota(jnp.int32, (n, n), 0) == lax.broadcasted_iota(jnp.int32, (n, n), 1)
    x = jnp.where(eye, jnp.broadcast_to(row, (n, n)), 0.0)
    xh = x.astype(BF16)
    xl = (x - xh.astype(F32)).astype(BF16)
    ones = jnp.ones((n, n), BF16)
    return _dot(xh, ones) + _dot(xl, ones)


def _gla_core(chains, tri, lmap, sgn_ref, nlev, gs_scr, state_transposed=True):
    C = chains[0][0].shape[0]
    rb = BF16_ROWS
    lmap_b = [lmap[i * rb:(i + 1) * rb] for i in range(C // rb)]
    g2s = []
    for u, (q, k, g, v, st) in enumerate(chains):
        ghi = g.astype(BF16)
        glo = (g - ghi.astype(F32)).astype(BF16)
        c2 = _dot(tri, jnp.concatenate([ghi, glo], axis=1))
        g2 = (c2[:, :LANES] + c2[:, LANES:]) * LOG2E
        gs_scr[u] = g2
        g2s.append(g2)
    accs = []
    for (q, k, g, v, st) in chains:
        d = _dot_nt(q, k)
        accs.append([jnp.where(lmap_b[i] == -1, d[i * rb:(i + 1) * rb], 0.0) for i in range(C // rb)])
    for l in range(nlev):
        for u, (q, k, g, v, st) in enumerate(chains):
            accs[u] = _gla_level(l, q, k, g, g2s[u], gs_scr.at[u], sgn_ref, lmap_b, accs[u])
    outs = []
    for u, (q, k, g, v, st) in enumerate(chains):
        g2 = g2s[u]
        o = _dot(jnp.concatenate(accs[u], axis=0).astype(BF16), v)
        qdec = (q.astype(F32) * jnp.exp2(g2)).astype(BF16)
        glast = g2[C - 1:C, :]
        kdec = (k.astype(F32) * jnp.exp2(glast - g2)).astype(BF16)
        if state_transposed:
            o = o + _dot_nt(qdec, st.astype(BF16))
            st_new = st * jnp.exp2(glast) + _dot_tn(v, kdec)
        else:
            o = o + _dot(qdec, st.astype(BF16))
            st_new = st * _column_broadcast(jnp.exp2(glast)) + _dot_tn(kdec, v)
        outs.append((o, st_new))
    return outs


def _gla_out(o, gon, gate):
    ms = jnp.mean(o * o, axis=-1, keepdims=True)
    return o * lax.rsqrt(ms + RMS_EPS) * gon * gate


def _gla_prompt_kernel(nlev, n_heads, n_riders, q_ref, k_ref, g_ref, v_ref, gt_ref, tri_ref, lmap_ref,
                       sgn_ref, gon_ref, *rest):
    rider_in, rest = rest[:n_riders], rest[n_riders:]
    o_ref, sfin_ref = rest[0], rest[1]
    rider_out = rest[2:2 + n_riders]
    st_scr, gs_scr = rest[2 + n_riders:]
    _run_riders(rider_in, rider_out)
    c = pl.program_id(1)
    hpi = _heads_per_iter(n_heads)
    n_iter = n_heads // hpi

    @pl.when(c == 0)
    def _():
        st_scr[...] = jnp.zeros_like(st_scr)

    def body(j, carry):
        heads = [j * hpi + u for u in range(hpi)]
        lanes = [pl.ds(pl.multiple_of(h * LANES, LANES), LANES) for h in heads]
        chains = [(q_ref[:, hs], k_ref[:, hs], g_ref[:, hs], v_ref[:, hs], st_scr[h])
                  for h, hs in zip(heads, lanes)]
        res = _gla_core(chains, tri_ref[...], lmap_ref[...], sgn_ref, nlev, gs_scr)
        for h, hs, (o, st_new) in zip(heads, lanes, res):
            st_scr[h] = st_new
            o_ref[:, hs] = _gla_out(o, gon_ref[...], gt_ref[:, hs].astype(F32)).astype(BF16)
        return carry

    lax.fori_loop(0, n_iter, body, 0)

    @pl.when(c == pl.num_programs(1) - 1)
    def _():
        def wb(h, carry):
            sfin_ref[0, h] = st_scr[h].T
            return carry
        lax.fori_loop(0, n_heads, wb, 0)


def _gla_prompt(q, k, g, v, gate, gon, batch, seq, riders, name):
    C = GLA_CHUNK
    W = q.shape[1]
    H = W // LANES
    nc = seq // C
    tri, lmap, sgn, nlev = _gla_consts(C)
    rspec = pl.BlockSpec((C, W), lambda b, c: (b * nc + c, 0))
    r_in, r_out, r_shape = _rider_specs(riders, nc, batch * nc)
    return pl.pallas_call(
        functools.partial(_gla_prompt_kernel, nlev, H, len(riders)),
        grid=(batch, nc),
        in_specs=[rspec, rspec, rspec, rspec, rspec,
                  pl.BlockSpec(tri.shape, lambda b, c: (0, 0)),
                  pl.BlockSpec(lmap.shape, lambda b, c: (0, 0)),
                  pl.BlockSpec(sgn.shape, lambda b, c: (0, 0, 0)),
                  pl.BlockSpec((1, LANES), lambda b, c: (0, 0))] + r_in,
        out_specs=[rspec, pl.BlockSpec((1, H, LANES, LANES), lambda b, c: (b, 0, 0, 0))] + r_out,
        out_shape=[jax.ShapeDtypeStruct((batch * seq, W), BF16),
                   jax.ShapeDtypeStruct((batch, H, LANES, LANES), F32)] + r_shape,
        scratch_shapes=[pltpu.VMEM((H, LANES, LANES), F32),
                        pltpu.VMEM((_heads_per_iter(H), C, LANES), F32)],
        compiler_params=_cparams(("arbitrary", "arbitrary")),
        name=name,
    )(q, k, g, v, gate, tri, lmap, sgn, gon, *[w for w, _ in riders])


def _gla_sample_kernel(nlev, n_heads, n_new, layer, creates_states, q_ref, k_ref, g_ref, v_ref, gt_ref,
                       tri_ref, lmap_ref, sgn_ref, gon_ref, s0_ref, *rest):
    if creates_states:
        o_ref, sall_ref, gs_scr = rest
        for other in range(sall_ref.shape[0]):
            if other != layer:
                sall_ref[other] = jnp.zeros(sall_ref.shape[1:], F32)
        sfin_ref = sall_ref.at[layer]
    else:
        _, o_ref, sfin_ref, gs_scr = rest
    nb = SAMPLE_PAD // n_new
    zpad = jnp.zeros((SAMPLE_PAD - n_new, LANES), F32)
    hpi = _heads_per_iter(n_heads)

    def body(j, carry):
        heads = [j * hpi + u for u in range(hpi)]
        lanes = [pl.ds(pl.multiple_of(h * LANES, LANES), LANES) for h in heads]
        chains = []
        for h, hs in zip(heads, lanes):
            q = q_ref[:, hs].astype(F32)
            k = k_ref[:, hs].astype(F32)
            g = g_ref[:, hs]
            v = v_ref[:, hs].astype(F32)
            for bb in range(nb):
                rs = slice(bb * n_new, (bb + 1) * n_new)
                pad = lambda t: jnp.concatenate([t[rs], zpad], axis=0)
                chains.append((pad(q).astype(BF16), pad(k).astype(BF16), pad(g), pad(v).astype(BF16),
                               s0_ref[bb, h]))
        res = _gla_core(chains, tri_ref[...], lmap_ref[...], sgn_ref, nlev, gs_scr, state_transposed=False)
        for u, (h, hs) in enumerate(zip(heads, lanes)):
            gate = gt_ref[:, hs].astype(F32)
            outs = []
            for bb in range(nb):
                o, st_new = res[u * nb + bb]
                sfin_ref[bb, h] = st_new
                outs.append(_gla_out(o[0:n_new], gon_ref[...], gate[bb * n_new:(bb + 1) * n_new]))
            o_ref[:, hs] = jnp.concatenate(outs, axis=0).astype(BF16)
        return carry

    lax.fori_loop(0, n_heads // hpi, body, 0)


def _gla_sample(q, k, g, v, gate, gon, s0, layer, row0, n_seq, n_new, name, states=None):
    W = q.shape[1]
    H = W // LANES
    L = s0.shape[0]
    nb = SAMPLE_PAD // n_new
    tri, lmap, sgn, nlev = _gla_consts(SAMPLE_PAD)
    blk0 = row0 // SAMPLE_PAD
    rspec = pl.BlockSpec((SAMPLE_PAD, W), lambda i: (blk0 + i, 0))
    s0spec = pl.BlockSpec((None, nb, H, LANES, LANES), lambda i: (layer, i, 0, 0, 0))
    in_specs = [rspec, rspec, rspec, rspec, rspec,
                pl.BlockSpec(tri.shape, lambda i: (0, 0)),
                pl.BlockSpec(lmap.shape, lambda i: (0, 0)),
                pl.BlockSpec(sgn.shape, lambda i: (0, 0, 0)),
                pl.BlockSpec((1, LANES), lambda i: (0, 0)),
                s0spec]
    args = [q, k, g, v, gate, tri, lmap, sgn, gon, s0]
    if states is None:
        sspec = pl.BlockSpec((L, nb, H, LANES, LANES), lambda i: (0, i, 0, 0, 0))
        aliases = {}
    else:
        sspec = s0spec
        in_specs.append(pl.BlockSpec(memory_space=pl.ANY))
        args.append(states)
        aliases = {len(args) - 1: 1}
    return pl.pallas_call(
        functools.partial(_gla_sample_kernel, nlev, H, n_new, layer, states is None),
        grid=(n_seq // nb,),
        in_specs=in_specs,
        out_specs=[pl.BlockSpec((SAMPLE_PAD, W), lambda i: (i, 0)), sspec],
        out_shape=[jax.ShapeDtypeStruct((n_seq * n_new, W), BF16),
                   jax.ShapeDtypeStruct((L, n_seq, H, LANES, LANES), F32)],
        input_output_aliases=aliases,
        scratch_shapes=[pltpu.VMEM((_heads_per_iter(H) * nb, SAMPLE_PAD, LANES), F32)],
        compiler_params=_cparams(("parallel",)),
        name=name,
    )(*args)


def _out_proj_kernel(split, x_is_pair, ap_ref, as_ref, w_ref, x_ref, *rest):
    xs_ref, o_ref = (rest[0], rest[1]) if x_is_pair else (None, rest[0])
    i = pl.program_id(0)
    last = pl.num_programs(0) - 1
    tail = lambda p_ref, s_ref: jnp.concatenate([p_ref[0:split, :], s_ref[...]], axis=0)

    @pl.when(i < last)
    def _():
        o_ref[...] = x_ref[...] + _wdot(ap_ref[...], w_ref)

    @pl.when(i == last)
    def _():
        x = tail(x_ref, xs_ref) if x_is_pair else x_ref[...]
        o_ref[...] = x + _wdot(tail(ap_ref, as_ref), w_ref)


def _out_proj(a_prompt, a_sample, w, x, tm, name):
    MP, K = a_prompt.shape
    MS = a_sample.shape[0]
    N = w.shape[1]
    M = MP + MS
    split = _row_split(MP, MS, tm)
    x_is_pair = isinstance(x, tuple)
    if x_is_pair:
        x_args = list(x)
        x_specs = [pl.BlockSpec((tm, N), lambda i: (i, 0)), pl.BlockSpec((MS, N), lambda i: (0, 0))]
    else:
        x_args, x_specs = [x], [pl.BlockSpec((tm, N), lambda i: (i, 0))]
    return pl.pallas_call(
        functools.partial(_out_proj_kernel, split, x_is_pair),
        grid=(M // tm,),
        in_specs=[pl.BlockSpec((tm, K), lambda i: (i, 0)),
                  pl.BlockSpec((MS, K), lambda i: (0, 0)),
                  pl.BlockSpec((K, N), lambda i: (0, 0))] + x_specs,
        out_specs=pl.BlockSpec((tm, N), lambda i: (i, 0)),
        out_shape=jax.ShapeDtypeStruct((M, N), F32),
        compiler_params=_cparams(("parallel",)),
        name=name,
    )(a_prompt, a_sample, w, *x_args)


def _ffn_kernel(split, x_ref, gn_ref, wg_ref, wu_ref, wd_ref, o_ref, *rest):
    ys_ref = rest[0] if split is not None else None
    h_scr = rest[-1]
    f = pl.program_id(1)

    @pl.when(f == 0)
    def _():
        _rms_to_scratch(x_ref, gn_ref, h_scr)
        o_ref[...] = x_ref[...]

    h = h_scr[...]
    g = _wdot(h, wg_ref)
    u = _wdot(h, wu_ref)
    a = (g * _sigmoid(g) * u).astype(BF16)
    o_ref[...] += _wdot(a, wd_ref)

    if split is not None:
        @pl.when((f == pl.num_programs(1) - 1) & (pl.program_id(0) == pl.num_programs(0) - 1))
        def _():
            ys_ref[...] = o_ref[split:, :]


def _ffn(x, gain, wg, wu, wd, tm, tf, name, prompt_rows=None):
    M, D = x.shape
    FF = wg.shape[1]
    o_spec = pl.BlockSpec((tm, D), lambda i, f: (i, 0))
    if prompt_rows is None:
        split, out_specs, out_shape = None, o_spec, jax.ShapeDtypeStruct((M, D), F32)
    else:
        split = prompt_rows % tm
        ms = M - prompt_rows
        assert tm - split == ms
        out_specs = [o_spec, pl.BlockSpec((ms, D), lambda i, f: (0, 0))]
        out_shape = [jax.ShapeDtypeStruct((prompt_rows, D), F32), jax.ShapeDtypeStruct((ms, D), F32)]
    return pl.pallas_call(
        functools.partial(_ffn_kernel, split),
        grid=(M // tm, FF // tf),
        in_specs=[pl.BlockSpec((tm, D), lambda i, f: (i, 0)),
                  pl.BlockSpec((1, D), lambda i, f: (0, 0)),
                  pl.BlockSpec((D, tf), lambda i, f: (0, f)),
                  pl.BlockSpec((D, tf), lambda i, f: (0, f)),
                  pl.BlockSpec((tf, D), lambda i, f: (f, 0))],
        out_specs=out_specs,
        out_shape=out_shape,
        scratch_shapes=[pltpu.VMEM((tm, D), BF16)],
        compiler_params=_cparams(("arbitrary", "arbitrary")),
        name=name,
    )(x, gain, wg, wu, wd)


MXU_COLS = 256


def _head_mean_square(acc):
    n = acc.shape[1]
    r = lax.broadcasted_iota(jnp.int32, (n, n), 0)
    c = lax.broadcasted_iota(jnp.int32, (n, n), 1)
    seg = jnp.where((r >> 6) == (c >> 6), 1.0 / ATTN_HEAD_DIM, 0.0).astype(BF16)
    return _dot((acc * acc).astype(BF16), seg)


def _norm_rope(acc, ms, gain, cos, sin, scale):
    y = acc * lax.rsqrt(ms + RMS_EPS) * gain
    lane = lax.broadcasted_iota(jnp.int32, y.shape, 1)
    half = ATTN_HEAD_DIM // 2
    rot = jnp.where((lane & half) == 0, pltpu.roll(y, LANES - half, 1), pltpu.roll(y, half, 1))
    out = y * cos + rot * sin
    return out * scale if scale != 1.0 else out


def _rope_cols(h, w_ref, hg_ref, cos_ref, sin_ref, o_ref, tn, scale):
    for n in range(w_ref.shape[1] // tn):
        acc = _dot(h, w_ref[:, n * tn:(n + 1) * tn])
        for c2 in range(tn // MXU_COLS):
            a2 = acc[:, c2 * MXU_COLS:(c2 + 1) * MXU_COLS]
            ms2 = _head_mean_square(a2)
            for c in range(MXU_COLS // LANES):
                cs = slice(c * LANES, (c + 1) * LANES)
                col = n * tn + c2 * MXU_COLS + c * LANES
                o_ref[:, col:col + LANES] = _norm_rope(a2[:, cs], ms2[:, cs], hg_ref[...], cos_ref[...],
                                                       sin_ref[...], scale).astype(BF16)


def _q_proj_kernel(tn, x_ref, gn_ref, w_ref, hg_ref, cos_ref, sin_ref, q_ref, h_scr):
    _rms_to_scratch(x_ref, gn_ref, h_scr)
    _rope_cols(h_scr[...], w_ref, hg_ref, cos_ref, sin_ref, q_ref, tn, ATTN_SCALE * LOG2E)


def _q_proj(x, gain, w, head_gain, cos, sin, tm, tn, name):
    M, D = x.shape
    N = w.shape[1]
    return pl.pallas_call(
        functools.partial(_q_proj_kernel, tn),
        grid=(M // tm,),
        in_specs=[pl.BlockSpec((tm, D), lambda i: (i, 0)),
                  pl.BlockSpec((1, D), lambda i: (0, 0)),
                  pl.BlockSpec((D, N), lambda i: (0, 0)),
                  pl.BlockSpec((1, LANES), lambda i: (0, 0)),
                  pl.BlockSpec((tm, LANES), lambda i: (i, 0)),
                  pl.BlockSpec((tm, LANES), lambda i: (i, 0))],
        out_specs=pl.BlockSpec((tm, N), lambda i: (i, 0)),
        out_shape=jax.ShapeDtypeStruct((M, N), BF16),
        scratch_shapes=[pltpu.VMEM((tm, D), BF16)],
        compiler_params=_cparams(("parallel",)),
        name=name,
    )(x, gain, w, head_gain, cos, sin)


def _kv_proj_kernel(tn, x_ref, gn_ref, wk_ref, wv_ref, hg_ref, cos_ref, sin_ref, k_ref, v_ref, h_scr):
    _rms_to_scratch(x_ref, gn_ref, h_scr)
    h = h_scr[...]
    _rope_cols(h, wk_ref, hg_ref, cos_ref, sin_ref, k_ref, tn, 1.0)
    for n in range(wv_ref.shape[1] // tn):
        ns = slice(n * tn, (n + 1) * tn)
        v_ref[:, ns] = _dot(h, wv_ref[:, ns]).astype(BF16)


def _kv_proj(x, gain, wk, wv, head_gain, cos, sin, tm, tn):
    M, D = x.shape
    N = wk.shape[1]
    wspec = pl.BlockSpec((D, N), lambda i: (0, 0))
    ospec = pl.BlockSpec((tm, N), lambda i: (i, 0))
    return pl.pallas_call(
        functools.partial(_kv_proj_kernel, tn),
        grid=(M // tm,),
        in_specs=[pl.BlockSpec((tm, D), lambda i: (i, 0)),
                  pl.BlockSpec((1, D), lambda i: (0, 0)),
                  wspec, wspec,
                  pl.BlockSpec((1, LANES), lambda i: (0, 0)),
                  pl.BlockSpec((tm, LANES), lambda i: (i, 0)),
                  pl.BlockSpec((tm, LANES), lambda i: (i, 0))],
        out_specs=[ospec, ospec],
        out_shape=[jax.ShapeDtypeStruct((M, N), BF16), jax.ShapeDtypeStruct((M, N), BF16)],
        scratch_shapes=[pltpu.VMEM((tm, D), BF16)],
        compiler_params=_cparams(("parallel",)),
        name="kv_proj",
    )(x, gain, wk, wv, head_gain, cos, sin)


def _softmax_sink(s_parts, sink2):
    m = sink2
    for s in s_parts:
        m = jnp.maximum(m, jnp.max(s, axis=-1, keepdims=True))
    den = jnp.exp2(sink2 - m)
    ps = []
    for s in s_parts:
        p = jnp.exp2(s - m)
        den = den + jnp.sum(p, axis=-1, keepdims=True)
        ps.append(p.astype(BF16))
    return ps, den


def _pv(ps, v_parts, den):
    o = None
    for p, v in zip(ps, v_parts):
        pv = _dot(p, v)
        o = pv if o is None else o + pv
    return o / den


def _swa_prompt_kernel(sk_ref, q_ref, kp_ref, kc_ref, vp_ref, vc_ref, o_ref):
    i = pl.program_id(1)
    W = WINDOW
    row = lax.broadcasted_iota(jnp.int32, (W, W), 0)
    col = lax.broadcasted_iota(jnp.int32, (W, W), 1)
    mask_prev = col > row + jnp.where(i > 0, 0, W)
    mask_cur = col <= row
    lo = lax.broadcasted_iota(jnp.int32, (W, LANES), 1) < ATTN_HEAD_DIM
    sink_col = col == 0
    first_row = lax.broadcasted_iota(jnp.int32, (W, LANES), 0) == 0

    def scores(g):
        gs = slice(g * LANES, (g + 1) * LANES)
        kd = jnp.concatenate([kp_ref[:, gs], kc_ref[:, gs]], axis=0)
        lhs = []
        for c in range(GQA_GROUP // 2):
            qc = q_ref[:, (2 * g + c) * LANES:(2 * g + c + 1) * LANES]
            zero = jnp.zeros_like(qc)
            lhs += [jnp.where(lo, qc, zero), jnp.where(lo, zero, qc)]
        s = _dot_nt(jnp.concatenate(lhs, axis=0), kd)
        out = []
        for a in range(GQA_GROUP):
            sa = s[a * W:(a + 1) * W]
            fill = jnp.where(sink_col, sk_ref[GQA_GROUP * g + a] * LOG2E, MASK_VALUE)
            out.append(jnp.concatenate([jnp.where(mask_prev, sa[:, :W], fill),
                                        jnp.where(mask_cur, sa[:, W:], MASK_VALUE)], axis=1))
        return out

    def finish(g, s_heads):
        gs = slice(g * LANES, (g + 1) * LANES)
        vprev = vp_ref[:, gs]
        vd = jnp.concatenate([jnp.where(first_row, jnp.zeros_like(vprev), vprev), vc_ref[:, gs]], axis=0)
        ps, dens = [], []
        for s in s_heads:
            m = jnp.max(s, axis=-1, keepdims=True)
            p = jnp.exp2(s - m)
            dens.append(jnp.sum(p, axis=-1, keepdims=True))
            ps.append(p.astype(BF16))
        o = _dot(jnp.concatenate(ps, axis=0), vd)
        for c in range(GQA_GROUP // 2):
            cs = slice((2 * g + c) * LANES, (2 * g + c + 1) * LANES)
            o0 = o[(2 * c) * W:(2 * c + 1) * W] / dens[2 * c]
            o1 = o[(2 * c + 1) * W:(2 * c + 2) * W] / dens[2 * c + 1]
            o_ref[:, cs] = jnp.where(lo, o0, o1).astype(BF16)

    s_next = scores(0)
    for g in range(ATTN_KV_HEADS):
        s_cur = s_next
        if g + 1 < ATTN_KV_HEADS:
            s_next = scores(g + 1)
        finish(g, s_cur)


def _swa_prompt(q, kdup, vdup, sinks, batch, seq, name):
    W = WINDOW
    nb = seq // W
    QW = q.shape[1]
    KW = kdup.shape[1]
    prev = lambda b, i: (b * nb + jnp.maximum(i - 1, 0), 0)
    cur = lambda b, i: (b * nb + i, 0)
    return pl.pallas_call(
        _swa_prompt_kernel,
        grid=(batch, nb),
        in_specs=[pl.BlockSpec(memory_space=pltpu.SMEM),
                  pl.BlockSpec((W, QW), cur),
                  pl.BlockSpec((W, KW), prev), pl.BlockSpec((W, KW), cur),
                  pl.BlockSpec((W, KW), prev), pl.BlockSpec((W, KW), cur)],
        out_specs=pl.BlockSpec((W, QW), cur),
        out_shape=jax.ShapeDtypeStruct((batch * seq, QW), BF16),
        compiler_params=_cparams(("parallel", "arbitrary")),
        name=name,
    )(sinks, q, kdup, kdup, vdup, vdup)


def _swa_sample_kernel(n_new, sk_ref, q_ref, kn_ref, vn_ref, kc_ref, vc_ref, o_ref):
    W = WINDOW
    nb = SAMPLE_PAD // n_new
    rows = GQA_GROUP * n_new
    qi = lax.broadcasted_iota(jnp.int32, (rows, W), 0) & (n_new - 1)
    mask_c = lax.broadcasted_iota(jnp.int32, (rows, W), 1) > qi
    qi_n = lax.broadcasted_iota(jnp.int32, (rows, SAMPLE_PAD), 0) & (n_new - 1)
    mask_n = lax.broadcasted_iota(jnp.int32, (rows, SAMPLE_PAD), 1) <= qi_n
    hrow = lax.broadcasted_iota(jnp.int32, (rows, 1), 0) // n_new
    lo = lax.broadcasted_iota(jnp.int32, (n_new, LANES), 1) < ATTN_HEAD_DIM
    zq = jnp.zeros((n_new, LANES), F32)
    zpad = jnp.zeros((SAMPLE_PAD - n_new, LANES), F32)
    q = q_ref[...].astype(F32)
    kn = kn_ref[...].astype(F32)
    vn = vn_ref[...].astype(F32)
    chains = [(bb, g) for bb in range(nb) for g in range(ATTN_KV_HEADS)]
    sinks2 = []
    for g in range(ATTN_KV_HEADS):
        sink = jnp.zeros((rows, 1), F32)
        for a in range(GQA_GROUP):
            sink = jnp.where(hrow == a, sk_ref[GQA_GROUP * g + a] * LOG2E, sink)
        sinks2.append(sink)
    scores, values = [], []
    for bb, g in chains:
        rs = slice(bb * n_new, (bb + 1) * n_new)
        gs = slice(g * LANES, (g + 1) * LANES)
        kn_g = jnp.concatenate([kn[rs, gs], zpad], axis=0).astype(BF16)
        vn_g = jnp.concatenate([vn[rs, gs], zpad], axis=0).astype(BF16)
        lhs = []
        for a in range(GQA_GROUP):
            c, par = divmod(a, 2)
            qc = q[rs, (2 * g + c) * LANES:(2 * g + c + 1) * LANES]
            lhs.append(jnp.where(lo, qc, zq) if par == 0 else jnp.where(lo, zq, qc))
        lhs = jnp.concatenate(lhs, axis=0).astype(BF16)
        s_c = jnp.where(mask_c, _dot_nt(lhs, kc_ref[bb, :, gs]), MASK_VALUE)
        s_n = jnp.where(mask_n, _dot_nt(lhs, kn_g), MASK_VALUE)
        scores.append([s_c, s_n])
        values.append([vc_ref[bb, :, gs], vn_g])
    probs = [_softmax_sink(s, sinks2[g]) for s, (bb, g) in zip(scores, chains)]
    outs = [_pv(ps, v, den) for (ps, den), v in zip(probs, values)]
    out_rows = []
    for bb in range(nb):
        chunks = []
        for g in range(ATTN_KV_HEADS):
            o = outs[bb * ATTN_KV_HEADS + g]
            for c in range(GQA_GROUP // 2):
                chunks.append(jnp.where(lo, o[(2 * c) * n_new:(2 * c + 1) * n_new],
                                        o[(2 * c + 1) * n_new:(2 * c + 2) * n_new]))
        out_rows.append(jnp.concatenate(chunks, axis=1))
    o_ref[...] = jnp.concatenate(out_rows, axis=0).astype(BF16)


def _swa_sample(q, kdup, vdup, cache_kdup, cache_vdup, sinks, row0, n_seq, n_new, name):
    W = WINDOW
    QW = q.shape[1]
    KW = kdup.shape[1]
    nb = SAMPLE_PAD // n_new
    blk0 = row0 // SAMPLE_PAD
    rmap = lambda i: (blk0 + i, 0)
    cspec = pl.BlockSpec((nb, W, KW), lambda i: (i, 0, 0))
    return pl.pallas_call(
        functools.partial(_swa_sample_kernel, n_new),
        grid=(n_seq // nb,),
        in_specs=[pl.BlockSpec(memory_space=pltpu.SMEM),
                  pl.BlockSpec((SAMPLE_PAD, QW), rmap),
                  pl.BlockSpec((SAMPLE_PAD, KW), rmap), pl.BlockSpec((SAMPLE_PAD, KW), rmap),
                  cspec, cspec],
        out_specs=pl.BlockSpec((SAMPLE_PAD, QW), lambda i: (i, 0)),
        out_shape=jax.ShapeDtypeStruct((n_seq * n_new, QW), BF16),
        compiler_params=_cparams(("parallel",)),
        name=name,
    )(sinks, q, kdup, vdup, cache_kdup, cache_vdup)


def _dup_heads(w):
    lead = w.shape[:-1]
    w3 = w.reshape(lead + (ATTN_KV_HEADS, ATTN_HEAD_DIM))
    return jnp.concatenate([w3, w3], axis=-1).reshape(lead + (ATTN_KV_HEADS * 2 * ATTN_HEAD_DIM,))


def _undup_heads(a, lead):
    return a.reshape(lead + (ATTN_KV_HEADS, 2 * ATTN_HEAD_DIM))[..., :ATTN_HEAD_DIM].astype(F32)


def _rope_tables(pos):
    half = ATTN_HEAD_DIM // 2
    inv = ROPE_THETA ** (-jnp.arange(half, dtype=F32) / half)
    ang = pos.astype(F32)[:, None] * inv[None, :]
    cos = jnp.cos(ang)
    sin = jnp.sin(ang)
    reps = LANES // ATTN_HEAD_DIM
    cos_t = jnp.tile(jnp.concatenate([cos, cos], axis=-1), (1, reps))
    sin_t = jnp.tile(jnp.concatenate([-sin, sin], axis=-1), (1, reps))
    return cos_t, sin_t


def kernel(x_prompt, x_sample, state_hgrn, cache_k_win, cache_v_win, hgrn_norm, hgrn_wq, hgrn_wf, hgrn_wi, hgrn_wg, hgrn_lb_logits, hgrn_onorm, hgrn_wo, kv_norm, w_k, w_v, k_norm, attn_norm, attn_wq, q_norm, sinks, attn_wo, ffn_norm, w_gate, w_up, w_down):
    B, T, D = x_prompt.shape
    SB, S, _ = x_sample.shape
    MP = B * T
    MS = SB * S
    n_a = hgrn_wq.shape[0]
    n_b = attn_wq.shape[0]
    assert SAMPLE_PAD % S == 0 and SB % (SAMPLE_PAD // S) == 0 and MP % SAMPLE_PAD == 0
    assert T % GLA_CHUNK == 0 and T % WINDOW == 0

    x = (x_prompt.reshape(MP, D).astype(F32), x_sample.reshape(MS, D).astype(F32))
    M = MP + MS
    tm = 768 if M % 768 == 0 else 256
    tn = 512
    tf = 512

    bf = lambda w: w.astype(BF16)
    f32 = lambda w: w.astype(F32)
    row = lambda g: g.reshape(1, -1).astype(F32)
    pos = jnp.concatenate([jnp.tile(jnp.arange(T), B), jnp.tile(PAST_LEN + jnp.arange(S), SB)])
    cos_t, sin_t = _rope_tables(pos)
    head_gain = lambda g: jnp.tile(g.astype(F32), LANES // ATTN_HEAD_DIM).reshape(1, LANES)

    w_gate32, w_up32, w_down32 = f32(w_gate), f32(w_up), f32(w_down)
    hgrn_w32 = [f32(hgrn_wq), f32(hgrn_wf), f32(hgrn_wi), f32(hgrn_wg)]
    hgrn_wo32, attn_wq32, attn_wo32 = f32(hgrn_wo), f32(attn_wq), f32(attn_wo)
    ffn_riders = lambda l: [(w_gate32, l), (w_up32, l), (w_down32, l)]

    depth = n_a + n_b
    ffn_bf = {}
    hosted = {l: list(range(l * depth // n_a, (l + 1) * depth // n_a)) for l in range(n_a)}
    st_prompt, st_sample = [], None
    s0 = f32(state_hgrn)
    assert n_a >= 1
    proj_w = _cast_weights([(w, 0) for w in hgrn_w32], 16, "cast_hgrn_0")
    attn_wq_bf, attn_wo_bf = [], []
    for l in range(n_a):
        q, k, lf, v, gate = _hgrn_proj(x, row(hgrn_norm[l]), *proj_w, f32(hgrn_lb_logits), l, tm, tn)
        gon = row(hgrn_onorm[l])
        riders = [(hgrn_wo32, l)] + [r for fl in hosted[l] for r in ffn_riders(fl)]
        if l + 1 < n_a:
            riders += [(w, l + 1) for w in hgrn_w32]
        else:
            riders += [(attn_wq32, j) for j in range(n_b)] + [(attn_wo32, j) for j in range(n_b)]
        o_p, s_p, wo_bf, *cast = _gla_prompt(q, k, lf, v, gate, gon, B, T, riders, f"gla_prompt_{l}")
        for fl in hosted[l]:
            ffn_bf[fl], cast = cast[:3], cast[3:]
        if l + 1 < n_a:
            proj_w = cast
        else:
            attn_wq_bf, attn_wo_bf = cast[:n_b], cast[n_b:]
        o_s, st_sample = _gla_sample(q, k, lf, v, gate, gon, s0, l, MP, SB, S, f"gla_sample_{l}",
                                     states=st_sample)
        st_prompt.append(s_p)
        x = _out_proj(o_p, o_s, wo_bf, x, tm, f"hgrn_out_{l}")
        x = _ffn(x, row(ffn_norm[l]), *ffn_bf.pop(l), tm, tf, f"ffn_{l}")

    kdup, vdup = _kv_proj(x, row(kv_norm), bf(_dup_heads(w_k)), bf(_dup_heads(w_v)), head_gain(k_norm),
                          cos_t, sin_t, tm, tn)
    KW = ATTN_KV_HEADS * ATTN_HEAD_DIM
    cache_kdup = bf(_dup_heads(cache_k_win.reshape(SB, WINDOW, KW)))
    cache_vdup = bf(_dup_heads(cache_v_win.reshape(SB, WINDOW, KW)))
    for j in range(n_b):
        l = n_a + j
        q = _q_proj(x, row(attn_norm[j]), attn_wq_bf[j], head_gain(q_norm[j]), cos_t, sin_t, tm, tn,
                    f"q_proj_{j}")
        sk = sinks[j].astype(F32)
        a_p = _swa_prompt(q, kdup, vdup, sk, B, T, f"swa_prompt_{j}")
        a_s = _swa_sample(q, kdup, vdup, cache_kdup, cache_vdup, sk, MP, SB, S, f"swa_sample_{j}")
        x = _out_proj(a_p, a_s, attn_wo_bf[j], x, tm, f"attn_out_{j}")
        last = j == n_b - 1
        x = _ffn(x, row(ffn_norm[l]), *ffn_bf.pop(l), tm, tf, f"ffn_{l}", prompt_rows=MP if last else None)

    y_prompt = x[0].reshape(B, T, D)
    y_sample = x[1].reshape(SB, S, D)
    k_win_p = _undup_heads(kdup[:MP].reshape(B, T, -1)[:, T - WINDOW:], (B, WINDOW))
    v_win_p = _undup_heads(vdup[:MP].reshape(B, T, -1)[:, T - WINDOW:], (B, WINDOW))
    k_new_s = _undup_heads(kdup[MP:].reshape(SB, S, -1), (SB, S))
    v_new_s = _undup_heads(vdup[MP:].reshape(SB, S, -1), (SB, S))
    k_win_s = jnp.concatenate([cache_k_win[:, S:].astype(F32), k_new_s], axis=1)
    v_win_s = jnp.concatenate([cache_v_win[:, S:].astype(F32), v_new_s], axis=1)
    return (y_prompt, y_sample, jnp.stack(st_prompt), st_sample,
            k_win_p, v_win_p, k_win_s, v_win_s)
```
